```python
import math
import jax, jax.numpy as jnp
from jax import lax
import numpy as np

D_MODEL = 2048
BATCH = 4
SEQ = 2048
DEPTH = 1
DEC_BATCH = 128
DEC_SEQ = 1
PAST_LEN = 16384
PAGE_SIZE = 128

SSD_HEADS = 32
SSD_HEAD_DIM = 64
SSD_WIDTH = SSD_HEADS * SSD_HEAD_DIM
SSD_STATE = 128
SSD_GROUPS = 4
SSD_HPG = SSD_HEADS // SSD_GROUPS
SSD_CONV = 4
SSD_CHUNK = 128
SSD_CONV_DIM = SSD_WIDTH + 2 * SSD_GROUPS * SSD_STATE

SC_WIDTH = 2048
SC_CONV = 3

PEER_HEADS = 8
PEER_KEYS = 128
PEER_EXPERTS = PEER_KEYS * PEER_KEYS
PEER_DKEY = 256
PEER_TOPK = 16
PEER_BLOCK = 128

PLE_DIM = 256
ALPHA = (2 * DEPTH) ** 0.25
BETA = (8 * DEPTH) ** -0.25
LN_EPS = 1e-5
RMS_EPS = 1e-5

IN_SPLITS = [SSD_WIDTH, SSD_CONV_DIM, SSD_HEADS, SC_WIDTH, SC_WIDTH, SC_WIDTH, D_MODEL, D_MODEL]
IN_COLS = sum(IN_SPLITS)
IN_OFFSETS = list(np.cumsum(IN_SPLITS)[:-1].tolist())

kernel_name = "hybrid_ssd_shortconv_peer_step"


def layer_norm(x, g, b):
    xf = x.astype(jnp.float32)
    mu = jnp.mean(xf, -1, keepdims=True)
    var = jnp.mean(jnp.square(xf - mu), -1, keepdims=True)
    return ((xf - mu) * lax.rsqrt(var + LN_EPS) * g.astype(jnp.float32) + b.astype(jnp.float32)).astype(x.dtype)


def causal_dwconv(x, buf, w):
    k = w.shape[0]
    t = x.shape[1]
    xx = jnp.concatenate([buf.astype(x.dtype), x], axis=1)
    y = xx[:, 0:t] * w[0]
    for j in range(1, k):
        y = y + xx[:, j:j + t] * w[j]
    return y, xx[:, t:]


def ssd_scan(X, A, Bm, Cm, h0):
    b, T = X.shape[:2]
    L = SSD_CHUNK if T % SSD_CHUNK == 0 else T
    c = T // L
    G, R, P, N = SSD_GROUPS, SSD_HPG, SSD_HEAD_DIM, SSD_STATE
    X = X.reshape(b, c, L, G, R, P)
    A = A.reshape(b, c, L, G, R)
    Bc = Bm.reshape(b, c, L, G, N)
    Cc = Cm.reshape(b, c, L, G, N)
    Acs = jnp.cumsum(A, axis=2)
    At = jnp.moveaxis(Acs, 2, -1)
    seg = At[..., :, None] - At[..., None, :]
    causal = jnp.tril(jnp.ones((L, L), dtype=bool))
    Lmat = jnp.exp(jnp.where(causal, seg, -jnp.inf))
    CB = jnp.einsum('bclgn,bcsgn->bcgls', Cc, Bc)
    y_diag = jnp.einsum('bcgls,bcgrls,bcsgrp->bclgrp', CB, Lmat, X)
    decay = jnp.exp(Acs[:, :, -1:] - Acs)
    states = jnp.einsum('bclgn,bclgr,bclgrp->bcgrpn', Bc, decay, X)
    chunk_decay = jnp.exp(Acs[:, :, -1])

    def step(h, inp):
        s_c, d_c = inp
        return h * d_c[..., None, None] + s_c, h

    h_final, h_prev = lax.scan(step, h0, (jnp.moveaxis(states, 1, 0), jnp.moveaxis(chunk_decay, 1, 0)))
    h_prev = jnp.moveaxis(h_prev, 0, 1)
    y_off = jnp.einsum('bclgn,bcgrpn,bclgr->bclgrp', Cc, h_prev, jnp.exp(Acs))
    return (y_diag + y_off).reshape(b, T, G, R, P), h_final


def ssd_branch(z, xbc, dt, conv_buf, h0, conv_w, conv_b, dt_bias, a_log, d_skip, norm_w):
    b, T = z.shape[:2]
    G, R, P, N = SSD_GROUPS, SSD_HPG, SSD_HEAD_DIM, SSD_STATE
    xbc, new_buf = causal_dwconv(xbc, conv_buf, conv_w)
    xbc = jax.nn.silu(xbc + conv_b).astype(jnp.float32)
    xs = xbc[..., :SSD_WIDTH].reshape(b, T, G, R, P)
    Bm = xbc[..., SSD_WIDTH:SSD_WIDTH + G * N].reshape(b, T, G, N)
    Cm = xbc[..., SSD_WIDTH + G * N:].reshape(b, T, G, N)
    dtf = jax.nn.softplus(dt.astype(jnp.float32) + dt_bias.astype(jnp.float32)).reshape(b, T, G, R)
    A = -jnp.exp(a_log.astype(jnp.float32)).reshape(G, R)
    h0g = h0.astype(jnp.float32).reshape(b, G, R, P, N)
    y, h = ssd_scan(xs * dtf[..., None], dtf * A, Bm, Cm, h0g)
    y = y + xs * d_skip.astype(jnp.float32).reshape(G, R)[:, :, None]
    y = y.reshape(b, T, G, R * P) * jax.nn.silu(z.astype(jnp.float32)).reshape(b, T, G, R * P)
    y = y * lax.rsqrt(jnp.mean(y * y, -1, keepdims=True) + RMS_EPS)
    y = y.reshape(b, T, SSD_WIDTH) * norm_w.astype(jnp.float32)
    return y.astype(z.dtype), new_buf, h.reshape(b, SSD_HEADS, P, N)


def shortconv_branch(gate_b, gate_c, h, conv_buf, conv_w):
    u, new_buf = causal_dwconv(gate_c * h, conv_buf, conv_w)
    return gate_b * u, new_buf


def peer(x, w_q, keys1, keys2, u_tab, v_tab):
    b, T, D = x.shape
    n = b * T
    n_pad = -(-n // PEER_BLOCK) * PEER_BLOCK
    xt = jnp.pad(x.reshape(n, D), ((0, n_pad - n), (0, 0)))
    q = (xt @ w_q).astype(jnp.float32).reshape(n_pad, PEER_HEADS, 2, PEER_DKEY // 2)
    s1 = jnp.einsum('thd,hkd->thk', q[:, :, 0], keys1.astype(jnp.float32))
    s2 = jnp.einsum('thd,hkd->thk', q[:, :, 1], keys2.astype(jnp.float32))
    t1, i1 = lax.top_k(s1, PEER_TOPK)
    t2, i2 = lax.top_k(s2, PEER_TOPK)
    cand = (t1[..., :, None] + t2[..., None, :]).reshape(n_pad, PEER_HEADS, PEER_TOPK * PEER_TOPK)
    cand_idx = (i1[..., :, None] * PEER_KEYS + i2[..., None, :]).reshape(n_pad, PEER_HEADS, PEER_TOPK * PEER_TOPK)
    top, pos = lax.top_k(cand, PEER_TOPK)
    idx = jnp.take_along_axis(cand_idx, pos, axis=-1)
    gate = jax.nn.softmax(top, axis=-1)

    def block(args):
        xb, ib, gb = args
        hid = jax.nn.gelu(jnp.einsum('td,thkd->thk', xb, u_tab[ib]).astype(jnp.float32), approximate=False)
        coef = (gb * hid).astype(xb.dtype)
        return jnp.einsum('thk,thkd->td', coef, v_tab[ib])

    nb = n_pad // PEER_BLOCK
    out = lax.map(block, (xt.reshape(nb, PEER_BLOCK, D),
                          idx.reshape(nb, PEER_BLOCK, PEER_HEADS, PEER_TOPK),
                          gate.reshape(nb, PEER_BLOCK, PEER_HEADS, PEER_TOPK)))
    return out.reshape(n_pad, D)[:n].reshape(b, T, D)


def trunk_layer(x, p, ssd_conv_buf, ssm_h, sc_buf,
                w_in, ssd_conv_w, ssd_conv_b, ssd_dt_bias, ssd_a_log, ssd_d, ssd_norm_w,
                sc_conv_w, w_branch_ssd, w_branch_sc, w_out, ln1_g, ln1_b,
                peer_wq, peer_keys1, peer_keys2, peer_u, peer_v, ln2_g, ln2_b,
                ple_gate_w, ple_proj_w):
    proj = x @ w_in
    z, xbc, dt, sc_b, sc_c, sc_h, g_a, g_b = jnp.split(proj, IN_OFFSETS, axis=-1)
    y_a, new_ssd_conv, new_h = ssd_branch(z, xbc, dt, ssd_conv_buf, ssm_h, ssd_conv_w, ssd_conv_b,
                                          ssd_dt_bias, ssd_a_log, ssd_d, ssd_norm_w)
    y_b, new_sc = shortconv_branch(sc_b, sc_c, sc_h, sc_buf, sc_conv_w)
    mix = jax.nn.sigmoid(g_a) * (y_a @ w_branch_ssd) + jax.nn.sigmoid(g_b) * (y_b @ w_branch_sc)
    x1 = layer_norm(ALPHA * x + mix @ w_out, ln1_g, ln1_b)
    x2 = layer_norm(ALPHA * x1 + peer(x1, peer_wq, peer_keys1, peer_keys2, peer_u, peer_v), ln2_g, ln2_b)
    y = x2 + jax.nn.sigmoid(x2 @ ple_gate_w) * (p @ ple_proj_w)
    return y, new_ssd_conv, new_h, new_sc


def setup_inputs(seed: int = 0) -> dict:
    key = jax.random.key(seed)
    ks = jax.random.split(key, 32)
    f32 = jnp.float32
    nrm = lambda k, shape, s: jax.random.normal(k, shape, f32) * s
    dt0 = jnp.exp(jax.random.uniform(ks[10], (DEPTH, SSD_HEADS), f32) * (math.log(0.1) - math.log(0.001)) + math.log(0.001))
    return {
        "x_prompt": nrm(ks[0], (BATCH, SEQ, D_MODEL), 1.0),
        "x_sample": nrm(ks[1], (DEC_BATCH, DEC_SEQ, D_MODEL), 1.0),
        "p_prompt": nrm(ks[2], (DEPTH, BATCH, SEQ, PLE_DIM), 1.0),
        "p_sample": nrm(ks[3], (DEPTH, DEC_BATCH, DEC_SEQ, PLE_DIM), 1.0),
        "state_ssm": nrm(ks[4], (DEPTH, DEC_BATCH, SSD_HEADS, SSD_HEAD_DIM, SSD_STATE), 0.1),
        "state_ssd_conv": nrm(ks[5], (DEPTH, DEC_BATCH, SSD_CONV - 1, SSD_CONV_DIM), 1.0),
        "state_shortconv": nrm(ks[6], (DEPTH, DEC_BATCH, SC_CONV - 1, SC_WIDTH), 1.0),
        "w_in": nrm(ks[7], (DEPTH, D_MODEL, IN_COLS), D_MODEL ** -0.5),
        "ssd_conv_w": nrm(ks[8], (DEPTH, SSD_CONV, SSD_CONV_DIM), SSD_CONV ** -0.5),
        "ssd_conv_b": nrm(ks[9], (DEPTH, SSD_CONV_DIM), 0.02),
        "ssd_dt_bias": dt0 + jnp.log(-jnp.expm1(-dt0)),
        "ssd_a_log": jnp.log(jax.random.uniform(ks[11], (DEPTH, SSD_HEADS), f32, 1.0, 16.0)),
        "ssd_d": 1.0 + nrm(ks[12], (DEPTH, SSD_HEADS), 0.02),
        "ssd_norm_w": 1.0 + nrm(ks[13], (DEPTH, SSD_WIDTH), 0.02),
        "sc_conv_w": nrm(ks[14], (DEPTH, SC_CONV, SC_WIDTH), SC_CONV ** -0.5),
        "w_branch_ssd": nrm(ks[15], (DEPTH, SSD_WIDTH, D_MODEL), SSD_WIDTH ** -0.5),
        "w_branch_sc": nrm(ks[16], (DEPTH, SC_WIDTH, D_MODEL), SC_WIDTH ** -0.5),
        "w_out": nrm(ks[17], (DEPTH, D_MODEL, D_MODEL), BETA * D_MODEL ** -0.5),
        "ln1_g": 1.0 + nrm(ks[18], (DEPTH, D_MODEL), 0.02),
        "ln1_b": nrm(ks[19], (DEPTH, D_MODEL), 0.02),
        "peer_wq": nrm(ks[20], (DEPTH, D_MODEL, PEER_HEADS * PEER_DKEY), D_MODEL ** -0.5),
        "peer_keys1": nrm(ks[21], (DEPTH, PEER_HEADS, PEER_KEYS, PEER_DKEY // 2), (PEER_DKEY // 2) ** -0.5),
        "peer_keys2": nrm(ks[22], (DEPTH, PEER_HEADS, PEER_KEYS, PEER_DKEY // 2), (PEER_DKEY // 2) ** -0.5),
        "peer_u": nrm(ks[23], (DEPTH, PEER_EXPERTS, D_MODEL), D_MODEL ** -0.5),
        "peer_v": nrm(ks[24], (DEPTH, PEER_EXPERTS, D_MODEL), BETA * PEER_HEADS ** -0.5),
        "ln2_g": 1.0 + nrm(ks[25], (DEPTH, D_MODEL), 0.02),
        "ln2_b": nrm(ks[26], (DEPTH, D_MODEL), 0.02),
        "ple_gate_w": nrm(ks[27], (DEPTH, D_MODEL, D_MODEL), D_MODEL ** -0.5),
        "ple_proj_w": nrm(ks[28], (DEPTH, PLE_DIM, D_MODEL), PLE_DIM ** -0.5),
    }


def reference(x_prompt, x_sample, p_prompt, p_sample, state_ssm, state_ssd_conv, state_shortconv,
              w_in, ssd_conv_w, ssd_conv_b, ssd_dt_bias, ssd_a_log, ssd_d, ssd_norm_w,
              sc_conv_w, w_branch_ssd, w_branch_sc, w_out, ln1_g, ln1_b,
              peer_wq, peer_keys1, peer_keys2, peer_u, peer_v, ln2_g, ln2_b,
              ple_gate_w, ple_proj_w):
    bp = x_prompt.shape[0]
    yp, ys = x_prompt, x_sample
    hp_l, cp_l, sp_l, hs_l, cs_l, ss_l = [], [], [], [], [], []
    for i in range(DEPTH):
        lw = (w_in[i], ssd_conv_w[i], ssd_conv_b[i], ssd_dt_bias[i], ssd_a_log[i], ssd_d[i], ssd_norm_w[i],
              sc_conv_w[i], w_branch_ssd[i], w_branch_sc[i], w_out[i], ln1_g[i], ln1_b[i],
              peer_wq[i], peer_keys1[i], peer_keys2[i], peer_u[i], peer_v[i], ln2_g[i], ln2_b[i],
              ple_gate_w[i], ple_proj_w[i])
        yp, cp, hp, sp = trunk_layer(
            yp, p_prompt[i],
            jnp.zeros((bp, SSD_CONV - 1, SSD_CONV_DIM), x_prompt.dtype),
            jnp.zeros((bp, SSD_HEADS, SSD_HEAD_DIM, SSD_STATE), jnp.float32),
            jnp.zeros((bp, SC_CONV - 1, SC_WIDTH), x_prompt.dtype), *lw)
        ys, cs, hs, ss = trunk_layer(ys, p_sample[i], state_ssd_conv[i], state_ssm[i], state_shortconv[i], *lw)
        hp_l.append(hp); cp_l.append(cp); sp_l.append(sp)
        hs_l.append(hs); cs_l.append(cs); ss_l.append(ss)
    return (yp, ys, jnp.stack(hp_l), jnp.stack(cp_l), jnp.stack(sp_l), jnp.stack(hs_l), jnp.stack(cs_l), jnp.stack(ss_l))
```

```python
import functools
import math

import jax
import jax.numpy as jnp
from jax import lax
from jax.experimental import pallas as pl
from jax.experimental.pallas import tpu as pltpu

F32 = jnp.float32
BF16 = jnp.bfloat16

LANES = 128
SUBLANES = 8
PEER_TOPK = 16
SSD_CHUNK = 128
LN_EPS = 1e-5
RMS_EPS = 1e-5
VMEM_LIMIT = 56 * 1024 * 1024

NT_DIMS = (((1,), (1,)), ((), ()))
TN_DIMS = (((0,), (0,)), ((), ()))


def _cparams(*sem):
    return pltpu.CompilerParams(dimension_semantics=sem, vmem_limit_bytes=VMEM_LIMIT)


def _dot(a, b):
    return jnp.dot(a, b, preferred_element_type=F32)


def _split3(v):
    hi = v.astype(BF16)
    r = v - hi.astype(F32)
    mid = r.astype(BF16)
    lo = (r - mid.astype(F32)).astype(BF16)
    return hi, mid, lo


def _dot3_lhs(v, rhs_bf16):
    hi, mid, lo = _split3(v)
    return _dot(hi, rhs_bf16) + _dot(mid, rhs_bf16) + _dot(lo, rhs_bf16)


def _dot3_rhs(lhs_bf16, v):
    hi, mid, lo = _split3(v)
    return _dot(lhs_bf16, hi) + _dot(lhs_bf16, mid) + _dot(lhs_bf16, lo)


def _sigmoid(x):
    return 1.0 / (1.0 + jnp.exp(-x))


def _silu(x):
    return x * _sigmoid(x)


def _softplus(x):
    return jnp.maximum(x, 0.0) + jnp.log1p(jnp.exp(-jnp.abs(x)))


def _layer_norm(x, g, b):
    mu = jnp.mean(x, axis=-1, keepdims=True)
    xc = x - mu
    var = jnp.mean(xc * xc, axis=-1, keepdims=True)
    return xc * lax.rsqrt(var + LN_EPS) * g + b


def _mm_kernel(x_ref, w_ref, o_ref):
    o_ref[...] = _dot(x_ref[...].astype(BF16), w_ref[...]).astype(o_ref.dtype)


def _matmul(x, w, tm, tn, out_dtype):
    m, k = x.shape
    n = w.shape[1]
    return pl.pallas_call(
        _mm_kernel,
        grid=(m // tm, n // tn),
        in_specs=[pl.BlockSpec((tm, k), lambda i, j: (i, 0)),
                  pl.BlockSpec((k, tn), lambda i, j: (0, j))],
        out_specs=pl.BlockSpec((tm, tn), lambda i, j: (i, j)),
        out_shape=jax.ShapeDtypeStruct((m, n), out_dtype),
        compiler_params=_cparams("parallel", "parallel"),
        name="in_proj",
    )(x, w)


def _ssd_gate_norm(y, z, normw_ref, out_ref, n_groups):
    y = y * _silu(z)
    gw = y.shape[1] // n_groups
    for g in range(n_groups):
        sl = slice(g * gw, (g + 1) * gw)
        yg = y[:, sl]
        ms = jnp.mean(yg * yg, axis=-1, keepdims=True)
        out_ref[:, sl] = (yg * lax.rsqrt(ms + RMS_EPS) * normw_ref[:, sl]).astype(out_ref.dtype)


def _ssd_prompt_kernel(z_ref, xs_ref, b_ref, c_ref, dt_ref, convw_ref, convb_ref, dtb_ref,
                       alog_ref, de_ref, normw_ref, e1_ref, e2_ref,
                       ya_ref, hout_ref,
                       cbuf, act, h_scr, x_scr, xd_scr, eae_scr, acsb_scr, acst_scr, y_scr,
                       *, n_groups, n_state, head_dim):
    c = pl.program_id(1)
    L = SSD_CHUNK
    hw = xs_ref.shape[1]
    gn = b_ref.shape[1]
    w_all = hw + 2 * gn
    n_heads = hw // head_dim
    hpl = LANES // head_dim
    n_blk = n_heads // hpl
    blk_per_group = n_blk // n_groups
    taps = convw_ref.shape[0]

    @pl.when(c == 0)
    def _():
        h_scr[...] = jnp.zeros_like(h_scr)
        cbuf[0:SUBLANES, :] = jnp.zeros((SUBLANES, w_all), F32)

    cbuf[SUBLANES:SUBLANES + L, 0:hw] = xs_ref[...]
    cbuf[SUBLANES:SUBLANES + L, hw:hw + gn] = b_ref[...]
    cbuf[SUBLANES:SUBLANES + L, hw + gn:] = c_ref[...]

    cw = math.gcd(w_all, 512)
    for blk in range(w_all // cw):
        sl = slice(blk * cw, (blk + 1) * cw)
        acc = convb_ref[:, sl] + convw_ref[taps - 1:taps, sl] * cbuf[SUBLANES:SUBLANES + L, sl]
        for j in range(1, taps):
            acc = acc + convw_ref[taps - 1 - j:taps - j, sl] * cbuf[SUBLANES - j:SUBLANES - j + L, sl]
        act[:, sl] = _silu(acc)
    cbuf[0:SUBLANES, :] = cbuf[L:L + SUBLANES, :]

    dt = _softplus(dt_ref[...] + dtb_ref[...])
    a_neg = -jnp.exp(alog_ref[...])
    dta = dt * a_neg
    ri = lax.broadcasted_iota(jnp.int32, (L, L), 0)
    ci = lax.broadcasted_iota(jnp.int32, (L, L), 1)
    causal = ri >= ci
    tri = jnp.where(causal, 1.0, 0.0).astype(BF16)
    acs = _dot3_rhs(tri, dta)
    e1 = e1_ref[...]
    dte = _dot3_lhs(dt, e1)
    acs_p = _split3(acs)
    acse = _dot(acs_p[0], e1) + _dot(acs_p[1], e1) + _dot(acs_p[2], e1)
    e2 = e2_ref[...]
    acsb_scr[...] = _dot(acs_p[0], e2) + _dot(acs_p[1], e2) + _dot(acs_p[2], e2)
    acst_scr[...] = acs.T

    xdt = act[:, 0:hw] * dte
    x_scr[...] = xdt.astype(BF16)
    xd_scr[...] = (xdt * jnp.exp(acse[L - 1:L, :] - acse)).astype(BF16)
    eae_scr[...] = jnp.exp(acse)

    lane = lax.broadcasted_iota(jnp.int32, (L, LANES), 1)
    cb = None
    for j in range(n_blk):
        g = j // blk_per_group
        bsl = slice(hw + g * n_state, hw + (g + 1) * n_state)
        csl = slice(hw + gn + g * n_state, hw + gn + (g + 1) * n_state)
        bg = act[:, bsl].astype(BF16)
        cg = act[:, csl].astype(BF16)
        if j % blk_per_group == 0:
            cb = lax.dot_general(cg, bg, NT_DIMS, preferred_element_type=F32)
        psl = slice(j * LANES, (j + 1) * LANES)
        xp = x_scr[:, psl]
        ydiag = None
        cds = []
        for q in range(hpl):
            r = j * hpl + q
            ab = acsb_scr[:, r * LANES:(r + 1) * LANES]
            at = jnp.broadcast_to(acst_scr[r:r + 1, :], (L, L))
            lm = jnp.where(causal, jnp.exp(ab - at), 0.0)
            m = (cb * lm).astype(BF16)
            inhead = (lane >= q * head_dim) & (lane < (q + 1) * head_dim)
            xq = jnp.where(inhead, xp, jnp.zeros_like(xp))
            yq = _dot(m, xq)
            ydiag = yq if ydiag is None else ydiag + yq
            cds.append(jnp.broadcast_to(jnp.exp(acsb_scr[L - 1:L, r * LANES:(r + 1) * LANES]),
                                        (head_dim, LANES)))
        cd = jnp.concatenate(cds, axis=0)
        hp = h_scr[psl, :]
        yoff = lax.dot_general(cg, hp.astype(BF16), NT_DIMS, preferred_element_type=F32)
        yoff = yoff * eae_scr[:, psl]
        st = lax.dot_general(xd_scr[:, psl], bg, TN_DIMS, preferred_element_type=F32)
        h_scr[psl, :] = hp * cd + st
        y_scr[:, psl] = ydiag + yoff + act[:, psl] * de_ref[:, psl]

    _ssd_gate_norm(y_scr[...], z_ref[...], normw_ref, ya_ref, n_groups)

    @pl.when(c == pl.num_programs(1) - 1)
    def _():
        hout_ref[0] = h_scr[...]


def _ssd_prompt(proj, dtraw, convw, convb, dtb, alog, de, normw, e1, e2, *, batch, seq,
                hw, gn, n_groups, n_state, head_dim, col):
    L = SSD_CHUNK
    nc = seq // L
    n_heads = hw // head_dim
    w_all = hw + 2 * gn
    row = lambda b, c: b * nc + c
    const = lambda b, c: (0, 0)
    kern = functools.partial(_ssd_prompt_kernel, n_groups=n_groups, n_state=n_state, head_dim=head_dim)
    return pl.pallas_call(
        kern,
        grid=(batch, nc),
        in_specs=[
            pl.BlockSpec((L, hw), lambda b, c: (row(b, c), col["z"] // hw)),
            pl.BlockSpec((L, hw), lambda b, c: (row(b, c), col["xs"] // hw)),
            pl.BlockSpec((L, gn), lambda b, c: (row(b, c), col["B"] // gn)),
            pl.BlockSpec((L, gn), lambda b, c: (row(b, c), col["C"] // gn)),
            pl.BlockSpec((L, LANES), lambda b, c: (row(b, c), 0)),
            pl.BlockSpec(convw.shape, const),
            pl.BlockSpec(convb.shape, const),
            pl.BlockSpec(dtb.shape, const),
            pl.BlockSpec(alog.shape, const),
            pl.BlockSpec(de.shape, const),
            pl.BlockSpec(normw.shape, const),
            pl.BlockSpec(e1.shape, const),
            pl.BlockSpec(e2.shape, const),
        ],
        out_specs=[
            pl.BlockSpec((L, hw), lambda b, c: (row(b, c), 0)),
            pl.BlockSpec((1, hw, n_state), lambda b, c: (b, 0, 0)),
        ],
        out_shape=[
            jax.ShapeDtypeStruct((batch * seq, hw), BF16),
            jax.ShapeDtypeStruct((batch, hw, n_state), F32),
        ],
        scratch_shapes=[
            pltpu.VMEM((L + SUBLANES, w_all), F32),
            pltpu.VMEM((L, w_all), F32),
            pltpu.VMEM((hw, n_state), F32),
            pltpu.VMEM((L, hw), BF16),
            pltpu.VMEM((L, hw), BF16),
            pltpu.VMEM((L, hw), F32),
            pltpu.VMEM((L, n_heads * LANES), F32),
            pltpu.VMEM((L, L), F32),
            pltpu.VMEM((L, hw), F32),
        ],
        compiler_params=_cparams("parallel", "arbitrary"),
        name="ssd_prompt",
    )(proj, proj, proj, proj, dtraw, convw, convb, dtb, alog, de, normw, e1, e2)


def _ssd_sample_kernel(z_ref, xs_ref, b_ref, c_ref, dt_ref, cst_ref, h_ref, convw_ref, convb_ref,
                       dtb_ref, alog_ref, de_ref, normw_ref, e1_ref,
                       ya_ref, hout_ref, xbc, y_scr,
                       *, n_groups, n_state):
    bb = xs_ref.shape[0]
    hw = xs_ref.shape[1]
    gn = b_ref.shape[1]
    taps = convw_ref.shape[0]
    gw = hw // n_groups

    xbc[:, 0:hw] = xs_ref[...]
    xbc[:, hw:hw + gn] = b_ref[...]
    xbc[:, hw + gn:] = c_ref[...]
    acc = convb_ref[...] + convw_ref[taps - 1:taps, :] * xbc[...]
    for j in range(taps - 1):
        acc = acc + convw_ref[j:j + 1, :] * cst_ref[j]
    act = _silu(acc)
    xs = act[:, 0:hw]

    dt = _softplus(dt_ref[...] + dtb_ref[...])
    dec = jnp.exp(dt * (-jnp.exp(alog_ref[...])))
    e1 = e1_ref[...]
    dte = _dot3_lhs(dt, e1)
    dece = _dot3_lhs(dec, e1)
    xdt = xs * dte

    pieces = [p.astype(F32) for p in _split3(dece)] + [p.astype(F32) for p in _split3(xdt)]
    npc = len(pieces)
    stack = jnp.concatenate(pieces + [jnp.zeros((LANES - npc * bb, hw), F32)], axis=0)
    lt = stack.T.astype(BF16)

    krow = lax.broadcasted_iota(jnp.int32, (LANES, LANES), 0)
    rowid = lax.broadcasted_iota(jnp.int32, (bb, gw), 0)
    half = npc // 2
    y_scr[...] = jnp.zeros_like(y_scr)
    for s in range(bb):
        is_s = (krow % bb) == s
        sel_dec = jnp.where(is_s & (krow < half * bb), 1.0, 0.0).astype(BF16)
        sel_x = jnp.where(is_s & (krow >= half * bb) & (krow < npc * bb), 1.0, 0.0).astype(BF16)
        dec_b = _dot(lt, sel_dec)
        x_b = _dot(lt, sel_x)
        for g in range(n_groups):
            rows = slice(g * gw, (g + 1) * gw)
            brow = act[s:s + 1, hw + g * n_state:hw + (g + 1) * n_state]
            hn = h_ref[s, rows, :] * dec_b[rows, :] + x_b[rows, :] * brow
            hout_ref[s, rows, :] = hn
            cg = act[:, hw + gn + g * n_state:hw + gn + (g + 1) * n_state].astype(BF16)
            yg = lax.dot_general(cg, hn.astype(BF16), NT_DIMS, preferred_element_type=F32)
            y_scr[:, rows] = y_scr[:, rows] + jnp.where(rowid == s, yg, 0.0)

    y = y_scr[...] + xs * de_ref[...]
    _ssd_gate_norm(y, z_ref[...], normw_ref, ya_ref, n_groups)


def _ssd_sample(proj, dtraw, cst, h0, convw, convb, dtb, alog, de, normw, e1, *, row0, nb,
                hw, gn, n_groups, n_state, col):
    bb = SUBLANES
    w_all = hw + 2 * gn
    r0 = row0 // bb
    const = lambda i: (0, 0)
    kern = functools.partial(_ssd_sample_kernel, n_groups=n_groups, n_state=n_state)
    return pl.pallas_call(
        kern,
        grid=(nb // bb,),
        in_specs=[
            pl.BlockSpec((bb, hw), lambda i: (r0 + i, col["z"] // hw)),
            pl.BlockSpec((bb, hw), lambda i: (r0 + i, col["xs"] // hw)),
            pl.BlockSpec((bb, gn), lambda i: (r0 + i, col["B"] // gn)),
            pl.BlockSpec((bb, gn), lambda i: (r0 + i, col["C"] // gn)),
            pl.BlockSpec((bb, LANES), lambda i: (r0 + i, 0)),
            pl.BlockSpec((cst.shape[0], bb, w_all), lambda i: (0, i, 0)),
            pl.BlockSpec((bb, hw, n_state), lambda i: (i, 0, 0)),
            pl.BlockSpec(convw.shape, const),
            pl.BlockSpec(convb.shape, const),
            pl.BlockSpec(dtb.shape, const),
            pl.BlockSpec(alog.shape, const),
            pl.BlockSpec(de.shape, const),
            pl.BlockSpec(normw.shape, const),
            pl.BlockSpec(e1.shape, const),
        ],
        out_specs=[
            pl.BlockSpec((bb, hw), lambda i: (i, 0)),
            pl.BlockSpec((bb, hw, n_state), lambda i: (i, 0, 0)),
        ],
        out_shape=[
            jax.ShapeDtypeStruct((nb, hw), BF16),
            jax.ShapeDtypeStruct((nb, hw, n_state), F32),
        ],
        scratch_shapes=[pltpu.VMEM((bb, w_all), F32), pltpu.VMEM((bb, hw), F32)],
        compiler_params=_cparams("parallel"),
        name="ssd_sample",
    )(proj, proj, proj, proj, dtraw, cst, h0, convw, convb, dtb, alog, de, normw, e1)


def _sc_prompt_kernel(b_ref, c_ref, h_ref, w_ref, yb_ref, tail_ref, cbuf):
    j = pl.program_id(1)
    ts = b_ref.shape[0]
    taps = w_ref.shape[0]

    @pl.when(j == 0)
    def _():
        cbuf[0:SUBLANES, :] = jnp.zeros((SUBLANES, cbuf.shape[1]), F32)

    cbuf[SUBLANES:SUBLANES + ts, :] = c_ref[...] * h_ref[...]
    u = w_ref[taps - 1:taps, :] * cbuf[SUBLANES:SUBLANES + ts, :]
    for k in range(1, taps):
        u = u + w_ref[taps - 1 - k:taps - k, :] * cbuf[SUBLANES - k:SUBLANES - k + ts, :]
    yb_ref[...] = (b_ref[...] * u).astype(yb_ref.dtype)
    cbuf[0:SUBLANES, :] = cbuf[ts:ts + SUBLANES, :]

    @pl.when(j == pl.num_programs(1) - 1)
    def _():
        tail_ref[0] = cbuf[0:SUBLANES, :]


def _sc_prompt(proj, w, *, batch, seq, width, col, ts):
    nt = seq // ts
    row = lambda b, j: b * nt + j
    return pl.pallas_call(
        _sc_prompt_kernel,
        grid=(batch, nt),
        in_specs=[
            pl.BlockSpec((ts, width), lambda b, j: (row(b, j), col["sc_b"] // width)),
            pl.BlockSpec((ts, width), lambda b, j: (row(b, j), col["sc_c"] // width)),
            pl.BlockSpec((ts, width), lambda b, j: (row(b, j), col["sc_h"] // width)),
            pl.BlockSpec(w.shape, lambda b, j: (0, 0)),
        ],
        out_specs=[
            pl.BlockSpec((ts, width), lambda b, j: (row(b, j), 0)),
            pl.BlockSpec((1, SUBLANES, width), lambda b, j: (b, 0, 0)),
        ],
        out_shape=[
            jax.ShapeDtypeStruct((batch * seq, width), BF16),
            jax.ShapeDtypeStruct((batch, SUBLANES, width), F32),
        ],
        scratch_shapes=[pltpu.VMEM((ts + SUBLANES, width), F32)],
        compiler_params=_cparams("parallel", "arbitrary"),
        name="shortconv_prompt",
    )(proj, proj, proj, w)


def _sc_sample_kernel(b_ref, c_ref, h_ref, st_ref, w_ref, yb_ref, ch_ref):
    taps = w_ref.shape[0]
    ch = c_ref[...] * h_ref[...]
    u = w_ref[taps - 1:taps, :] * ch
    for k in range(taps - 1):
        u = u + w_ref[k:k + 1, :] * st_ref[k]
    yb_ref[...] = (b_ref[...] * u).astype(yb_ref.dtype)
    ch_ref[...] = ch


def _sc_sample(proj, st, w, *, row0, nb, width, col):
    r0 = row0 // nb
    return pl.pallas_call(
        _sc_sample_kernel,
        grid=(1,),
        in_specs=[
            pl.BlockSpec((nb, width), lambda i: (r0, col["sc_b"] // width)),
            pl.BlockSpec((nb, width), lambda i: (r0, col["sc_c"] // width)),
            pl.BlockSpec((nb, width), lambda i: (r0, col["sc_h"] // width)),
            pl.BlockSpec(st.shape, lambda i: (0, 0, 0)),
            pl.BlockSpec(w.shape, lambda i: (0, 0)),
        ],
        out_specs=[pl.BlockSpec((nb, width), lambda i: (0, 0)),
                   pl.BlockSpec((nb, width), lambda i: (0, 0))],
        out_shape=[jax.ShapeDtypeStruct((nb, width), BF16),
                   jax.ShapeDtypeStruct((nb, width), F32)],
        compiler_params=_cparams("arbitrary"),
        name="shortconv_sample",
    )(proj, proj, proj, st, w)


def _branch_kernel(ya_ref, yb_ref, wa_ref, wb_ref, ga_ref, gb_ref, o_ref):
    ta = _dot(ya_ref[...], wa_ref[...])
    tb = _dot(yb_ref[...], wb_ref[...])
    o_ref[...] = (_sigmoid(ga_ref[...]) * ta + _sigmoid(gb_ref[...]) * tb).astype(o_ref.dtype)


def _branch(ya, yb, wa, wb, proj, *, col, tm, tn):
    m, k = ya.shape
    n = wa.shape[1]
    return pl.pallas_call(
        _branch_kernel,
        grid=(m // tm, n // tn),
        in_specs=[
            pl.BlockSpec((tm, k), lambda i, j: (i, 0)),
            pl.BlockSpec((tm, yb.shape[1]), lambda i, j: (i, 0)),
            pl.BlockSpec((k, tn), lambda i, j: (0, j)),
            pl.BlockSpec((yb.shape[1], tn), lambda i, j: (0, j)),
            pl.BlockSpec((tm, tn), lambda i, j: (i, col["g_a"] // tn + j)),
            pl.BlockSpec((tm, tn), lambda i, j: (i, col["g_b"] // tn + j)),
        ],
        out_specs=pl.BlockSpec((tm, tn), lambda i, j: (i, j)),
        out_shape=jax.ShapeDtypeStruct((m, n), BF16),
        compiler_params=_cparams("parallel", "parallel"),
        name="branch_mix",
    )(ya, yb, wa, wb, proj, proj)


def _x1_kernel(x_ref, mix_ref, w_ref, g_ref, b_ref, o_ref, ob_ref, *, alpha):
    t = alpha * x_ref[...] + _dot(mix_ref[...], w_ref[...])
    x1 = _layer_norm(t, g_ref[...], b_ref[...])
    o_ref[...] = x1
    ob_ref[...] = x1.astype(BF16)


def _x1(x, mix, w, g, b, *, alpha, tm):
    m, d = x.shape
    return pl.pallas_call(
        functools.partial(_x1_kernel, alpha=alpha),
        grid=(m // tm,),
        in_specs=[
            pl.BlockSpec((tm, d), lambda i: (i, 0)),
            pl.BlockSpec((tm, d), lambda i: (i, 0)),
            pl.BlockSpec(w.shape, lambda i: (0, 0)),
            pl.BlockSpec(g.shape, lambda i: (0, 0)),
            pl.BlockSpec(b.shape, lambda i: (0, 0)),
        ],
        out_specs=[pl.BlockSpec((tm, d), lambda i: (i, 0)), pl.BlockSpec((tm, d), lambda i: (i, 0))],
        out_shape=[jax.ShapeDtypeStruct((m, d), F32), jax.ShapeDtypeStruct((m, d), BF16)],
        compiler_params=_cparams("parallel"),
        name="x1_out_ln",
    )(x, mix, w, g, b)


def _top_rows(s, k):
    rows = []
    cur = s
    for _ in range(k):
        m = jnp.max(cur, axis=0, keepdims=True)
        rows.append(m)
        cur = jnp.where(cur == m, -jnp.inf, cur)
    return rows


def _route_kernel(x_ref, wq_ref, k1_ref, k2_ref, thr_ref, s1_ref, w1_ref, e2_ref, s2_ref, q_scr):
    n_heads = k1_ref.shape[0]
    dk = k1_ref.shape[2]
    q_scr[...] = _dot(x_ref[...], wq_ref[...]).astype(BF16)
    for h in range(n_heads):
        q1 = q_scr[:, (2 * h) * dk:(2 * h + 1) * dk]
        q2 = q_scr[:, (2 * h + 1) * dk:(2 * h + 2) * dk]
        s1 = lax.dot_general(k1_ref[h], q1, NT_DIMS, preferred_element_type=F32)
        s2 = lax.dot_general(k2_ref[h], q2, NT_DIMS, preferred_element_type=F32)
        t1 = _top_rows(s1, PEER_TOPK)
        t2 = jnp.concatenate(_top_rows(s2, PEER_TOPK), axis=0)
        cand = jnp.concatenate([t1[a] + t2 for a in range(PEER_TOPK)], axis=0)
        thr = _top_rows(cand, PEER_TOPK)[-1]
        m1 = t1[0]
        m2 = t2[0:1, :]
        zsum = jnp.sum(jnp.where(cand >= thr, jnp.exp(cand - (m1 + m2)), 0.0), axis=0, keepdims=True)
        thr_ref[h] = thr
        s1_ref[h] = s1
        w1_ref[h] = jnp.exp(s1 - m1) / zsum
        e2_ref[h] = jnp.exp(s2 - m2)
        s2_ref[h] = s2


def _route(x1b, wq, k1, k2, *, tm):
    m, d = x1b.shape
    n_heads, n_keys, _ = k1.shape
    oshape = jax.ShapeDtypeStruct((n_heads, n_keys, m), F32)
    ospec = pl.BlockSpec((n_heads, n_keys, tm), lambda i: (0, 0, i))
    return pl.pallas_call(
        _route_kernel,
        grid=(m // tm,),
        in_specs=[
            pl.BlockSpec((tm, d), lambda i: (i, 0)),
            pl.BlockSpec(wq.shape, lambda i: (0, 0)),
            pl.BlockSpec(k1.shape, lambda i: (0, 0, 0)),
            pl.BlockSpec(k2.shape, lambda i: (0, 0, 0)),
        ],
        out_specs=[pl.BlockSpec((n_heads, 1, tm), lambda i: (0, 0, i)), ospec, ospec, ospec, ospec],
        out_shape=[jax.ShapeDtypeStruct((n_heads, 1, m), F32), oshape, oshape, oshape, oshape],
        scratch_shapes=[pltpu.VMEM((tm, wq.shape[1]), BF16)],
        compiler_params=_cparams("parallel"),
        name="peer_route",
    )(x1b, wq, k1, k2)


def _expert_kernel(x_ref, u_ref, v_ref, thr_ref, s1_ref, w1_ref, e2_ref, s2_ref, o_ref, coef):
    e = pl.program_id(1)
    te = u_ref.shape[0]
    n_heads, n_keys, tm = s2_ref.shape
    per = te // n_keys

    @pl.when(e == 0)
    def _():
        o_ref[...] = jnp.zeros_like(o_ref)

    hid = lax.dot_general(u_ref[...], x_ref[...], NT_DIMS, preferred_element_type=F32)
    for k in range(per):
        i1 = e * per + k
        gate = jnp.zeros((n_keys, tm), F32)
        for h in range(n_heads):
            score = s1_ref[h, pl.ds(i1, 1), :] + s2_ref[h]
            w1 = w1_ref[h, pl.ds(i1, 1), :]
            gate = gate + jnp.where(score >= thr_ref[h], e2_ref[h] * w1, 0.0)
        hk = hid[k * n_keys:(k + 1) * n_keys, :]
        gelu = 0.5 * hk * (1.0 + lax.erf(hk * (1.0 / math.sqrt(2.0))))
        coef[k * n_keys:(k + 1) * n_keys, :] = (gate * gelu).astype(BF16)
    o_ref[...] += lax.dot_general(coef[...], v_ref[...], TN_DIMS, preferred_element_type=F32)


def _experts(x1b, u, v, thr, s1, w1, e2, s2, *, tm, te):
    m, d = x1b.shape
    n_exp = u.shape[0]
    n_heads, n_keys, _ = s2.shape
    rspec = pl.BlockSpec((n_heads, n_keys, tm), lambda i, e: (0, 0, i))
    return pl.pallas_call(
        _expert_kernel,
        grid=(m // tm, n_exp // te),
        in_specs=[
            pl.BlockSpec((tm, d), lambda i, e: (i, 0)),
            pl.BlockSpec((te, d), lambda i, e: (e, 0)),
            pl.BlockSpec((te, d), lambda i, e: (e, 0)),
            pl.BlockSpec((n_heads, 1, tm), lambda i, e: (0, 0, i)), rspec, rspec, rspec, rspec,
        ],
        out_specs=pl.BlockSpec((tm, d), lambda i, e: (i, 0)),
        out_shape=jax.ShapeDtypeStruct((m, d), F32),
        scratch_shapes=[pltpu.VMEM((te, tm), BF16)],
        compiler_params=_cparams("parallel", "arbitrary"),
        name="peer_experts",
    )(x1b, u, v, thr, s1, w1, e2, s2)


def _final_kernel(x1_ref, peer_ref, p_ref, wg_ref, wp_ref, g_ref, b_ref, o_ref, *, alpha):
    x2 = _layer_norm(alpha * x1_ref[...] + peer_ref[...], g_ref[...], b_ref[...])
    gate = _sigmoid(_dot(x2.astype(BF16), wg_ref[...]))
    o_ref[...] = x2 + gate * _dot(p_ref[...].astype(BF16), wp_ref[...])


def _final(x1, peer, p, wg, wp, g, b, *, alpha, tm):
    m, d = x1.shape
    return pl.pallas_call(
        functools.partial(_final_kernel, alpha=alpha),
        grid=(m // tm,),
        in_specs=[
            pl.BlockSpec((tm, d), lambda i: (i, 0)),
            pl.BlockSpec((tm, d), lambda i: (i, 0)),
            pl.BlockSpec((tm, p.shape[1]), lambda i: (i, 0)),
            pl.BlockSpec(wg.shape, lambda i: (0, 0)),
            pl.BlockSpec(wp.shape, lambda i: (0, 0)),
            pl.BlockSpec(g.shape, lambda i: (0, 0)),
            pl.BlockSpec(b.shape, lambda i: (0, 0)),
        ],
        out_specs=pl.BlockSpec((tm, d), lambda i: (i, 0)),
        out_shape=jax.ShapeDtypeStruct((m, d), F32),
        compiler_params=_cparams("parallel"),
        name="ln2_ple",
    )(x1, peer, p, wg, wp, g, b)


def _tile(m, cap, mult):
    best = None
    for t in range(mult, min(m, cap) + 1, mult):
        if m % t == 0:
            best = t
    assert best is not None, (m, cap, mult)
    return best


TOKEN_TILE = 640
ROW_TILE = 320
EXPERT_TILE = 512


def _pad_lanes(v):
    return jnp.pad(v.astype(F32), (0, LANES - v.shape[0])).reshape(1, LANES)


def _layer(x, p, ssm_h, conv_buf, sc_buf, n_prompt, batch, seq, depth,
           w_in, ssd_conv_w, ssd_conv_b, ssd_dt_bias, ssd_a_log, ssd_d, ssd_norm_w,
           sc_conv_w, w_branch_ssd, w_branch_sc, w_out, ln1_g, ln1_b,
           peer_wq, peer_keys1, peer_keys2, peer_u, peer_v, ln2_g, ln2_b,
           ple_gate_w, ple_proj_w):
    m, d = x.shape
    nb = m - n_prompt
    n_heads = ssd_dt_bias.shape[0]
    hw, n_state = ssm_h.shape[1] * ssm_h.shape[2], ssm_h.shape[3]
    head_dim = ssm_h.shape[2]
    conv_dim = ssd_conv_w.shape[1]
    gn = (conv_dim - hw) // 2
    n_groups = gn // n_state
    scw = sc_conv_w.shape[1]
    alpha = (2.0 * depth) ** 0.25
    assert hw == scw == d and n_heads <= LANES and LANES % head_dim == 0 and n_state == LANES

    o_z, o_xbc, o_dt = 0, hw, hw + conv_dim
    o_scb = o_dt + n_heads
    o_scc, o_sch, o_ga, o_gb = o_scb + scw, o_scb + 2 * scw, o_scb + 3 * scw, o_scb + 3 * scw + d
    seg = lambda o, w: w_in[:, o:o + w]
    w_main = jnp.concatenate(
        [seg(o_z, hw), seg(o_xbc, hw), seg(o_scb, scw), seg(o_scc, scw), seg(o_sch, scw),
         seg(o_ga, d), seg(o_gb, d), seg(o_xbc + hw, gn), seg(o_xbc + hw + gn, gn)], axis=1).astype(BF16)
    col = {"z": 0, "xs": hw, "sc_b": 2 * hw, "sc_c": 3 * hw, "sc_h": 4 * hw, "g_a": 5 * hw,
           "g_b": 6 * hw, "B": 7 * hw, "C": 7 * hw + gn}
    w_dt = jnp.pad(seg(o_dt, n_heads), ((0, 0), (0, LANES - n_heads))).astype(BF16)

    tm = _tile(m, TOKEN_TILE, LANES)
    proj = _matmul(x, w_main, tm, _tile(w_main.shape[1], 1024, LANES), F32)
    dtraw = _matmul(x, w_dt, tm, LANES, F32)

    ch_head = jnp.arange(hw) // head_dim
    e1 = (jnp.arange(LANES)[:, None] == ch_head[None, :]).astype(BF16)
    e2 = (jnp.arange(LANES)[:, None] == (jnp.arange(n_heads * LANES) // LANES)[None, :]).astype(BF16)
    convb = ssd_conv_b.reshape(1, conv_dim)
    dtb, alog = _pad_lanes(ssd_dt_bias), _pad_lanes(ssd_a_log)
    de = jnp.repeat(ssd_d.astype(F32), head_dim).reshape(1, hw)
    normw = ssd_norm_w.reshape(1, hw)
    shp = dict(hw=hw, gn=gn, n_groups=n_groups, n_state=n_state, col=col)
    ya_p, h_p = _ssd_prompt(proj, dtraw, ssd_conv_w, convb, dtb, alog, de, normw, e1, e2,
                            batch=batch, seq=seq, head_dim=head_dim, **shp)
    cst = jnp.transpose(conv_buf, (1, 0, 2))
    ya_s, h_s = _ssd_sample(proj, dtraw, cst, ssm_h.reshape(nb, hw, n_state), ssd_conv_w, convb,
                            dtb, alog, de, normw, e1, row0=n_prompt, nb=nb, **shp)

    yb_p, sc_tail = _sc_prompt(proj, sc_conv_w, batch=batch, seq=seq, width=scw, col=col,
                               ts=min(seq, 256))
    yb_s, ch_s = _sc_sample(proj, jnp.transpose(sc_buf, (1, 0, 2)), sc_conv_w,
                            row0=n_prompt, nb=nb, width=scw, col=col)

    ya = jnp.concatenate([ya_p, ya_s], axis=0)
    yb = jnp.concatenate([yb_p, yb_s], axis=0)
    mix = _branch(ya, yb, w_branch_ssd.astype(BF16), w_branch_sc.astype(BF16), proj,
                  col=col, tm=tm, tn=_tile(d, 512, LANES))
    tm2 = _tile(m, ROW_TILE, SUBLANES)
    x1, x1b = _x1(x, mix, w_out.astype(BF16), ln1_g.reshape(1, d), ln1_b.reshape(1, d),
                  alpha=alpha, tm=tm2)

    thr, s1, w1, e2r, s2 = _route(x1b, peer_wq.astype(BF16), peer_keys1.astype(BF16),
                              peer_keys2.astype(BF16), tm=tm)
    peer = _experts(x1b, peer_u.astype(BF16), peer_v.astype(BF16), thr, s1, w1, e2r, s2,
                    tm=tm, te=EXPERT_TILE)

    y = _final(x1, peer, p, ple_gate_w.astype(BF16), ple_proj_w.astype(BF16),
               ln2_g.reshape(1, d), ln2_b.reshape(1, d), alpha=alpha, tm=tm2)

    k_ssd = ssd_conv_w.shape[0] - 1
    k_sc = sc_conv_w.shape[0] - 1
    xbc_cols = lambda rows: jnp.concatenate(
        [rows[..., col["xs"]:col["xs"] + hw], rows[..., col["B"]:col["B"] + gn],
         rows[..., col["C"]:col["C"] + gn]], axis=-1)
    proj_p = proj[:n_prompt].reshape(batch, seq, -1)
    conv_p = xbc_cols(proj_p[:, seq - k_ssd:, :])
    conv_s = jnp.concatenate([conv_buf[:, 1:, :], xbc_cols(proj[n_prompt:])[:, None, :]], axis=1)
    sc_p = sc_tail[:, SUBLANES - k_sc:, :]
    sc_s = jnp.concatenate([sc_buf[:, 1:, :], ch_s[:, None, :]], axis=1)
    hshape = (-1, n_heads, head_dim, n_state)
    return y, conv_p, h_p.reshape(hshape), sc_p, conv_s, h_s.reshape(hshape), sc_s


def kernel(x_prompt, x_sample, p_prompt, p_sample, state_ssm, state_ssd_conv, state_shortconv, w_in, ssd_conv_w, ssd_conv_b, ssd_dt_bias, ssd_a_log, ssd_d, ssd_norm_w, sc_conv_w, w_branch_ssd, w_branch_sc, w_out, ln1_g, ln1_b, peer_wq, peer_keys1, peer_keys2, peer_u, peer_v, ln2_g, ln2_b, ple_gate_w, ple_proj_w):
    batch, seq, d = x_prompt.shape
    nb, dec_seq, _ = x_sample.shape
    assert dec_seq == 1 and seq % SSD_CHUNK == 0
    depth = w_in.shape[0]
    n_prompt = batch * seq
    x = jnp.concatenate([x_prompt.reshape(n_prompt, d), x_sample.reshape(nb, d)], axis=0)
    weights = (w_in, ssd_conv_w, ssd_conv_b, ssd_dt_bias, ssd_a_log, ssd_d, ssd_norm_w,
               sc_conv_w, w_branch_ssd, w_branch_sc, w_out, ln1_g, ln1_b,
               peer_wq, peer_keys1, peer_keys2, peer_u, peer_v, ln2_g, ln2_b,
               ple_gate_w, ple_proj_w)
    outs = [[] for _ in range(6)]
    for i in range(depth):
        p = jnp.concatenate([p_prompt[i].reshape(n_prompt, -1), p_sample[i].reshape(nb, -1)], axis=0)
        x, conv_p, h_p, sc_p, conv_s, h_s, sc_s = _layer(
            x, p, state_ssm[i], state_ssd_conv[i], state_shortconv[i], n_prompt, batch, seq, depth,
            *[w[i] for w in weights])
        for lst, val in zip(outs, (h_p, conv_p, sc_p, h_s, conv_s, sc_s)):
            lst.append(val)
    y_prompt = x[:n_prompt].reshape(batch, seq, d)
    y_sample = x[n_prompt:].reshape(nb, 1, d)
    return (y_prompt, y_sample) + tuple(jnp.stack(lst) for lst in outs)
```

```python
import functools
import math

import jax
import jax.numpy as jnp
from jax import lax
from jax.experimental import pallas as pl
from jax.experimental.pallas import tpu as pltpu

F32 = jnp.float32
BF16 = jnp.bfloat16

LANES = 128
SUBLANES = 8
BF16_ROWS = 16
PEER_TOPK = 16
SSD_CHUNK = 128
LN_EPS = 1e-5
RMS_EPS = 1e-5
VMEM_LIMIT = 56 * 1024 * 1024

NT_DIMS = (((1,), (1,)), ((), ()))
TN_DIMS = (((0,), (0,)), ((), ()))


def _cparams(*sem):
    return pltpu.CompilerParams(dimension_semantics=sem, vmem_limit_bytes=VMEM_LIMIT)


def _dot(a, b):
    return jnp.dot(a, b, preferred_element_type=F32)


def _split3(v):
    hi = v.astype(BF16)
    r = v - hi.astype(F32)
    mid = r.astype(BF16)
    lo = (r - mid.astype(F32)).astype(BF16)
    return hi, mid, lo


def _dot3_lhs(v, rhs_bf16):
    hi, mid, lo = _split3(v)
    return _dot(hi, rhs_bf16) + _dot(mid, rhs_bf16) + _dot(lo, rhs_bf16)


def _dot3_rhs(lhs_bf16, v):
    hi, mid, lo = _split3(v)
    return _dot(lhs_bf16, hi) + _dot(lhs_bf16, mid) + _dot(lhs_bf16, lo)


def _sigmoid(x):
    return 1.0 / (1.0 + jnp.exp(-x))


def _silu(x):
    return x * _sigmoid(x)


def _softplus(x):
    return jnp.maximum(x, 0.0) + jnp.log1p(jnp.exp(-jnp.abs(x)))


def _layer_norm(x, g, b):
    mu = jnp.mean(x, axis=-1, keepdims=True)
    xc = x - mu
    var = jnp.mean(xc * xc, axis=-1, keepdims=True)
    return xc * lax.rsqrt(var + LN_EPS) * g + b


def _mm_kernel(x_ref, w_ref, o_ref):
    o_ref[...] = _dot(x_ref[...].astype(BF16), w_ref[...]).astype(o_ref.dtype)


def _matmul(x, w, tm, tn, out_dtype):
    m, k = x.shape
    n = w.shape[1]
    return pl.pallas_call(
        _mm_kernel,
        grid=(m // tm, n // tn),
        in_specs=[pl.BlockSpec((tm, k), lambda i, j: (i, 0)),
                  pl.BlockSpec((k, tn), lambda i, j: (0, j))],
        out_specs=pl.BlockSpec((tm, tn), lambda i, j: (i, j)),
        out_shape=jax.ShapeDtypeStruct((m, n), out_dtype),
        compiler_params=_cparams("parallel", "parallel"),
        name="in_proj",
    )(x, w)


def _ssd_gate_norm(y, z, normw_ref, out_ref, n_groups):
    y = y * _silu(z)
    gw = y.shape[1] // n_groups
    for g in range(n_groups):
        sl = slice(g * gw, (g + 1) * gw)
        yg = y[:, sl]
        ms = jnp.mean(yg * yg, axis=-1, keepdims=True)
        out_ref[:, sl] = (yg * lax.rsqrt(ms + RMS_EPS) * normw_ref[:, sl]).astype(out_ref.dtype)


def _ssd_prompt_kernel(z_ref, xs_ref, b_ref, c_ref, dt_ref, convw_ref, convb_ref, dtb_ref,
                       alog_ref, de_ref, normw_ref, e1_ref, e2_ref,
                       ya_ref, hout_ref,
                       cbuf, act, h_scr, x_scr, xd_scr, eae_scr, acsb_scr, acst_scr, y_scr,
                       *, n_groups, n_state, head_dim):
    c = pl.program_id(1)
    L = SSD_CHUNK
    hw = xs_ref.shape[1]
    gn = b_ref.shape[1]
    w_all = hw + 2 * gn
    n_heads = hw // head_dim
    hpl = LANES // head_dim
    n_blk = n_heads // hpl
    blk_per_group = n_blk // n_groups
    taps = convw_ref.shape[0]

    @pl.when(c == 0)
    def _():
        h_scr[...] = jnp.zeros_like(h_scr)
        cbuf[0:SUBLANES, :] = jnp.zeros((SUBLANES, w_all), F32)

    cbuf[SUBLANES:SUBLANES + L, 0:hw] = xs_ref[...]
    cbuf[SUBLANES:SUBLANES + L, hw:hw + gn] = b_ref[...]
    cbuf[SUBLANES:SUBLANES + L, hw + gn:] = c_ref[...]

    cw = math.gcd(w_all, 512)
    for blk in range(w_all // cw):
        sl = slice(blk * cw, (blk + 1) * cw)
        acc = convb_ref[:, sl] + convw_ref[taps - 1:taps, sl] * cbuf[SUBLANES:SUBLANES + L, sl]
        for j in range(1, taps):
            acc = acc + convw_ref[taps - 1 - j:taps - j, sl] * cbuf[SUBLANES - j:SUBLANES - j + L, sl]
        act[:, sl] = _silu(acc)
    cbuf[0:SUBLANES, :] = cbuf[L:L + SUBLANES, :]

    dt = _softplus(dt_ref[...] + dtb_ref[...])
    a_neg = -jnp.exp(alog_ref[...])
    dta = dt * a_neg
    ri = lax.broadcasted_iota(jnp.int32, (L, L), 0)
    ci = lax.broadcasted_iota(jnp.int32, (L, L), 1)
    causal = ri >= ci
    tri = jnp.where(causal, 1.0, 0.0).astype(BF16)
    acs = _dot3_rhs(tri, dta)
    e1 = e1_ref[...]
    dte = _dot3_lhs(dt, e1)
    acs_p = _split3(acs)
    acse = _dot(acs_p[0], e1) + _dot(acs_p[1], e1) + _dot(acs_p[2], e1)
    e2 = e2_ref[...]
    acsb_scr[...] = _dot(acs_p[0], e2) + _dot(acs_p[1], e2) + _dot(acs_p[2], e2)
    acst_scr[...] = acs.T

    xdt = act[:, 0:hw] * dte
    x_scr[...] = xdt.astype(BF16)
    xd_scr[...] = (xdt * jnp.exp(acse[L - 1:L, :] - acse)).astype(BF16)
    eae_scr[...] = jnp.exp(acse)

    lane = lax.broadcasted_iota(jnp.int32, (L, LANES), 1)
    cb = None
    for j in range(n_blk):
        g = j // blk_per_group
        bsl = slice(hw + g * n_state, hw + (g + 1) * n_state)
        csl = slice(hw + gn + g * n_state, hw + gn + (g + 1) * n_state)
        bg = act[:, bsl].astype(BF16)
        cg = act[:, csl].astype(BF16)
        if j % blk_per_group == 0:
            cb = lax.dot_general(cg, bg, NT_DIMS, preferred_element_type=F32)
        psl = slice(j * LANES, (j + 1) * LANES)
        xp = x_scr[:, psl]
        ydiag = None
        cds = []
        for q in range(hpl):
            r = j * hpl + q
            ab = acsb_scr[:, r * LANES:(r + 1) * LANES]
            at = jnp.broadcast_to(acst_scr[r:r + 1, :], (L, L))
            lm = jnp.where(causal, jnp.exp(ab - at), 0.0)
            m = (cb * lm).astype(BF16)
            inhead = (lane >= q * head_dim) & (lane < (q + 1) * head_dim)
            xq = jnp.where(inhead, xp, jnp.zeros_like(xp))
            yq = _dot(m, xq)
            ydiag = yq if ydiag is None else ydiag + yq
            cds.append(jnp.broadcast_to(jnp.exp(acsb_scr[L - 1:L, r * LANES:(r + 1) * LANES]),
                                        (head_dim, LANES)))
        cd = jnp.concatenate(cds, axis=0)
        hp = h_scr[psl, :]
        yoff = lax.dot_general(cg, hp.astype(BF16), NT_DIMS, preferred_element_type=F32)
        yoff = yoff * eae_scr[:, psl]
        st = lax.dot_general(xd_scr[:, psl], bg, TN_DIMS, preferred_element_type=F32)
        h_scr[psl, :] = hp * cd + st
        y_scr[:, psl] = ydiag + yoff + act[:, psl] * de_ref[:, psl]

    _ssd_gate_norm(y_scr[...], z_ref[...], normw_ref, ya_ref, n_groups)

    @pl.when(c == pl.num_programs(1) - 1)
    def _():
        hout_ref[0] = h_scr[...]


def _ssd_prompt(proj, dtraw, convw, convb, dtb, alog, de, normw, e1, e2, *, batch, seq,
                hw, gn, n_groups, n_state, head_dim, col):
    L = SSD_CHUNK
    nc = seq // L
    n_heads = hw // head_dim
    w_all = hw + 2 * gn
    row = lambda b, c: b * nc + c
    const = lambda b, c: (0, 0)
    kern = functools.partial(_ssd_prompt_kernel, n_groups=n_groups, n_state=n_state, head_dim=head_dim)
    return pl.pallas_call(
        kern,
        grid=(batch, nc),
        in_specs=[
            pl.BlockSpec((L, hw), lambda b, c: (row(b, c), col["z"] // hw)),
            pl.BlockSpec((L, hw), lambda b, c: (row(b, c), col["xs"] // hw)),
            pl.BlockSpec((L, gn), lambda b, c: (row(b, c), col["B"] // gn)),
            pl.BlockSpec((L, gn), lambda b, c: (row(b, c), col["C"] // gn)),
            pl.BlockSpec((L, LANES), lambda b, c: (row(b, c), 0)),
            pl.BlockSpec(convw.shape, const),
            pl.BlockSpec(convb.shape, const),
            pl.BlockSpec(dtb.shape, const),
            pl.BlockSpec(alog.shape, const),
            pl.BlockSpec(de.shape, const),
            pl.BlockSpec(normw.shape, const),
            pl.BlockSpec(e1.shape, const),
            pl.BlockSpec(e2.shape, const),
        ],
        out_specs=[
            pl.BlockSpec((L, hw), lambda b, c: (row(b, c), 0)),
            pl.BlockSpec((1, hw, n_state), lambda b, c: (b, 0, 0)),
        ],
        out_shape=[
            jax.ShapeDtypeStruct((batch * seq, hw), BF16),
            jax.ShapeDtypeStruct((batch, hw, n_state), F32),
        ],
        scratch_shapes=[
            pltpu.VMEM((L + SUBLANES, w_all), F32),
            pltpu.VMEM((L, w_all), F32),
            pltpu.VMEM((hw, n_state), F32),
            pltpu.VMEM((L, hw), BF16),
            pltpu.VMEM((L, hw), BF16),
            pltpu.VMEM((L, hw), F32),
            pltpu.VMEM((L, n_heads * LANES), F32),
            pltpu.VMEM((L, L), F32),
            pltpu.VMEM((L, hw), F32),
        ],
        compiler_params=_cparams("parallel", "arbitrary"),
        name="ssd_prompt",
    )(proj, proj, proj, proj, dtraw, convw, convb, dtb, alog, de, normw, e1, e2)


def _ssd_sample_kernel(z_ref, xs_ref, b_ref, c_ref, dt_ref, cst_ref, h_ref, convw_ref, convb_ref,
                       dtb_ref, alog_ref, de_ref, normw_ref, e1_ref,
                       ya_ref, hout_ref, xbc, y_scr,
                       *, n_groups, n_state):
    bb = xs_ref.shape[0]
    hw = xs_ref.shape[1]
    gn = b_ref.shape[1]
    taps = convw_ref.shape[0]
    gw = hw // n_groups

    xbc[:, 0:hw] = xs_ref[...]
    xbc[:, hw:hw + gn] = b_ref[...]
    xbc[:, hw + gn:] = c_ref[...]
    acc = convb_ref[...] + convw_ref[taps - 1:taps, :] * xbc[...]
    for j in range(taps - 1):
        acc = acc + convw_ref[j:j + 1, :] * cst_ref[j]
    act = _silu(acc)
    xs = act[:, 0:hw]

    dt = _softplus(dt_ref[...] + dtb_ref[...])
    dec = jnp.exp(dt * (-jnp.exp(alog_ref[...])))
    e1 = e1_ref[...]
    dte = _dot3_lhs(dt, e1)
    dece = _dot3_lhs(dec, e1)
    xdt = xs * dte

    pieces = [p.astype(F32) for p in _split3(dece)] + [p.astype(F32) for p in _split3(xdt)]
    npc = len(pieces)
    stack = jnp.concatenate(pieces + [jnp.zeros((LANES - npc * bb, hw), F32)], axis=0)
    lt = stack.T.astype(BF16)

    krow = lax.broadcasted_iota(jnp.int32, (LANES, LANES), 0)
    rowid = lax.broadcasted_iota(jnp.int32, (bb, gw), 0)
    half = npc // 2
    y_scr[...] = jnp.zeros_like(y_scr)
    for s in range(bb):
        is_s = (krow % bb) == s
        sel_dec = jnp.where(is_s & (krow < half * bb), 1.0, 0.0).astype(BF16)
        sel_x = jnp.where(is_s & (krow >= half * bb) & (krow < npc * bb), 1.0, 0.0).astype(BF16)
        dec_b = _dot(lt, sel_dec)
        x_b = _dot(lt, sel_x)
        for g in range(n_groups):
            rows = slice(g * gw, (g + 1) * gw)
            brow = act[s:s + 1, hw + g * n_state:hw + (g + 1) * n_state]
            hn = h_ref[s, rows, :] * dec_b[rows, :] + x_b[rows, :] * brow
            hout_ref[s, rows, :] = hn
            cg = act[:, hw + gn + g * n_state:hw + gn + (g + 1) * n_state].astype(BF16)
            yg = lax.dot_general(cg, hn.astype(BF16), NT_DIMS, preferred_element_type=F32)
            y_scr[:, rows] = y_scr[:, rows] + jnp.where(rowid == s, yg, 0.0)

    y = y_scr[...] + xs * de_ref[...]
    _ssd_gate_norm(y, z_ref[...], normw_ref, ya_ref, n_groups)


def _ssd_sample(proj, dtraw, cst, h0, convw, convb, dtb, alog, de, normw, e1, *, row0, nb,
                hw, gn, n_groups, n_state, col):
    bb = SUBLANES
    w_all = hw + 2 * gn
    r0 = row0 // bb
    const = lambda i: (0, 0)
    kern = functools.partial(_ssd_sample_kernel, n_groups=n_groups, n_state=n_state)
    return pl.pallas_call(
        kern,
        grid=(nb // bb,),
        in_specs=[
            pl.BlockSpec((bb, hw), lambda i: (r0 + i, col["z"] // hw)),
            pl.BlockSpec((bb, hw), lambda i: (r0 + i, col["xs"] // hw)),
            pl.BlockSpec((bb, gn), lambda i: (r0 + i, col["B"] // gn)),
            pl.BlockSpec((bb, gn), lambda i: (r0 + i, col["C"] // gn)),
            pl.BlockSpec((bb, LANES), lambda i: (r0 + i, 0)),
            pl.BlockSpec((cst.shape[0], bb, w_all), lambda i: (0, i, 0)),
            pl.BlockSpec((bb, hw, n_state), lambda i: (i, 0, 0)),
            pl.BlockSpec(convw.shape, const),
            pl.BlockSpec(convb.shape, const),
            pl.BlockSpec(dtb.shape, const),
            pl.BlockSpec(alog.shape, const),
            pl.BlockSpec(de.shape, const),
            pl.BlockSpec(normw.shape, const),
            pl.BlockSpec(e1.shape, const),
        ],
        out_specs=[
            pl.BlockSpec((bb, hw), lambda i: (i, 0)),
            pl.BlockSpec((bb, hw, n_state), lambda i: (i, 0, 0)),
        ],
        out_shape=[
            jax.ShapeDtypeStruct((nb, hw), BF16),
            jax.ShapeDtypeStruct((nb, hw, n_state), F32),
        ],
        scratch_shapes=[pltpu.VMEM((bb, w_all), F32), pltpu.VMEM((bb, hw), F32)],
        compiler_params=_cparams("parallel"),
        name="ssd_sample",
    )(proj, proj, proj, proj, dtraw, cst, h0, convw, convb, dtb, alog, de, normw, e1)


def _sc_prompt_kernel(b_ref, c_ref, h_ref, w_ref, yb_ref, tail_ref, cbuf):
    j = pl.program_id(1)
    ts = b_ref.shape[0]
    taps = w_ref.shape[0]

    @pl.when(j == 0)
    def _():
        cbuf[0:SUBLANES, :] = jnp.zeros((SUBLANES, cbuf.shape[1]), F32)

    cbuf[SUBLANES:SUBLANES + ts, :] = c_ref[...] * h_ref[...]
    u = w_ref[taps - 1:taps, :] * cbuf[SUBLANES:SUBLANES + ts, :]
    for k in range(1, taps):
        u = u + w_ref[taps - 1 - k:taps - k, :] * cbuf[SUBLANES - k:SUBLANES - k + ts, :]
    yb_ref[...] = (b_ref[...] * u).astype(yb_ref.dtype)
    cbuf[0:SUBLANES, :] = cbuf[ts:ts + SUBLANES, :]

    @pl.when(j == pl.num_programs(1) - 1)
    def _():
        tail_ref[0] = cbuf[0:SUBLANES, :]


def _sc_prompt(proj, w, *, batch, seq, width, col, ts):
    nt = seq // ts
    row = lambda b, j: b * nt + j
    return pl.pallas_call(
        _sc_prompt_kernel,
        grid=(batch, nt),
        in_specs=[
            pl.BlockSpec((ts, width), lambda b, j: (row(b, j), col["sc_b"] // width)),
            pl.BlockSpec((ts, width), lambda b, j: (row(b, j), col["sc_c"] // width)),
            pl.BlockSpec((ts, width), lambda b, j: (row(b, j), col["sc_h"] // width)),
            pl.BlockSpec(w.shape, lambda b, j: (0, 0)),
        ],
        out_specs=[
            pl.BlockSpec((ts, width), lambda b, j: (row(b, j), 0)),
            pl.BlockSpec((1, SUBLANES, width), lambda b, j: (b, 0, 0)),
        ],
        out_shape=[
            jax.ShapeDtypeStruct((batch * seq, width), BF16),
            jax.ShapeDtypeStruct((batch, SUBLANES, width), F32),
        ],
        scratch_shapes=[pltpu.VMEM((ts + SUBLANES, width), F32)],
        compiler_params=_cparams("parallel", "arbitrary"),
        name="shortconv_prompt",
    )(proj, proj, proj, w)


def _sc_sample_kernel(b_ref, c_ref, h_ref, st_ref, w_ref, yb_ref, ch_ref):
    taps = w_ref.shape[0]
    ch = c_ref[...] * h_ref[...]
    u = w_ref[taps - 1:taps, :] * ch
    for k in range(taps - 1):
        u = u + w_ref[k:k + 1, :] * st_ref[k]
    yb_ref[...] = (b_ref[...] * u).astype(yb_ref.dtype)
    ch_ref[...] = ch


def _sc_sample(proj, st, w, *, row0, nb, width, col):
    r0 = row0 // nb
    return pl.pallas_call(
        _sc_sample_kernel,
        grid=(1,),
        in_specs=[
            pl.BlockSpec((nb, width), lambda i: (r0, col["sc_b"] // width)),
            pl.BlockSpec((nb, width), lambda i: (r0, col["sc_c"] // width)),
            pl.BlockSpec((nb, width), lambda i: (r0, col["sc_h"] // width)),
            pl.BlockSpec(st.shape, lambda i: (0, 0, 0)),
            pl.BlockSpec(w.shape, lambda i: (0, 0)),
        ],
        out_specs=[pl.BlockSpec((nb, width), lambda i: (0, 0)),
                   pl.BlockSpec((nb, width), lambda i: (0, 0))],
        out_shape=[jax.ShapeDtypeStruct((nb, width), BF16),
                   jax.ShapeDtypeStruct((nb, width), F32)],
        compiler_params=_cparams("arbitrary"),
        name="shortconv_sample",
    )(proj, proj, proj, st, w)


def _branch_kernel(ya_ref, yb_ref, wa_ref, wb_ref, ga_ref, gb_ref, o_ref):
    ta = _dot(ya_ref[...], wa_ref[...])
    tb = _dot(yb_ref[...], wb_ref[...])
    o_ref[...] = (_sigmoid(ga_ref[...]) * ta + _sigmoid(gb_ref[...]) * tb).astype(o_ref.dtype)


def _branch(ya, yb, wa, wb, proj, *, col, tm, tn):
    m, k = ya.shape
    n = wa.shape[1]
    return pl.pallas_call(
        _branch_kernel,
        grid=(m // tm, n // tn),
        in_specs=[
            pl.BlockSpec((tm, k), lambda i, j: (i, 0)),
            pl.BlockSpec((tm, yb.shape[1]), lambda i, j: (i, 0)),
            pl.BlockSpec((k, tn), lambda i, j: (0, j)),
            pl.BlockSpec((yb.shape[1], tn), lambda i, j: (0, j)),
            pl.BlockSpec((tm, tn), lambda i, j: (i, col["g_a"] // tn + j)),
            pl.BlockSpec((tm, tn), lambda i, j: (i, col["g_b"] // tn + j)),
        ],
        out_specs=pl.BlockSpec((tm, tn), lambda i, j: (i, j)),
        out_shape=jax.ShapeDtypeStruct((m, n), BF16),
        compiler_params=_cparams("parallel", "parallel"),
        name="branch_mix",
    )(ya, yb, wa, wb, proj, proj)


def _x1_kernel(x_ref, mix_ref, w_ref, g_ref, b_ref, o_ref, ob_ref, *, alpha):
    t = alpha * x_ref[...] + _dot(mix_ref[...], w_ref[...])
    x1 = _layer_norm(t, g_ref[...], b_ref[...])
    o_ref[...] = x1
    ob_ref[...] = x1.astype(BF16)


def _x1(x, mix, w, g, b, *, alpha, tm):
    m, d = x.shape
    return pl.pallas_call(
        functools.partial(_x1_kernel, alpha=alpha),
        grid=(m // tm,),
        in_specs=[
            pl.BlockSpec((tm, d), lambda i: (i, 0)),
            pl.BlockSpec((tm, d), lambda i: (i, 0)),
            pl.BlockSpec(w.shape, lambda i: (0, 0)),
            pl.BlockSpec(g.shape, lambda i: (0, 0)),
            pl.BlockSpec(b.shape, lambda i: (0, 0)),
        ],
        out_specs=[pl.BlockSpec((tm, d), lambda i: (i, 0)), pl.BlockSpec((tm, d), lambda i: (i, 0))],
        out_shape=[jax.ShapeDtypeStruct((m, d), F32), jax.ShapeDtypeStruct((m, d), BF16)],
        compiler_params=_cparams("parallel"),
        name="x1_out_ln",
    )(x, mix, w, g, b)


def _top_rows(s, k, with_rank=False):
    rows = []
    cur = s
    rank = jnp.full(s.shape, float(k), F32)
    for a in range(k):
        m = jnp.max(cur, axis=0, keepdims=True)
        rows.append(m)
        hit = cur == m
        if with_rank:
            rank = jnp.where(hit, float(a), rank)
        cur = jnp.where(hit, -jnp.inf, cur)
    return (rows, rank) if with_rank else rows


def _route_kernel(x_ref, wq_ref, k1_ref, k2_ref, c1_ref, w1_ref, r2_ref, e2_ref, q_scr):
    n_heads = k1_ref.shape[0]
    dk = k1_ref.shape[2]
    K = PEER_TOPK
    q_scr[...] = _dot(x_ref[...], wq_ref[...]).astype(BF16)
    for h in range(n_heads):
        q1 = q_scr[:, (2 * h) * dk:(2 * h + 1) * dk]
        q2 = q_scr[:, (2 * h + 1) * dk:(2 * h + 2) * dk]
        s1 = lax.dot_general(k1_ref[h], q1, NT_DIMS, preferred_element_type=F32)
        s2 = lax.dot_general(k2_ref[h], q2, NT_DIMS, preferred_element_type=F32)
        t1 = jnp.concatenate(_top_rows(s1, K), axis=0)
        rows2, rank2 = _top_rows(s2, K, with_rank=True)
        t2 = jnp.concatenate(rows2, axis=0)
        assert K == 2 * SUBLANES
        cand = jnp.concatenate(
            [t1[0:1] + t2] + [t1[a:a + 1] + t2[0:SUBLANES] for a in range(1, SUBLANES)]
            + [t1[SUBLANES:] + t2[0:1]], axis=0)
        thr = _top_rows(cand, K)[-1]
        m1 = t1[0:1]
        m2 = t2[0:1]
        zsum = jnp.sum(jnp.where(cand >= thr, jnp.exp(cand - (m1 + m2)), 0.0), axis=0, keepdims=True)
        cnt = jnp.zeros(s1.shape, F32)
        for b in range(K):
            cnt = cnt + jnp.where(s1 + t2[b:b + 1] >= thr, 1.0, 0.0)
        c1_ref[h] = cnt
        w1_ref[h] = jnp.exp(s1 - m1) / zsum
        r2_ref[h] = rank2.astype(BF16)
        e2_ref[h] = jnp.exp(s2 - m2).astype(BF16)


def _route(x1b, wq, k1, k2, *, tm):
    m, d = x1b.shape
    n_heads, n_keys, _ = k1.shape
    tok = lambda rows: pl.BlockSpec((n_heads, rows, tm), lambda i: (0, 0, i))
    shp = lambda rows, dt: jax.ShapeDtypeStruct((n_heads, rows, m), dt)
    return pl.pallas_call(
        _route_kernel,
        grid=(m // tm,),
        in_specs=[
            pl.BlockSpec((tm, d), lambda i: (i, 0)),
            pl.BlockSpec(wq.shape, lambda i: (0, 0)),
            pl.BlockSpec(k1.shape, lambda i: (0, 0, 0)),
            pl.BlockSpec(k2.shape, lambda i: (0, 0, 0)),
        ],
        out_specs=[tok(n_keys), tok(n_keys), tok(n_keys), tok(n_keys)],
        out_shape=[shp(n_keys, F32), shp(n_keys, F32), shp(n_keys, BF16), shp(n_keys, BF16)],
        scratch_shapes=[pltpu.VMEM((tm, wq.shape[1]), BF16)],
        compiler_params=_cparams("parallel"),
        name="peer_route",
    )(x1b, wq, k1, k2)


def _expert_kernel(x_ref, u_ref, v_ref, c1_ref, w1_ref, r2_ref, e2_ref, o_ref, coef, hid, rows):
    e = pl.program_id(1)
    n_tiles = pl.num_programs(1) - 1
    te = u_ref.shape[0]
    n_heads, n_keys, tm = c1_ref.shape
    per = te // n_keys
    slot = e % 2

    @pl.when(e == 0)
    def _():
        o_ref[...] = jnp.zeros_like(o_ref)
        coef[1] = jnp.zeros(coef.shape[1:], BF16)

    o_ref[...] += _dot(coef[1 - slot], v_ref[...])

    hid[...] = lax.dot_general(u_ref[...], x_ref[...], NT_DIMS, preferred_element_type=F32)
    tile = jnp.minimum(e, n_tiles - 1)
    for k in range(per):
        i1 = tile * per + k
        for h in range(n_heads):
            rows[k, h:h + 1, :] = c1_ref[h, pl.ds(i1, 1), :]
            rows[k, n_heads + h:n_heads + h + 1, :] = w1_ref[h, pl.ds(i1, 1), :]
    for k in range(per):
        ks = slice(k * n_keys, (k + 1) * n_keys)
        for c in range(tm // LANES):
            cs = slice(c * LANES, (c + 1) * LANES)
            gate = jnp.zeros((n_keys, LANES), BF16)
            for h in range(n_heads):
                cnt = rows[k, h:h + 1, cs].astype(BF16)
                w1 = rows[k, n_heads + h:n_heads + h + 1, cs].astype(BF16)
                gate = gate + jnp.where(r2_ref[h, :, cs] < cnt, e2_ref[h, :, cs] * w1, jnp.zeros_like(gate))
            hk = hid[ks, cs]
            gelu = 0.5 * hk * (1.0 + lax.erf(hk * (1.0 / math.sqrt(2.0))))
            prod = gate.astype(F32) * gelu
            coef[slot, cs, ks] = prod.T.astype(BF16)


def _experts(x1b, u, v, c1, w1, r2, e2, *, tm, te):
    m, d = x1b.shape
    n_tiles = u.shape[0] // te
    n_heads, n_keys, _ = c1.shape
    tok = pl.BlockSpec((n_heads, n_keys, tm), lambda i, e: (0, 0, i))
    return pl.pallas_call(
        _expert_kernel,
        grid=(m // tm, n_tiles + 1),
        in_specs=[
            pl.BlockSpec((tm, d), lambda i, e: (i, 0)),
            pl.BlockSpec((te, d), lambda i, e: (jnp.minimum(e, n_tiles - 1), 0)),
            pl.BlockSpec((te, d), lambda i, e: (jnp.maximum(e - 1, 0), 0)),
            tok, tok, tok, tok,
        ],
        out_specs=pl.BlockSpec((tm, d), lambda i, e: (i, 0)),
        out_shape=jax.ShapeDtypeStruct((m, d), F32),
        scratch_shapes=[pltpu.VMEM((2, tm, te), BF16),
                        pltpu.VMEM((te, tm), F32),
                        pltpu.VMEM((te // n_keys, 2 * n_heads, tm), F32)],
        compiler_params=_cparams("parallel", "arbitrary"),
        name="peer_experts",
    )(x1b, u, v, c1, w1, r2, e2)


def _final_kernel(x1_ref, peer_ref, p_ref, wg_ref, wp_ref, g_ref, b_ref, o_ref, *, alpha):
    x2 = _layer_norm(alpha * x1_ref[...] + peer_ref[...], g_ref[...], b_ref[...])
    gate = _sigmoid(_dot(x2.astype(BF16), wg_ref[...]))
    o_ref[...] = x2 + gate * _dot(p_ref[...].astype(BF16), wp_ref[...])


def _final(x1, peer, p, wg, wp, g, b, *, alpha, tm):
    m, d = x1.shape
    return pl.pallas_call(
        functools.partial(_final_kernel, alpha=alpha),
        grid=(m // tm,),
        in_specs=[
            pl.BlockSpec((tm, d), lambda i: (i, 0)),
            pl.BlockSpec((tm, d), lambda i: (i, 0)),
            pl.BlockSpec((tm, p.shape[1]), lambda i: (i, 0)),
            pl.BlockSpec(wg.shape, lambda i: (0, 0)),
            pl.BlockSpec(wp.shape, lambda i: (0, 0)),
            pl.BlockSpec(g.shape, lambda i: (0, 0)),
            pl.BlockSpec(b.shape, lambda i: (0, 0)),
        ],
        out_specs=pl.BlockSpec((tm, d), lambda i: (i, 0)),
        out_shape=jax.ShapeDtypeStruct((m, d), F32),
        compiler_params=_cparams("parallel"),
        name="ln2_ple",
    )(x1, peer, p, wg, wp, g, b)


def _tile(m, cap, mult):
    best = None
    for t in range(mult, min(m, cap) + 1, mult):
        if m % t == 0:
            best = t
    assert best is not None, (m, cap, mult)
    return best


TOKEN_TILE = 640
ROW_TILE = 320
EXPERT_TILE = 512


def _pad_lanes(v):
    return jnp.pad(v.astype(F32), (0, LANES - v.shape[0])).reshape(1, LANES)


def _layer(x, p, ssm_h, conv_buf, sc_buf, n_prompt, batch, seq, depth,
           w_in, ssd_conv_w, ssd_conv_b, ssd_dt_bias, ssd_a_log, ssd_d, ssd_norm_w,
           sc_conv_w, w_branch_ssd, w_branch_sc, w_out, ln1_g, ln1_b,
           peer_wq, peer_keys1, peer_keys2, peer_u, peer_v, ln2_g, ln2_b,
           ple_gate_w, ple_proj_w):
    m, d = x.shape
    nb = m - n_prompt
    n_heads = ssd_dt_bias.shape[0]
    hw, n_state = ssm_h.shape[1] * ssm_h.shape[2], ssm_h.shape[3]
    head_dim = ssm_h.shape[2]
    conv_dim = ssd_conv_w.shape[1]
    gn = (conv_dim - hw) // 2
    n_groups = gn // n_state
    scw = sc_conv_w.shape[1]
    alpha = (2.0 * depth) ** 0.25
    assert hw == scw == d and n_heads <= LANES and LANES % head_dim == 0 and n_state == LANES

    o_z, o_xbc, o_dt = 0, hw, hw + conv_dim
    o_scb = o_dt + n_heads
    o_scc, o_sch, o_ga, o_gb = o_scb + scw, o_scb + 2 * scw, o_scb + 3 * scw, o_scb + 3 * scw + d
    seg = lambda o, w: w_in[:, o:o + w]
    w_main = jnp.concatenate(
        [seg(o_z, hw), seg(o_xbc, hw), seg(o_scb, scw), seg(o_scc, scw), seg(o_sch, scw),
         seg(o_ga, d), seg(o_gb, d), seg(o_xbc + hw, gn), seg(o_xbc + hw + gn, gn)], axis=1).astype(BF16)
    col = {"z": 0, "xs": hw, "sc_b": 2 * hw, "sc_c": 3 * hw, "sc_h": 4 * hw, "g_a": 5 * hw,
           "g_b": 6 * hw, "B": 7 * hw, "C": 7 * hw + gn}
    w_dt = jnp.pad(seg(o_dt, n_heads), ((0, 0), (0, LANES - n_heads))).astype(BF16)

    tm = _tile(m, TOKEN_TILE, LANES)
    proj = _matmul(x, w_main, tm, _tile(w_main.shape[1], 1024, LANES), F32)
    dtraw = _matmul(x, w_dt, tm, LANES, F32)

    ch_head = jnp.arange(hw) // head_dim
    e1 = (jnp.arange(LANES)[:, None] == ch_head[None, :]).astype(BF16)
    e2 = (jnp.arange(LANES)[:, None] == (jnp.arange(n_heads * LANES) // LANES)[None, :]).astype(BF16)
    convb = ssd_conv_b.reshape(1, conv_dim)
    dtb, alog = _pad_lanes(ssd_dt_bias), _pad_lanes(ssd_a_log)
    de = jnp.repeat(ssd_d.astype(F32), head_dim).reshape(1, hw)
    normw = ssd_norm_w.reshape(1, hw)
    shp = dict(hw=hw, gn=gn, n_groups=n_groups, n_state=n_state, col=col)
    ya_p, h_p = _ssd_prompt(proj, dtraw, ssd_conv_w, convb, dtb, alog, de, normw, e1, e2,
                            batch=batch, seq=seq, head_dim=head_dim, **shp)
    cst = jnp.transpose(conv_buf, (1, 0, 2))
    ya_s, h_s = _ssd_sample(proj, dtraw, cst, ssm_h.reshape(nb, hw, n_state), ssd_conv_w, convb,
                            dtb, alog, de, normw, e1, row0=n_prompt, nb=nb, **shp)

    yb_p, sc_tail = _sc_prompt(proj, sc_conv_w, batch=batch, seq=seq, width=scw, col=col,
                               ts=min(seq, 256))
    yb_s, ch_s = _sc_sample(proj, jnp.transpose(sc_buf, (1, 0, 2)), sc_conv_w,
                            row0=n_prompt, nb=nb, width=scw, col=col)

    ya = jnp.concatenate([ya_p, ya_s], axis=0)
    yb = jnp.concatenate([yb_p, yb_s], axis=0)
    mix = _branch(ya, yb, w_branch_ssd.astype(BF16), w_branch_sc.astype(BF16), proj,
                  col=col, tm=tm, tn=_tile(d, 512, LANES))
    tm2 = _tile(m, ROW_TILE, SUBLANES)
    x1, x1b = _x1(x, mix, w_out.astype(BF16), ln1_g.reshape(1, d), ln1_b.reshape(1, d),
                  alpha=alpha, tm=tm2)

    route = _route(x1b, peer_wq.astype(BF16), peer_keys1.astype(BF16),
                              peer_keys2.astype(BF16), tm=tm)
    peer = _experts(x1b, peer_u.astype(BF16), peer_v.astype(BF16), *route,
                    tm=tm, te=EXPERT_TILE)

    y = _final(x1, peer, p, ple_gate_w.astype(BF16), ple_proj_w.astype(BF16),
               ln2_g.reshape(1, d), ln2_b.reshape(1, d), alpha=alpha, tm=tm2)

    k_ssd = ssd_conv_w.shape[0] - 1
    k_sc = sc_conv_w.shape[0] - 1
    xbc_cols = lambda rows: jnp.concatenate(
        [rows[..., col["xs"]:col["xs"] + hw], rows[..., col["B"]:col["B"] + gn],
         rows[..., col["C"]:col["C"] + gn]], axis=-1)
    conv_p = xbc_cols(jnp.stack([proj[(b + 1) * seq - k_ssd:(b + 1) * seq] for b in range(batch)]))
    conv_s = jnp.concatenate([conv_buf[:, 1:, :], xbc_cols(proj[n_prompt:])[:, None, :]], axis=1)
    sc_p = sc_tail[:, SUBLANES - k_sc:, :]
    sc_s = jnp.concatenate([sc_buf[:, 1:, :], ch_s[:, None, :]], axis=1)
    hshape = (-1, n_heads, head_dim, n_state)
    return y, conv_p, h_p.reshape(hshape), sc_p, conv_s, h_s.reshape(hshape), sc_s


def kernel(x_prompt, x_sample, p_prompt, p_sample, state_ssm, state_ssd_conv, state_shortconv, w_in, ssd_conv_w, ssd_conv_b, ssd_dt_bias, ssd_a_log, ssd_d, ssd_norm_w, sc_conv_w, w_branch_ssd, w_branch_sc, w_out, ln1_g, ln1_b, peer_wq, peer_keys1, peer_keys2, peer_u, peer_v, ln2_g, ln2_b, ple_gate_w, ple_proj_w):
    batch, seq, d = x_prompt.shape
    nb, dec_seq, _ = x_sample.shape
    assert dec_seq == 1 and seq % SSD_CHUNK == 0
    depth = w_in.shape[0]
    n_prompt = batch * seq
    x = jnp.concatenate([x_prompt.reshape(n_prompt, d), x_sample.reshape(nb, d)], axis=0)
    weights = (w_in, ssd_conv_w, ssd_conv_b, ssd_dt_bias, ssd_a_log, ssd_d, ssd_norm_w,
               sc_conv_w, w_branch_ssd, w_branch_sc, w_out, ln1_g, ln1_b,
               peer_wq, peer_keys1, peer_keys2, peer_u, peer_v, ln2_g, ln2_b,
               ple_gate_w, ple_proj_w)
    outs = [[] for _ in range(6)]
    for i in range(depth):
        p = jnp.concatenate([p_prompt[i].reshape(n_prompt, -1), p_sample[i].reshape(nb, -1)], axis=0)
        x, conv_p, h_p, sc_p, conv_s, h_s, sc_s = _layer(
            x, p, state_ssm[i], state_ssd_conv[i], state_shortconv[i], n_prompt, batch, seq, depth,
            *[w[i] for w in weights])
        for lst, val in zip(outs, (h_p, conv_p, sc_p, h_s, conv_s, sc_s)):
            lst.append(val)
    y_prompt = x[:n_prompt].reshape(batch, seq, d)
    y_sample = x[n_prompt:].reshape(nb, 1, d)
    return (y_prompt, y_sample) + tuple(jnp.stack(lst) for lst in outs)
```

```python
import functools
import math

import jax
import jax.numpy as jnp
from jax import lax
from jax.experimental import pallas as pl
from jax.experimental.pallas import tpu as pltpu

F32 = jnp.float32
BF16 = jnp.bfloat16
U32 = jnp.uint32

LANES = 128
SUBLANES = 8
BF16_ROWS = 16
PEER_TOPK = 16
SSD_CHUNK = 128
LN_EPS = 1e-5
RMS_EPS = 1e-5
VMEM_LIMIT = 56 * 1024 * 1024

NT_DIMS = (((1,), (1,)), ((), ()))
TN_DIMS = (((0,), (0,)), ((), ()))


def _cparams(*sem):
    return pltpu.CompilerParams(dimension_semantics=sem, vmem_limit_bytes=VMEM_LIMIT)


def _dot(a, b):
    return jnp.dot(a, b, preferred_element_type=F32)


def _split3(v):
    hi = v.astype(BF16)
    r = v - hi.astype(F32)
    mid = r.astype(BF16)
    lo = (r - mid.astype(F32)).astype(BF16)
    return hi, mid, lo


def _dot3_lhs(v, rhs_bf16):
    hi, mid, lo = _split3(v)
    return _dot(hi, rhs_bf16) + _dot(mid, rhs_bf16) + _dot(lo, rhs_bf16)


def _dot3_rhs(lhs_bf16, v):
    hi, mid, lo = _split3(v)
    return _dot(lhs_bf16, hi) + _dot(lhs_bf16, mid) + _dot(lhs_bf16, lo)


def _words(rows):
    return rows * jnp.dtype(BF16).itemsize // jnp.dtype(U32).itemsize


def _sigmoid(x):
    return 1.0 / (1.0 + jnp.exp(-x))


def _silu(x):
    return x * _sigmoid(x)


def _softplus(x):
    return jnp.maximum(x, 0.0) + jnp.log1p(jnp.exp(-jnp.abs(x)))


def _layer_norm(x, g, b):
    mu = jnp.mean(x, axis=-1, keepdims=True)
    xc = x - mu
    var = jnp.mean(xc * xc, axis=-1, keepdims=True)
    return xc * lax.rsqrt(var + LN_EPS) * g + b


def _mm_kernel(x_ref, w_ref, o_ref):
    o_ref[...] = _dot(x_ref[...].astype(BF16), w_ref[...]).astype(o_ref.dtype)


def _matmul(x, w, tm, tn, out_dtype):
    m, k = x.shape
    n = w.shape[1]
    return pl.pallas_call(
        _mm_kernel,
        grid=(m // tm, n // tn),
        in_specs=[pl.BlockSpec((tm, k), lambda i, j: (i, 0)),
                  pl.BlockSpec((k, tn), lambda i, j: (0, j))],
        out_specs=pl.BlockSpec((tm, tn), lambda i, j: (i, j)),
        out_shape=jax.ShapeDtypeStruct((m, n), out_dtype),
        compiler_params=_cparams("parallel", "parallel"),
        name="in_proj",
    )(x, w)


def _ssd_gate_norm(y, z, normw_ref, out_ref, n_groups):
    y = y * _silu(z)
    gw = y.shape[1] // n_groups
    for g in range(n_groups):
        sl = slice(g * gw, (g + 1) * gw)
        yg = y[:, sl]
        ms = jnp.mean(yg * yg, axis=-1, keepdims=True)
        out_ref[:, sl] = (yg * lax.rsqrt(ms + RMS_EPS) * normw_ref[:, sl]).astype(out_ref.dtype)


def _ssd_prompt_kernel(z_ref, xs_ref, b_ref, c_ref, dt_ref, convw_ref, convb_ref, dtb_ref,
                       alog_ref, de_ref, normw_ref, e1_ref, e2_ref,
                       ya_ref, hout_ref,
                       cbuf, act, h_scr, x_scr, xd_scr, eae_scr, acsb_scr, acst_scr, y_scr,
                       *, n_groups, n_state, head_dim):
    c = pl.program_id(1)
    L = SSD_CHUNK
    hw = xs_ref.shape[1]
    gn = b_ref.shape[1]
    w_all = hw + 2 * gn
    n_heads = hw // head_dim
    hpl = LANES // head_dim
    n_blk = n_heads // hpl
    blk_per_group = n_blk // n_groups
    taps = convw_ref.shape[0]

    @pl.when(c == 0)
    def _():
        h_scr[...] = jnp.zeros_like(h_scr)
        cbuf[0:SUBLANES, :] = jnp.zeros((SUBLANES, w_all), F32)

    cbuf[SUBLANES:SUBLANES + L, 0:hw] = xs_ref[...]
    cbuf[SUBLANES:SUBLANES + L, hw:hw + gn] = b_ref[...]
    cbuf[SUBLANES:SUBLANES + L, hw + gn:] = c_ref[...]

    cw = math.gcd(w_all, 512)
    for blk in range(w_all // cw):
        sl = slice(blk * cw, (blk + 1) * cw)
        acc = convb_ref[:, sl] + convw_ref[taps - 1:taps, sl] * cbuf[SUBLANES:SUBLANES + L, sl]
        for j in range(1, taps):
            acc = acc + convw_ref[taps - 1 - j:taps - j, sl] * cbuf[SUBLANES - j:SUBLANES - j + L, sl]
        act[:, sl] = _silu(acc)
    cbuf[0:SUBLANES, :] = cbuf[L:L + SUBLANES, :]

    dt = _softplus(dt_ref[...] + dtb_ref[...])
    a_neg = -jnp.exp(alog_ref[...])
    dta = dt * a_neg
    ri = lax.broadcasted_iota(jnp.int32, (L, L), 0)
    ci = lax.broadcasted_iota(jnp.int32, (L, L), 1)
    causal = ri >= ci
    tri = jnp.where(causal, 1.0, 0.0).astype(BF16)
    acs = _dot3_rhs(tri, dta)
    e1 = e1_ref[...]
    dte = _dot3_lhs(dt, e1)
    acs_p = _split3(acs)
    acse = _dot(acs_p[0], e1) + _dot(acs_p[1], e1) + _dot(acs_p[2], e1)
    e2 = e2_ref[...]
    acsb_scr[...] = _dot(acs_p[0], e2) + _dot(acs_p[1], e2) + _dot(acs_p[2], e2)
    acst_scr[...] = acs.T

    xdt = act[:, 0:hw] * dte
    x_scr[...] = xdt.astype(BF16)
    xd_scr[...] = (xdt * jnp.exp(acse[L - 1:L, :] - acse)).astype(BF16)
    eae_scr[...] = jnp.exp(acse)

    lane = lax.broadcasted_iota(jnp.int32, (L, LANES), 1)
    cb = None
    for j in range(n_blk):
        g = j // blk_per_group
        bsl = slice(hw + g * n_state, hw + (g + 1) * n_state)
        csl = slice(hw + gn + g * n_state, hw + gn + (g + 1) * n_state)
        bg = act[:, bsl].astype(BF16)
        cg = act[:, csl].astype(BF16)
        if j % blk_per_group == 0:
            cb = lax.dot_general(cg, bg, NT_DIMS, preferred_element_type=F32)
        psl = slice(j * LANES, (j + 1) * LANES)
        xp = x_scr[:, psl]
        ydiag = None
        cds = []
        for q in range(hpl):
            r = j * hpl + q
            ab = acsb_scr[:, r * LANES:(r + 1) * LANES]
            at = jnp.broadcast_to(acst_scr[r:r + 1, :], (L, L))
            lm = jnp.where(causal, jnp.exp(ab - at), 0.0)
            m = (cb * lm).astype(BF16)
            inhead = (lane >= q * head_dim) & (lane < (q + 1) * head_dim)
            xq = jnp.where(inhead, xp, jnp.zeros_like(xp))
            yq = _dot(m, xq)
            ydiag = yq if ydiag is None else ydiag + yq
            cds.append(jnp.broadcast_to(jnp.exp(acsb_scr[L - 1:L, r * LANES:(r + 1) * LANES]),
                                        (head_dim, LANES)))
        cd = jnp.concatenate(cds, axis=0)
        hp = h_scr[psl, :]
        yoff = lax.dot_general(cg, hp.astype(BF16), NT_DIMS, preferred_element_type=F32)
        yoff = yoff * eae_scr[:, psl]
        st = lax.dot_general(xd_scr[:, psl], bg, TN_DIMS, preferred_element_type=F32)
        h_scr[psl, :] = hp * cd + st
        y_scr[:, psl] = ydiag + yoff + act[:, psl] * de_ref[:, psl]

    _ssd_gate_norm(y_scr[...], z_ref[...], normw_ref, ya_ref, n_groups)

    @pl.when(c == pl.num_programs(1) - 1)
    def _():
        hout_ref[0] = h_scr[...]


def _ssd_prompt(proj, dtraw, convw, convb, dtb, alog, de, normw, e1, e2, *, batch, seq,
                hw, gn, n_groups, n_state, head_dim, col):
    L = SSD_CHUNK
    nc = seq // L
    n_heads = hw // head_dim
    w_all = hw + 2 * gn
    row = lambda b, c: b * nc + c
    const = lambda b, c: (0, 0)
    kern = functools.partial(_ssd_prompt_kernel, n_groups=n_groups, n_state=n_state, head_dim=head_dim)
    return pl.pallas_call(
        kern,
        grid=(batch, nc),
        in_specs=[
            pl.BlockSpec((L, hw), lambda b, c: (row(b, c), col["z"] // hw)),
            pl.BlockSpec((L, hw), lambda b, c: (row(b, c), col["xs"] // hw)),
            pl.BlockSpec((L, gn), lambda b, c: (row(b, c), col["B"] // gn)),
            pl.BlockSpec((L, gn), lambda b, c: (row(b, c), col["C"] // gn)),
            pl.BlockSpec((L, LANES), lambda b, c: (row(b, c), 0)),
            pl.BlockSpec(convw.shape, const),
            pl.BlockSpec(convb.shape, const),
            pl.BlockSpec(dtb.shape, const),
            pl.BlockSpec(alog.shape, const),
            pl.BlockSpec(de.shape, const),
            pl.BlockSpec(normw.shape, const),
            pl.BlockSpec(e1.shape, const),
            pl.BlockSpec(e2.shape, const),
        ],
        out_specs=[
            pl.BlockSpec((L, hw), lambda b, c: (row(b, c), 0)),
            pl.BlockSpec((1, hw, n_state), lambda b, c: (b, 0, 0)),
        ],
        out_shape=[
            jax.ShapeDtypeStruct((batch * seq, hw), BF16),
            jax.ShapeDtypeStruct((batch, hw, n_state), F32),
        ],
        scratch_shapes=[
            pltpu.VMEM((L + SUBLANES, w_all), F32),
            pltpu.VMEM((L, w_all), F32),
            pltpu.VMEM((hw, n_state), F32),
            pltpu.VMEM((L, hw), BF16),
            pltpu.VMEM((L, hw), BF16),
            pltpu.VMEM((L, hw), F32),
            pltpu.VMEM((L, n_heads * LANES), F32),
            pltpu.VMEM((L, L), F32),
            pltpu.VMEM((L, hw), F32),
        ],
        compiler_params=_cparams("parallel", "arbitrary"),
        name="ssd_prompt",
    )(proj, proj, proj, proj, dtraw, convw, convb, dtb, alog, de, normw, e1, e2)


def _ssd_sample_kernel(z_ref, xs_ref, b_ref, c_ref, dt_ref, cst_ref, h_ref, convw_ref, convb_ref,
                       dtb_ref, alog_ref, de_ref, normw_ref, e1_ref,
                       ya_ref, hout_ref, xbc, y_scr,
                       *, n_groups, n_state):
    bb = xs_ref.shape[0]
    hw = xs_ref.shape[1]
    gn = b_ref.shape[1]
    taps = convw_ref.shape[0]
    gw = hw // n_groups

    xbc[:, 0:hw] = xs_ref[...]
    xbc[:, hw:hw + gn] = b_ref[...]
    xbc[:, hw + gn:] = c_ref[...]
    acc = convb_ref[...] + convw_ref[taps - 1:taps, :] * xbc[...]
    for j in range(taps - 1):
        acc = acc + convw_ref[j:j + 1, :] * cst_ref[j]
    act = _silu(acc)
    xs = act[:, 0:hw]

    dt = _softplus(dt_ref[...] + dtb_ref[...])
    dec = jnp.exp(dt * (-jnp.exp(alog_ref[...])))
    e1 = e1_ref[...]
    dte = _dot3_lhs(dt, e1)
    dece = _dot3_lhs(dec, e1)
    xdt = xs * dte

    pieces = [p.astype(F32) for p in _split3(dece)] + [p.astype(F32) for p in _split3(xdt)]
    npc = len(pieces)
    stack = jnp.concatenate(pieces + [jnp.zeros((LANES - npc * bb, hw), F32)], axis=0)
    lt = stack.T.astype(BF16)

    krow = lax.broadcasted_iota(jnp.int32, (LANES, LANES), 0)
    rowid = lax.broadcasted_iota(jnp.int32, (bb, gw), 0)
    half = npc // 2
    y_scr[...] = jnp.zeros_like(y_scr)
    for s in range(bb):
        is_s = (krow % bb) == s
        sel_dec = jnp.where(is_s & (krow < half * bb), 1.0, 0.0).astype(BF16)
        sel_x = jnp.where(is_s & (krow >= half * bb) & (krow < npc * bb), 1.0, 0.0).astype(BF16)
        dec_b = _dot(lt, sel_dec)
        x_b = _dot(lt, sel_x)
        for g in range(n_groups):
            rows = slice(g * gw, (g + 1) * gw)
            brow = act[s:s + 1, hw + g * n_state:hw + (g + 1) * n_state]
            hn = h_ref[s, rows, :] * dec_b[rows, :] + x_b[rows, :] * brow
            hout_ref[s, rows, :] = hn
            cg = act[:, hw + gn + g * n_state:hw + gn + (g + 1) * n_state].astype(BF16)
            yg = lax.dot_general(cg, hn.astype(BF16), NT_DIMS, preferred_element_type=F32)
            y_scr[:, rows] = y_scr[:, rows] + jnp.where(rowid == s, yg, 0.0)

    y = y_scr[...] + xs * de_ref[...]
    _ssd_gate_norm(y, z_ref[...], normw_ref, ya_ref, n_groups)


def _ssd_sample(proj, dtraw, cst, h0, convw, convb, dtb, alog, de, normw, e1, *, row0, nb,
                hw, gn, n_groups, n_state, col):
    bb = SUBLANES
    w_all = hw + 2 * gn
    r0 = row0 // bb
    const = lambda i: (0, 0)
    kern = functools.partial(_ssd_sample_kernel, n_groups=n_groups, n_state=n_state)
    return pl.pallas_call(
        kern,
        grid=(nb // bb,),
        in_specs=[
            pl.BlockSpec((bb, hw), lambda i: (r0 + i, col["z"] // hw)),
            pl.BlockSpec((bb, hw), lambda i: (r0 + i, col["xs"] // hw)),
            pl.BlockSpec((bb, gn), lambda i: (r0 + i, col["B"] // gn)),
            pl.BlockSpec((bb, gn), lambda i: (r0 + i, col["C"] // gn)),
            pl.BlockSpec((bb, LANES), lambda i: (r0 + i, 0)),
            pl.BlockSpec((cst.shape[0], bb, w_all), lambda i: (0, i, 0)),
            pl.BlockSpec((bb, hw, n_state), lambda i: (i, 0, 0)),
            pl.BlockSpec(convw.shape, const),
            pl.BlockSpec(convb.shape, const),
            pl.BlockSpec(dtb.shape, const),
            pl.BlockSpec(alog.shape, const),
            pl.BlockSpec(de.shape, const),
            pl.BlockSpec(normw.shape, const),
            pl.BlockSpec(e1.shape, const),
        ],
        out_specs=[
            pl.BlockSpec((bb, hw), lambda i: (i, 0)),
            pl.BlockSpec((bb, hw, n_state), lambda i: (i, 0, 0)),
        ],
        out_shape=[
            jax.ShapeDtypeStruct((nb, hw), BF16),
            jax.ShapeDtypeStruct((nb, hw, n_state), F32),
        ],
        scratch_shapes=[pltpu.VMEM((bb, w_all), F32), pltpu.VMEM((bb, hw), F32)],
        compiler_params=_cparams("parallel"),
        name="ssd_sample",
    )(proj, proj, proj, proj, dtraw, cst, h0, convw, convb, dtb, alog, de, normw, e1)


def _sc_prompt_kernel(b_ref, c_ref, h_ref, w_ref, yb_ref, tail_ref, cbuf):
    j = pl.program_id(1)
    ts = b_ref.shape[0]
    taps = w_ref.shape[0]

    @pl.when(j == 0)
    def _():
        cbuf[0:SUBLANES, :] = jnp.zeros((SUBLANES, cbuf.shape[1]), F32)

    cbuf[SUBLANES:SUBLANES + ts, :] = c_ref[...] * h_ref[...]
    u = w_ref[taps - 1:taps, :] * cbuf[SUBLANES:SUBLANES + ts, :]
    for k in range(1, taps):
        u = u + w_ref[taps - 1 - k:taps - k, :] * cbuf[SUBLANES - k:SUBLANES - k + ts, :]
    yb_ref[...] = (b_ref[...] * u).astype(yb_ref.dtype)
    cbuf[0:SUBLANES, :] = cbuf[ts:ts + SUBLANES, :]

    @pl.when(j == pl.num_programs(1) - 1)
    def _():
        tail_ref[0] = cbuf[0:SUBLANES, :]


def _sc_prompt(proj, w, *, batch, seq, width, col, ts):
    nt = seq // ts
    row = lambda b, j: b * nt + j
    return pl.pallas_call(
        _sc_prompt_kernel,
        grid=(batch, nt),
        in_specs=[
            pl.BlockSpec((ts, width), lambda b, j: (row(b, j), col["sc_b"] // width)),
            pl.BlockSpec((ts, width), lambda b, j: (row(b, j), col["sc_c"] // width)),
            pl.BlockSpec((ts, width), lambda b, j: (row(b, j), col["sc_h"] // width)),
            pl.BlockSpec(w.shape, lambda b, j: (0, 0)),
        ],
        out_specs=[
            pl.BlockSpec((ts, width), lambda b, j: (row(b, j), 0)),
            pl.BlockSpec((1, SUBLANES, width), lambda b, j: (b, 0, 0)),
        ],
        out_shape=[
            jax.ShapeDtypeStruct((batch * seq, width), BF16),
            jax.ShapeDtypeStruct((batch, SUBLANES, width), F32),
        ],
        scratch_shapes=[pltpu.VMEM((ts + SUBLANES, width), F32)],
        compiler_params=_cparams("parallel", "arbitrary"),
        name="shortconv_prompt",
    )(proj, proj, proj, w)


def _sc_sample_kernel(b_ref, c_ref, h_ref, st_ref, w_ref, yb_ref, ch_ref):
    taps = w_ref.shape[0]
    ch = c_ref[...] * h_ref[...]
    u = w_ref[taps - 1:taps, :] * ch
    for k in range(taps - 1):
        u = u + w_ref[k:k + 1, :] * st_ref[k]
    yb_ref[...] = (b_ref[...] * u).astype(yb_ref.dtype)
    ch_ref[...] = ch


def _sc_sample(proj, st, w, *, row0, nb, width, col):
    r0 = row0 // nb
    return pl.pallas_call(
        _sc_sample_kernel,
        grid=(1,),
        in_specs=[
            pl.BlockSpec((nb, width), lambda i: (r0, col["sc_b"] // width)),
            pl.BlockSpec((nb, width), lambda i: (r0, col["sc_c"] // width)),
            pl.BlockSpec((nb, width), lambda i: (r0, col["sc_h"] // width)),
            pl.BlockSpec(st.shape, lambda i: (0, 0, 0)),
            pl.BlockSpec(w.shape, lambda i: (0, 0)),
        ],
        out_specs=[pl.BlockSpec((nb, width), lambda i: (0, 0)),
                   pl.BlockSpec((nb, width), lambda i: (0, 0))],
        out_shape=[jax.ShapeDtypeStruct((nb, width), BF16),
                   jax.ShapeDtypeStruct((nb, width), F32)],
        compiler_params=_cparams("arbitrary"),
        name="shortconv_sample",
    )(proj, proj, proj, st, w)


def _branch_kernel(ya_ref, yb_ref, wa_ref, wb_ref, ga_ref, gb_ref, o_ref):
    ta = _dot(ya_ref[...], wa_ref[...])
    tb = _dot(yb_ref[...], wb_ref[...])
    o_ref[...] = (_sigmoid(ga_ref[...]) * ta + _sigmoid(gb_ref[...]) * tb).astype(o_ref.dtype)


def _branch(ya, yb, wa, wb, proj, *, col, tm, tn):
    m, k = ya.shape
    n = wa.shape[1]
    return pl.pallas_call(
        _branch_kernel,
        grid=(m // tm, n // tn),
        in_specs=[
            pl.BlockSpec((tm, k), lambda i, j: (i, 0)),
            pl.BlockSpec((tm, yb.shape[1]), lambda i, j: (i, 0)),
            pl.BlockSpec((k, tn), lambda i, j: (0, j)),
            pl.BlockSpec((yb.shape[1], tn), lambda i, j: (0, j)),
            pl.BlockSpec((tm, tn), lambda i, j: (i, col["g_a"] // tn + j)),
            pl.BlockSpec((tm, tn), lambda i, j: (i, col["g_b"] // tn + j)),
        ],
        out_specs=pl.BlockSpec((tm, tn), lambda i, j: (i, j)),
        out_shape=jax.ShapeDtypeStruct((m, n), BF16),
        compiler_params=_cparams("parallel", "parallel"),
        name="branch_mix",
    )(ya, yb, wa, wb, proj, proj)


def _x1_kernel(x_ref, mix_ref, w_ref, g_ref, b_ref, o_ref, ob_ref, *, alpha):
    t = alpha * x_ref[...] + _dot(mix_ref[...], w_ref[...])
    x1 = _layer_norm(t, g_ref[...], b_ref[...])
    o_ref[...] = x1
    ob_ref[...] = x1.astype(BF16)


def _x1(x, mix, w, g, b, *, alpha, tm):
    m, d = x.shape
    return pl.pallas_call(
        functools.partial(_x1_kernel, alpha=alpha),
        grid=(m // tm,),
        in_specs=[
            pl.BlockSpec((tm, d), lambda i: (i, 0)),
            pl.BlockSpec((tm, d), lambda i: (i, 0)),
            pl.BlockSpec(w.shape, lambda i: (0, 0)),
            pl.BlockSpec(g.shape, lambda i: (0, 0)),
            pl.BlockSpec(b.shape, lambda i: (0, 0)),
        ],
        out_specs=[pl.BlockSpec((tm, d), lambda i: (i, 0)), pl.BlockSpec((tm, d), lambda i: (i, 0))],
        out_shape=[jax.ShapeDtypeStruct((m, d), F32), jax.ShapeDtypeStruct((m, d), BF16)],
        compiler_params=_cparams("parallel"),
        name="x1_out_ln",
    )(x, mix, w, g, b)


def _top_rows(s, k, with_rank=False):
    rows = []
    cur = s
    rank = jnp.full(s.shape, float(k), F32)
    for a in range(k):
        m = jnp.max(cur, axis=0, keepdims=True)
        rows.append(m)
        hit = cur == m
        if with_rank:
            rank = jnp.where(hit, float(a), rank)
        cur = jnp.where(hit, -jnp.inf, cur)
    return (rows, rank) if with_rank else rows


def _route_kernel(x_ref, wq_ref, k1_ref, k2_ref, c1_ref, w1_ref, r2_ref, e2_ref, q_scr):
    n_heads = k1_ref.shape[0]
    dk = k1_ref.shape[2]
    K = PEER_TOPK
    q_scr[...] = _dot(x_ref[...], wq_ref[...]).astype(BF16)
    for h in range(n_heads):
        q1 = q_scr[:, (2 * h) * dk:(2 * h + 1) * dk]
        q2 = q_scr[:, (2 * h + 1) * dk:(2 * h + 2) * dk]
        s1 = lax.dot_general(k1_ref[h], q1, NT_DIMS, preferred_element_type=F32)
        s2 = lax.dot_general(k2_ref[h], q2, NT_DIMS, preferred_element_type=F32)
        t1 = jnp.concatenate(_top_rows(s1, K), axis=0)
        rows2, rank2 = _top_rows(s2, K, with_rank=True)
        t2 = jnp.concatenate(rows2, axis=0)
        assert K == 2 * SUBLANES
        cand = jnp.concatenate(
            [t1[0:1] + t2] + [t1[a:a + 1] + t2[0:SUBLANES] for a in range(1, SUBLANES)]
            + [t1[SUBLANES:] + t2[0:1]], axis=0)
        thr = _top_rows(cand, K)[-1]
        m1 = t1[0:1]
        m2 = t2[0:1]
        zsum = jnp.sum(jnp.where(cand >= thr, jnp.exp(cand - (m1 + m2)), 0.0), axis=0, keepdims=True)
        cnt = jnp.zeros(s1.shape, F32)
        for b in range(K):
            cnt = cnt + jnp.where(s1 + t2[b:b + 1] >= thr, 1.0, 0.0)
        c1_ref[h] = cnt
        w1_ref[h] = jnp.exp(s1 - m1) / zsum
        r2_ref[h] = pltpu.bitcast(rank2.astype(BF16), U32)
        e2_ref[h] = pltpu.bitcast(jnp.exp(s2 - m2).astype(BF16), U32)


def _route(x1b, wq, k1, k2, *, tm):
    m, d = x1b.shape
    n_heads, n_keys, _ = k1.shape
    tok = lambda rows: pl.BlockSpec((n_heads, rows, tm), lambda i: (0, 0, i))
    shp = lambda rows, dt: jax.ShapeDtypeStruct((n_heads, rows, m), dt)
    return pl.pallas_call(
        _route_kernel,
        grid=(m // tm,),
        in_specs=[
            pl.BlockSpec((tm, d), lambda i: (i, 0)),
            pl.BlockSpec(wq.shape, lambda i: (0, 0)),
            pl.BlockSpec(k1.shape, lambda i: (0, 0, 0)),
            pl.BlockSpec(k2.shape, lambda i: (0, 0, 0)),
        ],
        out_specs=[tok(n_keys), tok(n_keys), tok(_words(n_keys)), tok(_words(n_keys))],
        out_shape=[shp(n_keys, F32), shp(n_keys, F32), shp(_words(n_keys), U32), shp(_words(n_keys), U32)],
        scratch_shapes=[pltpu.VMEM((tm, wq.shape[1]), BF16)],
        compiler_params=_cparams("parallel"),
        name="peer_route",
    )(x1b, wq, k1, k2)


def _expert_kernel(x_ref, u_ref, vt_ref, c1_ref, w1_ref, r2_ref, e2_ref, o_ref, acc, coef, hid, rows):
    e = pl.program_id(1)
    n_tiles = pl.num_programs(1) - 1
    te = u_ref.shape[0]
    n_heads, n_keys, tm = c1_ref.shape
    per = te // n_keys
    half = _words(n_keys)
    slot = e % 2

    @pl.when(e == 0)
    def _():
        acc[...] = jnp.zeros_like(acc)
        coef[1] = jnp.zeros(coef.shape[1:], U32)

    acc[...] += _dot(vt_ref[...], pltpu.bitcast(coef[1 - slot], BF16))

    hid[...] = lax.dot_general(u_ref[...], x_ref[...], NT_DIMS, preferred_element_type=F32)
    tile = jnp.minimum(e, n_tiles - 1)
    for k in range(per):
        i1 = tile * per + k
        for h in range(n_heads):
            rows[k, h:h + 1, :] = c1_ref[h, pl.ds(i1, 1), :]
            rows[k, n_heads + h:n_heads + h + 1, :] = w1_ref[h, pl.ds(i1, 1), :]
    for k in range(per):
        for c in range(tm // LANES):
            cs = slice(c * LANES, (c + 1) * LANES)
            gate = jnp.zeros((n_keys, LANES), BF16)
            for h in range(n_heads):
                cnt = rows[k, h:h + 1, cs].astype(BF16)
                w1 = rows[k, n_heads + h:n_heads + h + 1, cs].astype(BF16)
                r2 = pltpu.bitcast(r2_ref[h, :, cs], BF16)
                e2 = pltpu.bitcast(e2_ref[h, :, cs], BF16)
                gate = gate + jnp.where(r2 < cnt, e2 * w1, jnp.zeros_like(gate))
            hk = hid[k * n_keys:(k + 1) * n_keys, cs]
            gelu = 0.5 * hk * (1.0 + lax.erf(hk * (1.0 / math.sqrt(2.0))))
            coef[slot, k * half:(k + 1) * half, cs] = pltpu.bitcast(gate * gelu.astype(BF16), U32)

    @pl.when(e == n_tiles)
    def _():
        o_ref[...] = acc[...].T.astype(o_ref.dtype)


def _experts(x1b, u, vt, c1, w1, r2, e2, *, tm, te):
    m, d = x1b.shape
    n_tiles = u.shape[0] // te
    n_heads, n_keys, _ = c1.shape
    tok = lambda rows: pl.BlockSpec((n_heads, rows, tm), lambda i, e: (0, 0, i))
    return pl.pallas_call(
        _expert_kernel,
        grid=(m // tm, n_tiles + 1),
        in_specs=[
            pl.BlockSpec((tm, d), lambda i, e: (i, 0)),
            pl.BlockSpec((te, d), lambda i, e: (jnp.minimum(e, n_tiles - 1), 0)),
            pl.BlockSpec((d, te), lambda i, e: (0, jnp.maximum(e - 1, 0))),
            tok(n_keys), tok(n_keys), tok(_words(n_keys)), tok(_words(n_keys)),
        ],
        out_specs=pl.BlockSpec((tm, d), lambda i, e: (i, 0)),
        out_shape=jax.ShapeDtypeStruct((m, d), BF16),
        scratch_shapes=[pltpu.VMEM((d, tm), F32),
                        pltpu.VMEM((2, _words(te), tm), U32),
                        pltpu.VMEM((te, tm), F32),
                        pltpu.VMEM((te // n_keys, 2 * n_heads, tm), F32)],
        compiler_params=_cparams("parallel", "arbitrary"),
        name="peer_experts",
    )(x1b, u, vt, c1, w1, r2, e2)


def _final_kernel(x1_ref, peer_ref, p_ref, wg_ref, wp_ref, g_ref, b_ref, o_ref, *, alpha):
    x2 = _layer_norm(alpha * x1_ref[...] + peer_ref[...], g_ref[...], b_ref[...])
    gate = _sigmoid(_dot(x2.astype(BF16), wg_ref[...]))
    o_ref[...] = x2 + gate * _dot(p_ref[...].astype(BF16), wp_ref[...])


def _final(x1, peer, p, wg, wp, g, b, *, alpha, tm):
    m, d = x1.shape
    return pl.pallas_call(
        functools.partial(_final_kernel, alpha=alpha),
        grid=(m // tm,),
        in_specs=[
            pl.BlockSpec((tm, d), lambda i: (i, 0)),
            pl.BlockSpec((tm, d), lambda i: (i, 0)),
            pl.BlockSpec((tm, p.shape[1]), lambda i: (i, 0)),
            pl.BlockSpec(wg.shape, lambda i: (0, 0)),
            pl.BlockSpec(wp.shape, lambda i: (0, 0)),
            pl.BlockSpec(g.shape, lambda i: (0, 0)),
            pl.BlockSpec(b.shape, lambda i: (0, 0)),
        ],
        out_specs=pl.BlockSpec((tm, d), lambda i: (i, 0)),
        out_shape=jax.ShapeDtypeStruct((m, d), F32),
        compiler_params=_cparams("parallel"),
        name="ln2_ple",
    )(x1, peer, p, wg, wp, g, b)


def _tile(m, cap, mult):
    best = None
    for t in range(mult, min(m, cap) + 1, mult):
        if m % t == 0:
            best = t
    assert best is not None, (m, cap, mult)
    return best


TOKEN_TILE = 640
ROW_TILE = 320
EXPERT_TILE = 512
PEER_TOKEN_TILE = 768


def _pad_lanes(v):
    return jnp.pad(v.astype(F32), (0, LANES - v.shape[0])).reshape(1, LANES)


def _layer(x, p, ssm_h, conv_buf, sc_buf, n_prompt, batch, seq, depth,
           w_in, ssd_conv_w, ssd_conv_b, ssd_dt_bias, ssd_a_log, ssd_d, ssd_norm_w,
           sc_conv_w, w_branch_ssd, w_branch_sc, w_out, ln1_g, ln1_b,
           peer_wq, peer_keys1, peer_keys2, peer_u, peer_v, ln2_g, ln2_b,
           ple_gate_w, ple_proj_w):
    m, d = x.shape
    nb = m - n_prompt
    n_heads = ssd_dt_bias.shape[0]
    hw, n_state = ssm_h.shape[1] * ssm_h.shape[2], ssm_h.shape[3]
    head_dim = ssm_h.shape[2]
    conv_dim = ssd_conv_w.shape[1]
    gn = (conv_dim - hw) // 2
    n_groups = gn // n_state
    scw = sc_conv_w.shape[1]
    alpha = (2.0 * depth) ** 0.25
    assert hw == scw == d and n_heads <= LANES and LANES % head_dim == 0 and n_state == LANES

    o_z, o_xbc, o_dt = 0, hw, hw + conv_dim
    o_scb = o_dt + n_heads
    o_scc, o_sch, o_ga, o_gb = o_scb + scw, o_scb + 2 * scw, o_scb + 3 * scw, o_scb + 3 * scw + d
    seg = lambda o, w: w_in[:, o:o + w]
    w_main = jnp.concatenate(
        [seg(o_z, hw), seg(o_xbc, hw), seg(o_scb, scw), seg(o_scc, scw), seg(o_sch, scw),
         seg(o_ga, d), seg(o_gb, d), seg(o_xbc + hw, gn), seg(o_xbc + hw + gn, gn)], axis=1).astype(BF16)
    col = {"z": 0, "xs": hw, "sc_b": 2 * hw, "sc_c": 3 * hw, "sc_h": 4 * hw, "g_a": 5 * hw,
           "g_b": 6 * hw, "B": 7 * hw, "C": 7 * hw + gn}
    w_dt = jnp.pad(seg(o_dt, n_heads), ((0, 0), (0, LANES - n_heads))).astype(BF16)

    tm = _tile(m, TOKEN_TILE, LANES)
    proj = _matmul(x, w_main, tm, _tile(w_main.shape[1], 1024, LANES), F32)
    dtraw = _matmul(x, w_dt, tm, LANES, F32)

    ch_head = jnp.arange(hw) // head_dim
    e1 = (jnp.arange(LANES)[:, None] == ch_head[None, :]).astype(BF16)
    e2 = (jnp.arange(LANES)[:, None] == (jnp.arange(n_heads * LANES) // LANES)[None, :]).astype(BF16)
    convb = ssd_conv_b.reshape(1, conv_dim)
    dtb, alog = _pad_lanes(ssd_dt_bias), _pad_lanes(ssd_a_log)
    de = jnp.repeat(ssd_d.astype(F32), head_dim).reshape(1, hw)
    normw = ssd_norm_w.reshape(1, hw)
    shp = dict(hw=hw, gn=gn, n_groups=n_groups, n_state=n_state, col=col)
    ya_p, h_p = _ssd_prompt(proj, dtraw, ssd_conv_w, convb, dtb, alog, de, normw, e1, e2,
                            batch=batch, seq=seq, head_dim=head_dim, **shp)
    cst = jnp.transpose(conv_buf, (1, 0, 2))
    ya_s, h_s = _ssd_sample(proj, dtraw, cst, ssm_h.reshape(nb, hw, n_state), ssd_conv_w, convb,
                            dtb, alog, de, normw, e1, row0=n_prompt, nb=nb, **shp)

    yb_p, sc_tail = _sc_prompt(proj, sc_conv_w, batch=batch, seq=seq, width=scw, col=col,
                               ts=min(seq, 256))
    yb_s, ch_s = _sc_sample(proj, jnp.transpose(sc_buf, (1, 0, 2)), sc_conv_w,
                            row0=n_prompt, nb=nb, width=scw, col=col)

    ya = jnp.concatenate([ya_p, ya_s], axis=0)
    yb = jnp.concatenate([yb_p, yb_s], axis=0)
    mix = _branch(ya, yb, w_branch_ssd.astype(BF16), w_branch_sc.astype(BF16), proj,
                  col=col, tm=tm, tn=_tile(d, 512, LANES))
    tm2 = _tile(m, ROW_TILE, SUBLANES)
    x1, x1b = _x1(x, mix, w_out.astype(BF16), ln1_g.reshape(1, d), ln1_b.reshape(1, d),
                  alpha=alpha, tm=tm2)

    tmp = PEER_TOKEN_TILE
    x1p = jnp.pad(x1b, ((0, -m % tmp), (0, 0)))
    route = _route(x1p, peer_wq.astype(BF16), peer_keys1.astype(BF16), peer_keys2.astype(BF16), tm=tmp)
    peer = _experts(x1p, peer_u.astype(BF16), peer_v.T.astype(BF16), *route, tm=tmp, te=EXPERT_TILE)

    y = _final(x1, peer, p, ple_gate_w.astype(BF16), ple_proj_w.astype(BF16),
               ln2_g.reshape(1, d), ln2_b.reshape(1, d), alpha=alpha, tm=tm2)

    k_ssd = ssd_conv_w.shape[0] - 1
    k_sc = sc_conv_w.shape[0] - 1
    xbc_cols = lambda rows: jnp.concatenate(
        [rows[..., col["xs"]:col["xs"] + hw], rows[..., col["B"]:col["B"] + gn],
         rows[..., col["C"]:col["C"] + gn]], axis=-1)
    conv_p = xbc_cols(jnp.stack([proj[(b + 1) * seq - k_ssd:(b + 1) * seq] for b in range(batch)]))
    conv_s = jnp.concatenate([conv_buf[:, 1:, :], xbc_cols(proj[n_prompt:])[:, None, :]], axis=1)
    sc_p = sc_tail[:, SUBLANES - k_sc:, :]
    sc_s = jnp.concatenate([sc_buf[:, 1:, :], ch_s[:, None, :]], axis=1)
    hshape = (-1, n_heads, head_dim, n_state)
    return y, conv_p, h_p.reshape(hshape), sc_p, conv_s, h_s.reshape(hshape), sc_s


def kernel(x_prompt, x_sample, p_prompt, p_sample, state_ssm, state_ssd_conv, state_shortconv, w_in, ssd_conv_w, ssd_conv_b, ssd_dt_bias, ssd_a_log, ssd_d, ssd_norm_w, sc_conv_w, w_branch_ssd, w_branch_sc, w_out, ln1_g, ln1_b, peer_wq, peer_keys1, peer_keys2, peer_u, peer_v, ln2_g, ln2_b, ple_gate_w, ple_proj_w):
    batch, seq, d = x_prompt.shape
    nb, dec_seq, _ = x_sample.shape
    assert dec_seq == 1 and seq % SSD_CHUNK == 0
    depth = w_in.shape[0]
    n_prompt = batch * seq
    x = jnp.concatenate([x_prompt.reshape(n_prompt, d), x_sample.reshape(nb, d)], axis=0)
    weights = (w_in, ssd_conv_w, ssd_conv_b, ssd_dt_bias, ssd_a_log, ssd_d, ssd_norm_w,
               sc_conv_w, w_branch_ssd, w_branch_sc, w_out, ln1_g, ln1_b,
               peer_wq, peer_keys1, peer_keys2, peer_u, peer_v, ln2_g, ln2_b,
               ple_gate_w, ple_proj_w)
    outs = [[] for _ in range(6)]
    for i in range(depth):
        p = jnp.concatenate([p_prompt[i].reshape(n_prompt, -1), p_sample[i].reshape(nb, -1)], axis=0)
        x, conv_p, h_p, sc_p, conv_s, h_s, sc_s = _layer(
            x, p, state_ssm[i], state_ssd_conv[i], state_shortconv[i], n_prompt, batch, seq, depth,
            *[w[i] for w in weights])
        for lst, val in zip(outs, (h_p, conv_p, sc_p, h_s, conv_s, sc_s)):
            lst.append(val)
    y_prompt = x[:n_prompt].reshape(batch, seq, d)
    y_sample = x[n_prompt:].reshape(nb, 1, d)
    return (y_prompt, y_sample) + tuple(jnp.stack(lst) for lst in outs)
```

```python
import functools
import math

import jax
import jax.numpy as jnp
from jax import lax
from jax.experimental import pallas as pl
from jax.experimental.pallas import tpu as pltpu

F32 = jnp.float32
BF16 = jnp.bfloat16
U32 = jnp.uint32

LANES = 128
SUBLANES = 8
BF16_ROWS = 16
PEER_TOPK = 16
SSD_CHUNK = 128
LN_EPS = 1e-5
RMS_EPS = 1e-5
VMEM_LIMIT = 56 * 1024 * 1024

NT_DIMS = (((1,), (1,)), ((), ()))
TN_DIMS = (((0,), (0,)), ((), ()))


def _cparams(*sem, flags=None):
    return pltpu.CompilerParams(dimension_semantics=sem, vmem_limit_bytes=VMEM_LIMIT, flags=flags)


def _dot(a, b):
    return jnp.dot(a, b, preferred_element_type=F32)


def _split3(v):
    hi = v.astype(BF16)
    r = v - hi.astype(F32)
    mid = r.astype(BF16)
    lo = (r - mid.astype(F32)).astype(BF16)
    return hi, mid, lo


def _dot3_lhs(v, rhs_bf16):
    hi, mid, lo = _split3(v)
    return _dot(hi, rhs_bf16) + _dot(mid, rhs_bf16) + _dot(lo, rhs_bf16)


def _dot3_rhs(lhs_bf16, v):
    hi, mid, lo = _split3(v)
    return _dot(lhs_bf16, hi) + _dot(lhs_bf16, mid) + _dot(lhs_bf16, lo)


def _words(rows):
    return rows * jnp.dtype(BF16).itemsize // jnp.dtype(U32).itemsize


def _sigmoid(x):
    return 1.0 / (1.0 + jnp.exp(-x))


def _silu(x):
    return x * _sigmoid(x)


def _softplus(x):
    return jnp.maximum(x, 0.0) + jnp.log1p(jnp.exp(-jnp.abs(x)))


def _layer_norm(x, g, b):
    mu = jnp.mean(x, axis=-1, keepdims=True)
    xc = x - mu
    var = jnp.mean(xc * xc, axis=-1, keepdims=True)
    return xc * lax.rsqrt(var + LN_EPS) * g + b


def _mm_kernel(x_ref, w_ref, o_ref):
    o_ref[...] = _dot(x_ref[...].astype(BF16), w_ref[...]).astype(o_ref.dtype)


def _matmul(x, w, tm, tn, out_dtype):
    m, k = x.shape
    n = w.shape[1]
    return pl.pallas_call(
        _mm_kernel,
        grid=(m // tm, n // tn),
        in_specs=[pl.BlockSpec((tm, k), lambda i, j: (i, 0)),
                  pl.BlockSpec((k, tn), lambda i, j: (0, j))],
        out_specs=pl.BlockSpec((tm, tn), lambda i, j: (i, j)),
        out_shape=jax.ShapeDtypeStruct((m, n), out_dtype),
        compiler_params=_cparams("parallel", "parallel"),
        name="in_proj",
    )(x, w)


def _ssd_gate_norm(y, z, normw_ref, out_ref, n_groups):
    y = y * _silu(z)
    gw = y.shape[1] // n_groups
    for g in range(n_groups):
        sl = slice(g * gw, (g + 1) * gw)
        yg = y[:, sl]
        ms = jnp.mean(yg * yg, axis=-1, keepdims=True)
        out_ref[:, sl] = (yg * lax.rsqrt(ms + RMS_EPS) * normw_ref[:, sl]).astype(out_ref.dtype)


def _ssd_prompt_kernel(z_ref, xs_ref, b_ref, c_ref, dt_ref, convw_ref, convb_ref, dtb_ref,
                       alog_ref, de_ref, normw_ref, e1_ref, e2_ref,
                       ya_ref, hout_ref,
                       cbuf, act, h_scr, x_scr, xd_scr, eae_scr, acsb_scr, acst_scr, y_scr,
                       *, n_groups, n_state, head_dim):
    c = pl.program_id(1)
    L = SSD_CHUNK
    hw = xs_ref.shape[1]
    gn = b_ref.shape[1]
    w_all = hw + 2 * gn
    n_heads = hw // head_dim
    hpl = LANES // head_dim
    n_blk = n_heads // hpl
    blk_per_group = n_blk // n_groups
    taps = convw_ref.shape[0]

    @pl.when(c == 0)
    def _():
        h_scr[...] = jnp.zeros_like(h_scr)
        cbuf[0:SUBLANES, :] = jnp.zeros((SUBLANES, w_all), F32)

    cbuf[SUBLANES:SUBLANES + L, 0:hw] = xs_ref[...]
    cbuf[SUBLANES:SUBLANES + L, hw:hw + gn] = b_ref[...]
    cbuf[SUBLANES:SUBLANES + L, hw + gn:] = c_ref[...]

    cw = math.gcd(w_all, 512)
    for blk in range(w_all // cw):
        sl = slice(blk * cw, (blk + 1) * cw)
        acc = convb_ref[:, sl] + convw_ref[taps - 1:taps, sl] * cbuf[SUBLANES:SUBLANES + L, sl]
        for j in range(1, taps):
            acc = acc + convw_ref[taps - 1 - j:taps - j, sl] * cbuf[SUBLANES - j:SUBLANES - j + L, sl]
        act[:, sl] = _silu(acc)
    cbuf[0:SUBLANES, :] = cbuf[L:L + SUBLANES, :]

    dt = _softplus(dt_ref[...] + dtb_ref[...])
    a_neg = -jnp.exp(alog_ref[...])
    dta = dt * a_neg
    ri = lax.broadcasted_iota(jnp.int32, (L, L), 0)
    ci = lax.broadcasted_iota(jnp.int32, (L, L), 1)
    causal = ri >= ci
    tri = jnp.where(causal, 1.0, 0.0).astype(BF16)
    acs = _dot3_rhs(tri, dta)
    e1 = e1_ref[...]
    dte = _dot3_lhs(dt, e1)
    acs_p = _split3(acs)
    acse = _dot(acs_p[0], e1) + _dot(acs_p[1], e1) + _dot(acs_p[2], e1)
    e2 = e2_ref[...]
    acsb_scr[...] = _dot(acs_p[0], e2) + _dot(acs_p[1], e2) + _dot(acs_p[2], e2)
    acst_scr[...] = acs.T

    xdt = act[:, 0:hw] * dte
    x_scr[...] = xdt.astype(BF16)
    xd_scr[...] = (xdt * jnp.exp(acse[L - 1:L, :] - acse)).astype(BF16)
    eae_scr[...] = jnp.exp(acse)

    lane = lax.broadcasted_iota(jnp.int32, (L, LANES), 1)
    cb = None
    for j in range(n_blk):
        g = j // blk_per_group
        bsl = slice(hw + g * n_state, hw + (g + 1) * n_state)
        csl = slice(hw + gn + g * n_state, hw + gn + (g + 1) * n_state)
        bg = act[:, bsl].astype(BF16)
        cg = act[:, csl].astype(BF16)
        if j % blk_per_group == 0:
            cb = lax.dot_general(cg, bg, NT_DIMS, preferred_element_type=F32)
        psl = slice(j * LANES, (j + 1) * LANES)
        xp = x_scr[:, psl]
        ydiag = None
        cds = []
        for q in range(hpl):
            r = j * hpl + q
            ab = acsb_scr[:, r * LANES:(r + 1) * LANES]
            at = jnp.broadcast_to(acst_scr[r:r + 1, :], (L, L))
            lm = jnp.where(causal, jnp.exp(ab - at), 0.0)
            m = (cb * lm).astype(BF16)
            inhead = (lane >= q * head_dim) & (lane < (q + 1) * head_dim)
            xq = jnp.where(inhead, xp, jnp.zeros_like(xp))
            yq = _dot(m, xq)
            ydiag = yq if ydiag is None else ydiag + yq
            cds.append(jnp.broadcast_to(jnp.exp(acsb_scr[L - 1:L, r * LANES:(r + 1) * LANES]),
                                        (head_dim, LANES)))
        cd = jnp.concatenate(cds, axis=0)
        hp = h_scr[psl, :]
        yoff = lax.dot_general(cg, hp.astype(BF16), NT_DIMS, preferred_element_type=F32)
        yoff = yoff * eae_scr[:, psl]
        st = lax.dot_general(xd_scr[:, psl], bg, TN_DIMS, preferred_element_type=F32)
        h_scr[psl, :] = hp * cd + st
        y_scr[:, psl] = ydiag + yoff + act[:, psl] * de_ref[:, psl]

    _ssd_gate_norm(y_scr[...], z_ref[...], normw_ref, ya_ref, n_groups)

    @pl.when(c == pl.num_programs(1) - 1)
    def _():
        hout_ref[0] = h_scr[...]


def _ssd_prompt(proj, dtraw, convw, convb, dtb, alog, de, normw, e1, e2, *, batch, seq,
                hw, gn, n_groups, n_state, head_dim, col):
    L = SSD_CHUNK
    nc = seq // L
    n_heads = hw // head_dim
    w_all = hw + 2 * gn
    row = lambda b, c: b * nc + c
    const = lambda b, c: (0, 0)
    kern = functools.partial(_ssd_prompt_kernel, n_groups=n_groups, n_state=n_state, head_dim=head_dim)
    return pl.pallas_call(
        kern,
        grid=(batch, nc),
        in_specs=[
            pl.BlockSpec((L, hw), lambda b, c: (row(b, c), col["z"] // hw)),
            pl.BlockSpec((L, hw), lambda b, c: (row(b, c), col["xs"] // hw)),
            pl.BlockSpec((L, gn), lambda b, c: (row(b, c), col["B"] // gn)),
            pl.BlockSpec((L, gn), lambda b, c: (row(b, c), col["C"] // gn)),
            pl.BlockSpec((L, LANES), lambda b, c: (row(b, c), 0)),
            pl.BlockSpec(convw.shape, const),
            pl.BlockSpec(convb.shape, const),
            pl.BlockSpec(dtb.shape, const),
            pl.BlockSpec(alog.shape, const),
            pl.BlockSpec(de.shape, const),
            pl.BlockSpec(normw.shape, const),
            pl.BlockSpec(e1.shape, const),
            pl.BlockSpec(e2.shape, const),
        ],
        out_specs=[
            pl.BlockSpec((L, hw), lambda b, c: (row(b, c), 0)),
            pl.BlockSpec((1, hw, n_state), lambda b, c: (b, 0, 0)),
        ],
        out_shape=[
            jax.ShapeDtypeStruct((batch * seq, hw), BF16),
            jax.ShapeDtypeStruct((batch, hw, n_state), F32),
        ],
        scratch_shapes=[
            pltpu.VMEM((L + SUBLANES, w_all), F32),
            pltpu.VMEM((L, w_all), F32),
            pltpu.VMEM((hw, n_state), F32),
            pltpu.VMEM((L, hw), BF16),
            pltpu.VMEM((L, hw), BF16),
            pltpu.VMEM((L, hw), F32),
            pltpu.VMEM((L, n_heads * LANES), F32),
            pltpu.VMEM((L, L), F32),
            pltpu.VMEM((L, hw), F32),
        ],
        compiler_params=_cparams("parallel", "arbitrary"),
        name="ssd_prompt",
    )(proj, proj, proj, proj, dtraw, convw, convb, dtb, alog, de, normw, e1, e2)


def _ssd_sample_kernel(z_ref, xs_ref, b_ref, c_ref, dt_ref, cst_ref, h_ref, convw_ref, convb_ref,
                       dtb_ref, alog_ref, de_ref, normw_ref, e1_ref,
                       ya_ref, hout_ref, xbc, y_scr,
                       *, n_groups, n_state):
    bb = xs_ref.shape[0]
    hw = xs_ref.shape[1]
    gn = b_ref.shape[1]
    taps = convw_ref.shape[0]
    gw = hw // n_groups

    xbc[:, 0:hw] = xs_ref[...]
    xbc[:, hw:hw + gn] = b_ref[...]
    xbc[:, hw + gn:] = c_ref[...]
    acc = convb_ref[...] + convw_ref[taps - 1:taps, :] * xbc[...]
    for j in range(taps - 1):
        acc = acc + convw_ref[j:j + 1, :] * cst_ref[j]
    act = _silu(acc)
    xs = act[:, 0:hw]

    dt = _softplus(dt_ref[...] + dtb_ref[...])
    dec = jnp.exp(dt * (-jnp.exp(alog_ref[...])))
    e1 = e1_ref[...]
    dte = _dot3_lhs(dt, e1)
    dece = _dot3_lhs(dec, e1)
    xdt = xs * dte

    pieces = [p.astype(F32) for p in _split3(dece)] + [p.astype(F32) for p in _split3(xdt)]
    npc = len(pieces)
    stack = jnp.concatenate(pieces + [jnp.zeros((LANES - npc * bb, hw), F32)], axis=0)
    lt = stack.T.astype(BF16)

    krow = lax.broadcasted_iota(jnp.int32, (LANES, LANES), 0)
    rowid = lax.broadcasted_iota(jnp.int32, (bb, gw), 0)
    half = npc // 2
    y_scr[...] = jnp.zeros_like(y_scr)
    for s in range(bb):
        is_s = (krow % bb) == s
        sel_dec = jnp.where(is_s & (krow < half * bb), 1.0, 0.0).astype(BF16)
        sel_x = jnp.where(is_s & (krow >= half * bb) & (krow < npc * bb), 1.0, 0.0).astype(BF16)
        dec_b = _dot(lt, sel_dec)
        x_b = _dot(lt, sel_x)
        for g in range(n_groups):
            rows = slice(g * gw, (g + 1) * gw)
            brow = act[s:s + 1, hw + g * n_state:hw + (g + 1) * n_state]
            hn = h_ref[s, rows, :] * dec_b[rows, :] + x_b[rows, :] * brow
            hout_ref[s, rows, :] = hn
            cg = act[:, hw + gn + g * n_state:hw + gn + (g + 1) * n_state].astype(BF16)
            yg = lax.dot_general(cg, hn.astype(BF16), NT_DIMS, preferred_element_type=F32)
            y_scr[:, rows] = y_scr[:, rows] + jnp.where(rowid == s, yg, 0.0)

    y = y_scr[...] + xs * de_ref[...]
    _ssd_gate_norm(y, z_ref[...], normw_ref, ya_ref, n_groups)


def _ssd_sample(proj, dtraw, cst, h0, convw, convb, dtb, alog, de, normw, e1, *, row0, nb,
                hw, gn, n_groups, n_state, col):
    bb = SUBLANES
    w_all = hw + 2 * gn
    r0 = row0 // bb
    const = lambda i: (0, 0)
    kern = functools.partial(_ssd_sample_kernel, n_groups=n_groups, n_state=n_state)
    return pl.pallas_call(
        kern,
        grid=(nb // bb,),
        in_specs=[
            pl.BlockSpec((bb, hw), lambda i: (r0 + i, col["z"] // hw)),
            pl.BlockSpec((bb, hw), lambda i: (r0 + i, col["xs"] // hw)),
            pl.BlockSpec((bb, gn), lambda i: (r0 + i, col["B"] // gn)),
            pl.BlockSpec((bb, gn), lambda i: (r0 + i, col["C"] // gn)),
            pl.BlockSpec((bb, LANES), lambda i: (r0 + i, 0)),
            pl.BlockSpec((cst.shape[0], bb, w_all), lambda i: (0, i, 0)),
            pl.BlockSpec((bb, hw, n_state), lambda i: (i, 0, 0)),
            pl.BlockSpec(convw.shape, const),
            pl.BlockSpec(convb.shape, const),
            pl.BlockSpec(dtb.shape, const),
            pl.BlockSpec(alog.shape, const),
            pl.BlockSpec(de.shape, const),
            pl.BlockSpec(normw.shape, const),
            pl.BlockSpec(e1.shape, const),
        ],
        out_specs=[
            pl.BlockSpec((bb, hw), lambda i: (i, 0)),
            pl.BlockSpec((bb, hw, n_state), lambda i: (i, 0, 0)),
        ],
        out_shape=[
            jax.ShapeDtypeStruct((nb, hw), BF16),
            jax.ShapeDtypeStruct((nb, hw, n_state), F32),
        ],
        scratch_shapes=[pltpu.VMEM((bb, w_all), F32), pltpu.VMEM((bb, hw), F32)],
        compiler_params=_cparams("parallel"),
        name="ssd_sample",
    )(proj, proj, proj, proj, dtraw, cst, h0, convw, convb, dtb, alog, de, normw, e1)


def _sc_prompt_kernel(b_ref, c_ref, h_ref, w_ref, yb_ref, tail_ref, cbuf):
    j = pl.program_id(1)
    ts = b_ref.shape[0]
    taps = w_ref.shape[0]

    @pl.when(j == 0)
    def _():
        cbuf[0:SUBLANES, :] = jnp.zeros((SUBLANES, cbuf.shape[1]), F32)

    cbuf[SUBLANES:SUBLANES + ts, :] = c_ref[...] * h_ref[...]
    u = w_ref[taps - 1:taps, :] * cbuf[SUBLANES:SUBLANES + ts, :]
    for k in range(1, taps):
        u = u + w_ref[taps - 1 - k:taps - k, :] * cbuf[SUBLANES - k:SUBLANES - k + ts, :]
    yb_ref[...] = (b_ref[...] * u).astype(yb_ref.dtype)
    cbuf[0:SUBLANES, :] = cbuf[ts:ts + SUBLANES, :]

    @pl.when(j == pl.num_programs(1) - 1)
    def _():
        tail_ref[0] = cbuf[0:SUBLANES, :]


def _sc_prompt(proj, w, *, batch, seq, width, col, ts):
    nt = seq // ts
    row = lambda b, j: b * nt + j
    return pl.pallas_call(
        _sc_prompt_kernel,
        grid=(batch, nt),
        in_specs=[
            pl.BlockSpec((ts, width), lambda b, j: (row(b, j), col["sc_b"] // width)),
            pl.BlockSpec((ts, width), lambda b, j: (row(b, j), col["sc_c"] // width)),
            pl.BlockSpec((ts, width), lambda b, j: (row(b, j), col["sc_h"] // width)),
            pl.BlockSpec(w.shape, lambda b, j: (0, 0)),
        ],
        out_specs=[
            pl.BlockSpec((ts, width), lambda b, j: (row(b, j), 0)),
            pl.BlockSpec((1, SUBLANES, width), lambda b, j: (b, 0, 0)),
        ],
        out_shape=[
            jax.ShapeDtypeStruct((batch * seq, width), BF16),
            jax.ShapeDtypeStruct((batch, SUBLANES, width), F32),
        ],
        scratch_shapes=[pltpu.VMEM((ts + SUBLANES, width), F32)],
        compiler_params=_cparams("parallel", "arbitrary"),
        name="shortconv_prompt",
    )(proj, proj, proj, w)


def _sc_sample_kernel(b_ref, c_ref, h_ref, st_ref, w_ref, yb_ref, ch_ref):
    taps = w_ref.shape[0]
    ch = c_ref[...] * h_ref[...]
    u = w_ref[taps - 1:taps, :] * ch
    for k in range(taps - 1):
        u = u + w_ref[k:k + 1, :] * st_ref[k]
    yb_ref[...] = (b_ref[...] * u).astype(yb_ref.dtype)
    ch_ref[...] = ch


def _sc_sample(proj, st, w, *, row0, nb, width, col):
    r0 = row0 // nb
    return pl.pallas_call(
        _sc_sample_kernel,
        grid=(1,),
        in_specs=[
            pl.BlockSpec((nb, width), lambda i: (r0, col["sc_b"] // width)),
            pl.BlockSpec((nb, width), lambda i: (r0, col["sc_c"] // width)),
            pl.BlockSpec((nb, width), lambda i: (r0, col["sc_h"] // width)),
            pl.BlockSpec(st.shape, lambda i: (0, 0, 0)),
            pl.BlockSpec(w.shape, lambda i: (0, 0)),
        ],
        out_specs=[pl.BlockSpec((nb, width), lambda i: (0, 0)),
                   pl.BlockSpec((nb, width), lambda i: (0, 0))],
        out_shape=[jax.ShapeDtypeStruct((nb, width), BF16),
                   jax.ShapeDtypeStruct((nb, width), F32)],
        compiler_params=_cparams("arbitrary"),
        name="shortconv_sample",
    )(proj, proj, proj, st, w)


def _branch_kernel(ya_ref, yb_ref, wa_ref, wb_ref, ga_ref, gb_ref, o_ref):
    ta = _dot(ya_ref[...], wa_ref[...])
    tb = _dot(yb_ref[...], wb_ref[...])
    o_ref[...] = (_sigmoid(ga_ref[...]) * ta + _sigmoid(gb_ref[...]) * tb).astype(o_ref.dtype)


def _branch(ya, yb, wa, wb, proj, *, col, tm, tn):
    m, k = ya.shape
    n = wa.shape[1]
    return pl.pallas_call(
        _branch_kernel,
        grid=(m // tm, n // tn),
        in_specs=[
            pl.BlockSpec((tm, k), lambda i, j: (i, 0)),
            pl.BlockSpec((tm, yb.shape[1]), lambda i, j: (i, 0)),
            pl.BlockSpec((k, tn), lambda i, j: (0, j)),
            pl.BlockSpec((yb.shape[1], tn), lambda i, j: (0, j)),
            pl.BlockSpec((tm, tn), lambda i, j: (i, col["g_a"] // tn + j)),
            pl.BlockSpec((tm, tn), lambda i, j: (i, col["g_b"] // tn + j)),
        ],
        out_specs=pl.BlockSpec((tm, tn), lambda i, j: (i, j)),
        out_shape=jax.ShapeDtypeStruct((m, n), BF16),
        compiler_params=_cparams("parallel", "parallel"),
        name="branch_mix",
    )(ya, yb, wa, wb, proj, proj)


def _x1_kernel(x_ref, mix_ref, w_ref, g_ref, b_ref, o_ref, ob_ref, *, alpha):
    t = alpha * x_ref[...] + _dot(mix_ref[...], w_ref[...])
    x1 = _layer_norm(t, g_ref[...], b_ref[...])
    o_ref[...] = x1
    ob_ref[...] = pltpu.bitcast(x1.astype(BF16), U32)


def _x1(x, mix, w, g, b, *, alpha, tm):
    m, d = x.shape
    return pl.pallas_call(
        functools.partial(_x1_kernel, alpha=alpha),
        grid=(m // tm,),
        in_specs=[
            pl.BlockSpec((tm, d), lambda i: (i, 0)),
            pl.BlockSpec((tm, d), lambda i: (i, 0)),
            pl.BlockSpec(w.shape, lambda i: (0, 0)),
            pl.BlockSpec(g.shape, lambda i: (0, 0)),
            pl.BlockSpec(b.shape, lambda i: (0, 0)),
        ],
        out_specs=[pl.BlockSpec((tm, d), lambda i: (i, 0)), pl.BlockSpec((_words(tm), d), lambda i: (i, 0))],
        out_shape=[jax.ShapeDtypeStruct((m, d), F32), jax.ShapeDtypeStruct((_words(m), d), U32)],
        compiler_params=_cparams("parallel"),
        name="x1_out_ln",
    )(x, mix, w, g, b)


def _top_rows(s, k, with_rank=False):
    rows = []
    cur = s
    rank = jnp.full(s.shape, float(k), F32)
    for a in range(k):
        m = jnp.max(cur, axis=0, keepdims=True)
        rows.append(m)
        hit = cur == m
        if with_rank:
            rank = jnp.where(hit, float(a), rank)
        cur = jnp.where(hit, -jnp.inf, cur)
    return (rows, rank) if with_rank else rows


def _route_kernel(x_ref, wq_ref, k1_ref, k2_ref, c1_ref, w1_ref, r2_ref, e2_ref, q_scr):
    n_heads = k1_ref.shape[0]
    dk = k1_ref.shape[2]
    K = PEER_TOPK
    q_scr[...] = _dot(pltpu.bitcast(x_ref[...], BF16), wq_ref[...]).astype(BF16)
    for h in range(n_heads):
        q1 = q_scr[:, (2 * h) * dk:(2 * h + 1) * dk]
        q2 = q_scr[:, (2 * h + 1) * dk:(2 * h + 2) * dk]
        s1 = lax.dot_general(k1_ref[h], q1, NT_DIMS, preferred_element_type=F32)
        s2 = lax.dot_general(k2_ref[h], q2, NT_DIMS, preferred_element_type=F32)
        t1 = jnp.concatenate(_top_rows(s1, K), axis=0)
        rows2, rank2 = _top_rows(s2, K, with_rank=True)
        t2 = jnp.concatenate(rows2, axis=0)
        assert K == 2 * SUBLANES
        cand = jnp.concatenate(
            [t1[0:1] + t2] + [t1[a:a + 1] + t2[0:SUBLANES] for a in range(1, SUBLANES)]
            + [t1[SUBLANES:] + t2[0:1]], axis=0)
        thr = _top_rows(cand, K)[-1]
        m1 = t1[0:1]
        m2 = t2[0:1]
        zsum = jnp.sum(jnp.where(cand >= thr, jnp.exp(cand - (m1 + m2)), 0.0), axis=0, keepdims=True)
        cnt = jnp.zeros(s1.shape, F32)
        for b in range(K):
            cnt = cnt + jnp.where(s1 + t2[b:b + 1] >= thr, 1.0, 0.0)
        c1_ref[h] = cnt
        w1_ref[h] = jnp.exp(s1 - m1) / zsum
        r2_ref[h] = pltpu.bitcast(rank2.astype(BF16), U32)
        e2_ref[h] = pltpu.bitcast(jnp.exp(s2 - m2).astype(BF16), U32)


def _route(x1p, wq, k1, k2, *, m, tm):
    d = x1p.shape[1]
    n_heads, n_keys, _ = k1.shape
    tok = lambda rows: pl.BlockSpec((n_heads, rows, tm), lambda i: (0, 0, i))
    shp = lambda rows, dt: jax.ShapeDtypeStruct((n_heads, rows, m), dt)
    return pl.pallas_call(
        _route_kernel,
        grid=(m // tm,),
        in_specs=[
            pl.BlockSpec((_words(tm), d), lambda i: (i, 0)),
            pl.BlockSpec(wq.shape, lambda i: (0, 0)),
            pl.BlockSpec(k1.shape, lambda i: (0, 0, 0)),
            pl.BlockSpec(k2.shape, lambda i: (0, 0, 0)),
        ],
        out_specs=[tok(n_keys), tok(n_keys), tok(_words(n_keys)), tok(_words(n_keys))],
        out_shape=[shp(n_keys, F32), shp(n_keys, F32), shp(_words(n_keys), U32), shp(_words(n_keys), U32)],
        scratch_shapes=[pltpu.VMEM((tm, wq.shape[1]), BF16)],
        compiler_params=_cparams("parallel"),
        name="peer_route",
    )(x1p, wq, k1, k2)


def _expert_kernel(x_ref, u_ref, vt_ref, c1_ref, w1_ref, r2_ref, e2_ref, o_ref, acc, coef, hid):
    e = pl.program_id(1)
    n_heads, per, tm = c1_ref.shape
    n_keys = hid.shape[0] // per
    half = _words(n_keys)

    @pl.when(e == 0)
    def _():
        acc[...] = jnp.zeros_like(acc)

    hid[...] = lax.dot_general(pltpu.bitcast(u_ref[...], BF16), pltpu.bitcast(x_ref[...], BF16), NT_DIMS,
                               preferred_element_type=F32)
    for k in range(per):
        for c in range(tm // LANES):
            cs = slice(c * LANES, (c + 1) * LANES)
            gate = jnp.zeros((n_keys, LANES), BF16)
            for h in range(n_heads):
                cnt = c1_ref[h, k:k + 1, cs].astype(BF16)
                w1 = w1_ref[h, k:k + 1, cs].astype(BF16)
                r2 = pltpu.bitcast(r2_ref[h, :, cs], BF16)
                e2 = pltpu.bitcast(e2_ref[h, :, cs], BF16)
                gate = gate + jnp.where(r2 < cnt, e2 * w1, jnp.zeros_like(gate))
            hk = hid[k * n_keys:(k + 1) * n_keys, cs]
            gelu = 0.5 * hk * (1.0 + lax.erf(hk * (1.0 / math.sqrt(2.0))))
            coef[k * half:(k + 1) * half, cs] = pltpu.bitcast(gate * gelu.astype(BF16), U32)

    acc[...] += _dot(pltpu.bitcast(vt_ref[...], BF16), pltpu.bitcast(coef[...], BF16))

    @pl.when(e == pl.num_programs(1) - 1)
    def _():
        o_ref[...] = acc[...].T.astype(o_ref.dtype)


def _experts(x1p, u, vt, c1, w1, r2, e2, *, m, tm, te):
    d = x1p.shape[1]
    n_tiles = u.shape[0] // _words(te)
    n_heads, n_keys, _ = c1.shape
    tok = lambda rows: pl.BlockSpec((n_heads, rows, tm), lambda i, e: (0, 0, i))
    per = te // n_keys
    assert per % SUBLANES == 0
    row = pl.BlockSpec((n_heads, per, tm), lambda i, e: (0, e, i))
    return pl.pallas_call(
        _expert_kernel,
        grid=(m // tm, n_tiles),
        in_specs=[
            pl.BlockSpec((_words(tm), d), lambda i, e: (i, 0)),
            pl.BlockSpec((_words(te), d), lambda i, e: (e, 0)),
            pl.BlockSpec((_words(d), te), lambda i, e: (0, e)),
            row, row, tok(_words(n_keys)), tok(_words(n_keys)),
        ],
        out_specs=pl.BlockSpec((tm, d), lambda i, e: (i, 0)),
        out_shape=jax.ShapeDtypeStruct((m, d), BF16),
        scratch_shapes=[pltpu.VMEM((d, tm), F32),
                        pltpu.VMEM((_words(te), tm), U32),
                        pltpu.VMEM((te, tm), F32)],
        compiler_params=_cparams("parallel", "arbitrary"),
        name="peer_experts",
    )(x1p, u, vt, c1, w1, r2, e2)


def _final_kernel(x1_ref, peer_ref, p_ref, wg_ref, wp_ref, g_ref, b_ref, o_ref, *, alpha):
    x2 = _layer_norm(alpha * x1_ref[...] + peer_ref[...], g_ref[...], b_ref[...])
    gate = _sigmoid(_dot(x2.astype(BF16), wg_ref[...]))
    o_ref[...] = x2 + gate * _dot(p_ref[...].astype(BF16), wp_ref[...])


def _final(x1, peer, p, wg, wp, g, b, *, alpha, tm):
    m, d = x1.shape
    return pl.pallas_call(
        functools.partial(_final_kernel, alpha=alpha),
        grid=(m // tm,),
        in_specs=[
            pl.BlockSpec((tm, d), lambda i: (i, 0)),
            pl.BlockSpec((tm, d), lambda i: (i, 0)),
            pl.BlockSpec((tm, p.shape[1]), lambda i: (i, 0)),
            pl.BlockSpec(wg.shape, lambda i: (0, 0)),
            pl.BlockSpec(wp.shape, lambda i: (0, 0)),
            pl.BlockSpec(g.shape, lambda i: (0, 0)),
            pl.BlockSpec(b.shape, lambda i: (0, 0)),
        ],
        out_specs=pl.BlockSpec((tm, d), lambda i: (i, 0)),
        out_shape=jax.ShapeDtypeStruct((m, d), F32),
        compiler_params=_cparams("parallel"),
        name="ln2_ple",
    )(x1, peer, p, wg, wp, g, b)


def _tile(m, cap, mult):
    best = None
    for t in range(mult, min(m, cap) + 1, mult):
        if m % t == 0:
            best = t
    assert best is not None, (m, cap, mult)
    return best


TOKEN_TILE = 640
ROW_TILE = 320
EXPERT_TILE = 1024
PEER_TOKEN_TILE = 768


def _pack_rows(w):
    b = lax.bitcast_convert_type(w.astype(BF16), jnp.uint16).astype(U32)
    return b[0::2] | (b[1::2] << 16)


def _pad_lanes(v):
    return jnp.pad(v.astype(F32), (0, LANES - v.shape[0])).reshape(1, LANES)


def _layer(x, p, ssm_h, conv_buf, sc_buf, n_prompt, batch, seq, depth,
           w_in, ssd_conv_w, ssd_conv_b, ssd_dt_bias, ssd_a_log, ssd_d, ssd_norm_w,
           sc_conv_w, w_branch_ssd, w_branch_sc, w_out, ln1_g, ln1_b,
           peer_wq, peer_keys1, peer_keys2, peer_u, peer_v, ln2_g, ln2_b,
           ple_gate_w, ple_proj_w):
    m, d = x.shape
    nb = m - n_prompt
    n_heads = ssd_dt_bias.shape[0]
    hw, n_state = ssm_h.shape[1] * ssm_h.shape[2], ssm_h.shape[3]
    head_dim = ssm_h.shape[2]
    conv_dim = ssd_conv_w.shape[1]
    gn = (conv_dim - hw) // 2
    n_groups = gn // n_state
    scw = sc_conv_w.shape[1]
    alpha = (2.0 * depth) ** 0.25
    assert hw == scw == d and n_heads <= LANES and LANES % head_dim == 0 and n_state == LANES

    o_z, o_xbc, o_dt = 0, hw, hw + conv_dim
    o_scb = o_dt + n_heads
    o_scc, o_sch, o_ga, o_gb = o_scb + scw, o_scb + 2 * scw, o_scb + 3 * scw, o_scb + 3 * scw + d
    seg = lambda o, w: w_in[:, o:o + w]
    w_main = jnp.concatenate(
        [seg(o_z, hw), seg(o_xbc, hw), seg(o_scb, scw), seg(o_scc, scw), seg(o_sch, scw),
         seg(o_ga, d), seg(o_gb, d), seg(o_xbc + hw, gn), seg(o_xbc + hw + gn, gn)], axis=1).astype(BF16)
    col = {"z": 0, "xs": hw, "sc_b": 2 * hw, "sc_c": 3 * hw, "sc_h": 4 * hw, "g_a": 5 * hw,
           "g_b": 6 * hw, "B": 7 * hw, "C": 7 * hw + gn}
    w_dt = jnp.pad(seg(o_dt, n_heads), ((0, 0), (0, LANES - n_heads))).astype(BF16)

    tm = _tile(m, TOKEN_TILE, LANES)
    proj = _matmul(x, w_main, tm, _tile(w_main.shape[1], 1024, LANES), F32)
    dtraw = _matmul(x, w_dt, tm, LANES, F32)

    ch_head = jnp.arange(hw) // head_dim
    e1 = (jnp.arange(LANES)[:, None] == ch_head[None, :]).astype(BF16)
    e2 = (jnp.arange(LANES)[:, None] == (jnp.arange(n_heads * LANES) // LANES)[None, :]).astype(BF16)
    convb = ssd_conv_b.reshape(1, conv_dim)
    dtb, alog = _pad_lanes(ssd_dt_bias), _pad_lanes(ssd_a_log)
    de = jnp.repeat(ssd_d.astype(F32), head_dim).reshape(1, hw)
    normw = ssd_norm_w.reshape(1, hw)
    shp = dict(hw=hw, gn=gn, n_groups=n_groups, n_state=n_state, col=col)
    ya_p, h_p = _ssd_prompt(proj, dtraw, ssd_conv_w, convb, dtb, alog, de, normw, e1, e2,
                            batch=batch, seq=seq, head_dim=head_dim, **shp)
    cst = jnp.transpose(conv_buf, (1, 0, 2))
    ya_s, h_s = _ssd_sample(proj, dtraw, cst, ssm_h.reshape(nb, hw, n_state), ssd_conv_w, convb,
                            dtb, alog, de, normw, e1, row0=n_prompt, nb=nb, **shp)

    yb_p, sc_tail = _sc_prompt(proj, sc_conv_w, batch=batch, seq=seq, width=scw, col=col,
                               ts=min(seq, 256))
    yb_s, ch_s = _sc_sample(proj, jnp.transpose(sc_buf, (1, 0, 2)), sc_conv_w,
                            row0=n_prompt, nb=nb, width=scw, col=col)

    ya = jnp.concatenate([ya_p, ya_s], axis=0)
    yb = jnp.concatenate([yb_p, yb_s], axis=0)
    mix = _branch(ya, yb, w_branch_ssd.astype(BF16), w_branch_sc.astype(BF16), proj,
                  col=col, tm=tm, tn=_tile(d, 512, LANES))
    tm2 = _tile(m, ROW_TILE, SUBLANES)
    x1, x1b = _x1(x, mix, w_out.astype(BF16), ln1_g.reshape(1, d), ln1_b.reshape(1, d),
                  alpha=alpha, tm=tm2)

    tmp = PEER_TOKEN_TILE
    mp = m + (-m % tmp)
    x1p = jnp.pad(x1b, ((0, _words(mp - m)), (0, 0)))
    route = _route(x1p, peer_wq.astype(BF16), peer_keys1.astype(BF16), peer_keys2.astype(BF16), m=mp, tm=tmp)
    peer = _experts(x1p, _pack_rows(peer_u), _pack_rows(peer_v.T), *route, m=mp, tm=tmp, te=EXPERT_TILE)

    y = _final(x1, peer, p, ple_gate_w.astype(BF16), ple_proj_w.astype(BF16),
               ln2_g.reshape(1, d), ln2_b.reshape(1, d), alpha=alpha, tm=tm2)

    k_ssd = ssd_conv_w.shape[0] - 1
    k_sc = sc_conv_w.shape[0] - 1
    xbc_cols = lambda rows: jnp.concatenate(
        [rows[..., col["xs"]:col["xs"] + hw], rows[..., col["B"]:col["B"] + gn],
         rows[..., col["C"]:col["C"] + gn]], axis=-1)
    conv_p = xbc_cols(jnp.stack([proj[(b + 1) * seq - k_ssd:(b + 1) * seq] for b in range(batch)]))
    conv_s = jnp.concatenate([conv_buf[:, 1:, :], xbc_cols(proj[n_prompt:])[:, None, :]], axis=1)
    sc_p = sc_tail[:, SUBLANES - k_sc:, :]
    sc_s = jnp.concatenate([sc_buf[:, 1:, :], ch_s[:, None, :]], axis=1)
    hshape = (-1, n_heads, head_dim, n_state)
    return y, conv_p, h_p.reshape(hshape), sc_p, conv_s, h_s.reshape(hshape), sc_s


def kernel(x_prompt, x_sample, p_prompt, p_sample, state_ssm, state_ssd_conv, state_shortconv, w_in, ssd_conv_w, ssd_conv_b, ssd_dt_bias, ssd_a_log, ssd_d, ssd_norm_w, sc_conv_w, w_branch_ssd, w_branch_sc, w_out, ln1_g, ln1_b, peer_wq, peer_keys1, peer_keys2, peer_u, peer_v, ln2_g, ln2_b, ple_gate_w, ple_proj_w):
    batch, seq, d = x_prompt.shape
    nb, dec_seq, _ = x_sample.shape
    assert dec_seq == 1 and seq % SSD_CHUNK == 0
    depth = w_in.shape[0]
    n_prompt = batch * seq
    x = jnp.concatenate([x_prompt.reshape(n_prompt, d), x_sample.reshape(nb, d)], axis=0)
    weights = (w_in, ssd_conv_w, ssd_conv_b, ssd_dt_bias, ssd_a_log, ssd_d, ssd_norm_w,
               sc_conv_w, w_branch_ssd, w_branch_sc, w_out, ln1_g, ln1_b,
               peer_wq, peer_keys1, peer_keys2, peer_u, peer_v, ln2_g, ln2_b,
               ple_gate_w, ple_proj_w)
    outs = [[] for _ in range(6)]
    for i in range(depth):
        p = jnp.concatenate([p_prompt[i].reshape(n_prompt, -1), p_sample[i].reshape(nb, -1)], axis=0)
        x, conv_p, h_p, sc_p, conv_s, h_s, sc_s = _layer(
            x, p, state_ssm[i], state_ssd_conv[i], state_shortconv[i], n_prompt, batch, seq, depth,
            *[w[i] for w in weights])
        for lst, val in zip(outs, (h_p, conv_p, sc_p, h_s, conv_s, sc_s)):
            lst.append(val)
    y_prompt = x[:n_prompt].reshape(batch, seq, d)
    y_sample = x[n_prompt:].reshape(nb, 1, d)
    return (y_prompt, y_sample) + tuple(jnp.stack(lst) for lst in outs)
```

```python
import functools
import math

import jax
import jax.numpy as jnp
from jax import lax
from jax.experimental import pallas as pl
from jax.experimental.pallas import tpu as pltpu

F32 = jnp.float32
BF16 = jnp.bfloat16
U32 = jnp.uint32

LANES = 128
SUBLANES = 8
BF16_ROWS = 16
PEER_TOPK = 16
SSD_CHUNK = 128
LN_EPS = 1e-5
RMS_EPS = 1e-5
VMEM_LIMIT = 56 * 1024 * 1024

NT_DIMS = (((1,), (1,)), ((), ()))
TN_DIMS = (((0,), (0,)), ((), ()))


def _cparams(*sem, flags=None):
    return pltpu.CompilerParams(dimension_semantics=sem, vmem_limit_bytes=VMEM_LIMIT, flags=flags)


def _dot(a, b):
    return jnp.dot(a, b, preferred_element_type=F32)


def _split3(v):
    hi = v.astype(BF16)
    r = v - hi.astype(F32)
    mid = r.astype(BF16)
    lo = (r - mid.astype(F32)).astype(BF16)
    return hi, mid, lo


def _dot3_lhs(v, rhs_bf16):
    hi, mid, lo = _split3(v)
    return _dot(hi, rhs_bf16) + _dot(mid, rhs_bf16) + _dot(lo, rhs_bf16)


def _dot3_rhs(lhs_bf16, v):
    hi, mid, lo = _split3(v)
    return _dot(lhs_bf16, hi) + _dot(lhs_bf16, mid) + _dot(lhs_bf16, lo)


def _words(rows):
    return rows * jnp.dtype(BF16).itemsize // jnp.dtype(U32).itemsize


def _sigmoid(x):
    return 1.0 / (1.0 + jnp.exp(-x))


def _silu(x):
    return x * _sigmoid(x)


def _softplus(x):
    return jnp.maximum(x, 0.0) + jnp.log1p(jnp.exp(-jnp.abs(x)))


def _layer_norm(x, g, b):
    mu = jnp.mean(x, axis=-1, keepdims=True)
    xc = x - mu
    var = jnp.mean(xc * xc, axis=-1, keepdims=True)
    return xc * lax.rsqrt(var + LN_EPS) * g + b


def _tile(m, cap, mult):
    best = None
    for t in range(mult, min(m, cap) + 1, mult):
        if m % t == 0:
            best = t
    assert best is not None, (m, cap, mult)
    return best


def _in_proj_kernel(x_ref, wa_ref, wb_ref, o_ref, w_scr, *, shift):
    tn = w_scr.shape[1]

    @pl.when(pl.program_id(1) == 0)
    def _():
        if shift:
            w = jnp.concatenate([wa_ref[...], wb_ref[...]], axis=1)[:, shift:shift + tn]
        else:
            w = wa_ref[...]
        w_scr[...] = w.astype(BF16)

    o_ref[...] = _dot(x_ref[...].astype(BF16), w_scr[...])


def _in_proj(x, w, *, col0, n, tm):
    m, k = x.shape
    shift = col0 % LANES
    base = col0 - shift
    tn = _tile(math.gcd(n, base) if base else n, 1024, LANES)
    last = pl.cdiv(w.shape[1], LANES) - 1
    return pl.pallas_call(
        functools.partial(_in_proj_kernel, shift=shift),
        grid=(n // tn, m // tm),
        in_specs=[pl.BlockSpec((tm, k), lambda j, i: (i, 0)),
                  pl.BlockSpec((k, tn), lambda j, i: (0, base // tn + j)),
                  pl.BlockSpec((k, LANES), lambda j, i: (0, jnp.minimum((base + (j + 1) * tn) // LANES, last)))],
        out_specs=pl.BlockSpec((tm, tn), lambda j, i: (i, j)),
        out_shape=jax.ShapeDtypeStruct((m, n), F32),
        scratch_shapes=[pltpu.VMEM((k, tn), BF16)],
        compiler_params=_cparams("parallel", "arbitrary"),
        name="in_proj",
    )(x, w, w)


def _ssd_gate_norm(y, z, normw_ref, out_ref, n_groups):
    y = y * _silu(z)
    gw = y.shape[1] // n_groups
    for g in range(n_groups):
        sl = slice(g * gw, (g + 1) * gw)
        yg = y[:, sl]
        ms = jnp.mean(yg * yg, axis=-1, keepdims=True)
        out_ref[:, sl] = (yg * lax.rsqrt(ms + RMS_EPS) * normw_ref[:, sl]).astype(out_ref.dtype)


def _ssd_prompt_kernel(z_ref, xs_ref, b_ref, c_ref, dt_ref, convw_ref, convb_ref, dtb_ref,
                       alog_ref, de_ref, normw_ref, e1_ref, e2_ref,
                       ya_ref, hout_ref,
                       cbuf, act, h_scr, x_scr, xd_scr, eae_scr, acsb_scr, acst_scr, y_scr,
                       *, n_groups, n_state, head_dim):
    c = pl.program_id(1)
    L = SSD_CHUNK
    hw = xs_ref.shape[1]
    gn = b_ref.shape[1]
    w_all = hw + 2 * gn
    n_heads = hw // head_dim
    hpl = LANES // head_dim
    n_blk = n_heads // hpl
    blk_per_group = n_blk // n_groups
    taps = convw_ref.shape[0]

    @pl.when(c == 0)
    def _():
        h_scr[...] = jnp.zeros_like(h_scr)
        cbuf[0:SUBLANES, :] = jnp.zeros((SUBLANES, w_all), F32)

    cbuf[SUBLANES:SUBLANES + L, 0:hw] = xs_ref[...]
    cbuf[SUBLANES:SUBLANES + L, hw:hw + gn] = b_ref[...]
    cbuf[SUBLANES:SUBLANES + L, hw + gn:] = c_ref[...]

    cw = math.gcd(w_all, 512)
    for blk in range(w_all // cw):
        sl = slice(blk * cw, (blk + 1) * cw)
        acc = convb_ref[:, sl] + convw_ref[taps - 1:taps, sl] * cbuf[SUBLANES:SUBLANES + L, sl]
        for j in range(1, taps):
            acc = acc + convw_ref[taps - 1 - j:taps - j, sl] * cbuf[SUBLANES - j:SUBLANES - j + L, sl]
        act[:, sl] = _silu(acc)
    cbuf[0:SUBLANES, :] = cbuf[L:L + SUBLANES, :]

    dt = _softplus(dt_ref[...] + dtb_ref[...])
    a_neg = -jnp.exp(alog_ref[...])
    dta = dt * a_neg
    ri = lax.broadcasted_iota(jnp.int32, (L, L), 0)
    ci = lax.broadcasted_iota(jnp.int32, (L, L), 1)
    causal = ri >= ci
    tri = jnp.where(causal, 1.0, 0.0).astype(BF16)
    acs = _dot3_rhs(tri, dta)
    e1 = e1_ref[...]
    dte = _dot3_lhs(dt, e1)
    acs_p = _split3(acs)
    acse = _dot(acs_p[0], e1) + _dot(acs_p[1], e1) + _dot(acs_p[2], e1)
    e2 = e2_ref[...]
    acsb_scr[...] = _dot(acs_p[0], e2) + _dot(acs_p[1], e2) + _dot(acs_p[2], e2)
    acst_scr[...] = acs.T

    xdt = act[:, 0:hw] * dte
    x_scr[...] = xdt.astype(BF16)
    xd_scr[...] = (xdt * jnp.exp(acse[L - 1:L, :] - acse)).astype(BF16)
    eae_scr[...] = jnp.exp(acse)

    lane = lax.broadcasted_iota(jnp.int32, (L, LANES), 1)
    cb = None
    for j in range(n_blk):
        g = j // blk_per_group
        bsl = slice(hw + g * n_state, hw + (g + 1) * n_state)
        csl = slice(hw + gn + g * n_state, hw + gn + (g + 1) * n_state)
        bg = act[:, bsl].astype(BF16)
        cg = act[:, csl].astype(BF16)
        if j % blk_per_group == 0:
            cb = lax.dot_general(cg, bg, NT_DIMS, preferred_element_type=F32)
        psl = slice(j * LANES, (j + 1) * LANES)
        xp = x_scr[:, psl]
        ydiag = None
        cds = []
        for q in range(hpl):
            r = j * hpl + q
            ab = acsb_scr[:, r * LANES:(r + 1) * LANES]
            at = jnp.broadcast_to(acst_scr[r:r + 1, :], (L, L))
            lm = jnp.where(causal, jnp.exp(ab - at), 0.0)
            m = (cb * lm).astype(BF16)
            inhead = (lane >= q * head_dim) & (lane < (q + 1) * head_dim)
            xq = jnp.where(inhead, xp, jnp.zeros_like(xp))
            yq = _dot(m, xq)
            ydiag = yq if ydiag is None else ydiag + yq
            cds.append(jnp.broadcast_to(jnp.exp(acsb_scr[L - 1:L, r * LANES:(r + 1) * LANES]),
                                        (head_dim, LANES)))
        cd = jnp.concatenate(cds, axis=0)
        hp = h_scr[psl, :]
        yoff = lax.dot_general(cg, hp.astype(BF16), NT_DIMS, preferred_element_type=F32)
        yoff = yoff * eae_scr[:, psl]
        st = lax.dot_general(xd_scr[:, psl], bg, TN_DIMS, preferred_element_type=F32)
        h_scr[psl, :] = hp * cd + st
        y_scr[:, psl] = ydiag + yoff + act[:, psl] * de_ref[:, psl]

    _ssd_gate_norm(y_scr[...], z_ref[...], normw_ref, ya_ref, n_groups)

    @pl.when(c == pl.num_programs(1) - 1)
    def _():
        hout_ref[0] = h_scr[...]


def _ssd_prompt(proj, dtraw, convw, convb, dtb, alog, de, normw, e1, e2, *, batch, seq,
                hw, gn, n_groups, n_state, head_dim, col):
    L = SSD_CHUNK
    nc = seq // L
    n_heads = hw // head_dim
    w_all = hw + 2 * gn
    row = lambda b, c: b * nc + c
    const = lambda b, c: (0, 0)
    kern = functools.partial(_ssd_prompt_kernel, n_groups=n_groups, n_state=n_state, head_dim=head_dim)
    return pl.pallas_call(
        kern,
        grid=(batch, nc),
        in_specs=[
            pl.BlockSpec((L, hw), lambda b, c: (row(b, c), col["z"] // hw)),
            pl.BlockSpec((L, hw), lambda b, c: (row(b, c), col["xs"] // hw)),
            pl.BlockSpec((L, gn), lambda b, c: (row(b, c), col["B"] // gn)),
            pl.BlockSpec((L, gn), lambda b, c: (row(b, c), col["C"] // gn)),
            pl.BlockSpec((L, LANES), lambda b, c: (row(b, c), 0)),
            pl.BlockSpec(convw.shape, const),
            pl.BlockSpec(convb.shape, const),
            pl.BlockSpec(dtb.shape, const),
            pl.BlockSpec(alog.shape, const),
            pl.BlockSpec(de.shape, const),
            pl.BlockSpec(normw.shape, const),
            pl.BlockSpec(e1.shape, const),
            pl.BlockSpec(e2.shape, const),
        ],
        out_specs=[
            pl.BlockSpec((L, hw), lambda b, c: (row(b, c), 0)),
            pl.BlockSpec((1, hw, n_state), lambda b, c: (b, 0, 0)),
        ],
        out_shape=[
            jax.ShapeDtypeStruct((proj.shape[0], hw), BF16),
            jax.ShapeDtypeStruct((batch, hw, n_state), F32),
        ],
        scratch_shapes=[
            pltpu.VMEM((L + SUBLANES, w_all), F32),
            pltpu.VMEM((L, w_all), F32),
            pltpu.VMEM((hw, n_state), F32),
            pltpu.VMEM((L, hw), BF16),
            pltpu.VMEM((L, hw), BF16),
            pltpu.VMEM((L, hw), F32),
            pltpu.VMEM((L, n_heads * LANES), F32),
            pltpu.VMEM((L, L), F32),
            pltpu.VMEM((L, hw), F32),
        ],
        compiler_params=_cparams("parallel", "arbitrary"),
        name="ssd_prompt",
    )(proj, proj, proj, proj, dtraw, convw, convb, dtb, alog, de, normw, e1, e2)


def _ssd_sample_kernel(z_ref, xs_ref, b_ref, c_ref, dt_ref, cst_ref, h_ref, convw_ref, convb_ref,
                       dtb_ref, alog_ref, de_ref, normw_ref, e1_ref, ya_all_ref,
                       ya_ref, hout_ref, xbc, y_scr,
                       *, n_groups, n_state):
    bb = xs_ref.shape[0]
    hw = xs_ref.shape[1]
    gn = b_ref.shape[1]
    taps = convw_ref.shape[0]
    gw = hw // n_groups

    xbc[:, 0:hw] = xs_ref[...]
    xbc[:, hw:hw + gn] = b_ref[...]
    xbc[:, hw + gn:] = c_ref[...]
    acc = convb_ref[...] + convw_ref[taps - 1:taps, :] * xbc[...]
    for j in range(taps - 1):
        acc = acc + convw_ref[j:j + 1, :] * cst_ref[j]
    act = _silu(acc)
    xs = act[:, 0:hw]

    dt = _softplus(dt_ref[...] + dtb_ref[...])
    dec = jnp.exp(dt * (-jnp.exp(alog_ref[...])))
    e1 = e1_ref[...]
    dte = _dot3_lhs(dt, e1)
    dece = _dot3_lhs(dec, e1)
    xdt = xs * dte

    pieces = [p.astype(F32) for p in _split3(dece)] + [p.astype(F32) for p in _split3(xdt)]
    npc = len(pieces)
    stack = jnp.concatenate(pieces + [jnp.zeros((LANES - npc * bb, hw), F32)], axis=0)
    lt = stack.T.astype(BF16)

    krow = lax.broadcasted_iota(jnp.int32, (LANES, LANES), 0)
    rowid = lax.broadcasted_iota(jnp.int32, (bb, gw), 0)
    half = npc // 2
    y_scr[...] = jnp.zeros_like(y_scr)
    for s in range(bb):
        is_s = (krow % bb) == s
        sel_dec = jnp.where(is_s & (krow < half * bb), 1.0, 0.0).astype(BF16)
        sel_x = jnp.where(is_s & (krow >= half * bb) & (krow < npc * bb), 1.0, 0.0).astype(BF16)
        dec_b = _dot(lt, sel_dec)
        x_b = _dot(lt, sel_x)
        for g in range(n_groups):
            rows = slice(g * gw, (g + 1) * gw)
            brow = act[s:s + 1, hw + g * n_state:hw + (g + 1) * n_state]
            hn = h_ref[s, rows, :] * dec_b[rows, :] + x_b[rows, :] * brow
            hout_ref[s, rows, :] = hn
            cg = act[:, hw + gn + g * n_state:hw + gn + (g + 1) * n_state].astype(BF16)
            yg = lax.dot_general(cg, hn.astype(BF16), NT_DIMS, preferred_element_type=F32)
            y_scr[:, rows] = y_scr[:, rows] + jnp.where(rowid == s, yg, 0.0)

    y = y_scr[...] + xs * de_ref[...]
    _ssd_gate_norm(y, z_ref[...], normw_ref, ya_ref, n_groups)


def _ssd_sample(proj, dtraw, cst, h0, convw, convb, dtb, alog, de, normw, e1, ya_all, *, row0, nb,
                hw, gn, n_groups, n_state, col):
    bb = SUBLANES
    w_all = hw + 2 * gn
    r0 = row0 // bb
    const = lambda i: (0, 0)
    kern = functools.partial(_ssd_sample_kernel, n_groups=n_groups, n_state=n_state)
    return pl.pallas_call(
        kern,
        grid=(nb // bb,),
        in_specs=[
            pl.BlockSpec((bb, hw), lambda i: (r0 + i, col["z"] // hw)),
            pl.BlockSpec((bb, hw), lambda i: (r0 + i, col["xs"] // hw)),
            pl.BlockSpec((bb, gn), lambda i: (r0 + i, col["B"] // gn)),
            pl.BlockSpec((bb, gn), lambda i: (r0 + i, col["C"] // gn)),
            pl.BlockSpec((bb, LANES), lambda i: (r0 + i, 0)),
            pl.BlockSpec((cst.shape[0], bb, w_all), lambda i: (0, i, 0)),
            pl.BlockSpec((bb, hw, n_state), lambda i: (i, 0, 0)),
            pl.BlockSpec(convw.shape, const),
            pl.BlockSpec(convb.shape, const),
            pl.BlockSpec(dtb.shape, const),
            pl.BlockSpec(alog.shape, const),
            pl.BlockSpec(de.shape, const),
            pl.BlockSpec(normw.shape, const),
            pl.BlockSpec(e1.shape, const),
            pl.BlockSpec(memory_space=pl.ANY),
        ],
        out_specs=[
            pl.BlockSpec((bb, hw), lambda i: (r0 + i, 0)),
            pl.BlockSpec((bb, hw, n_state), lambda i: (i, 0, 0)),
        ],
        out_shape=[
            jax.ShapeDtypeStruct(ya_all.shape, BF16),
            jax.ShapeDtypeStruct((nb, hw, n_state), F32),
        ],
        scratch_shapes=[pltpu.VMEM((bb, w_all), F32), pltpu.VMEM((bb, hw), F32)],
        compiler_params=_cparams("parallel"),
        input_output_aliases={14: 0},
        name="ssd_sample",
    )(proj, proj, proj, proj, dtraw, cst, h0, convw, convb, dtb, alog, de, normw, e1, ya_all)


def _sc_prompt_kernel(b_ref, c_ref, h_ref, w_ref, yb_ref, tail_ref, cbuf):
    j = pl.program_id(1)
    ts = b_ref.shape[0]
    taps = w_ref.shape[0]

    @pl.when(j == 0)
    def _():
        cbuf[0:SUBLANES, :] = jnp.zeros((SUBLANES, cbuf.shape[1]), F32)

    cbuf[SUBLANES:SUBLANES + ts, :] = c_ref[...] * h_ref[...]
    u = w_ref[taps - 1:taps, :] * cbuf[SUBLANES:SUBLANES + ts, :]
    for k in range(1, taps):
        u = u + w_ref[taps - 1 - k:taps - k, :] * cbuf[SUBLANES - k:SUBLANES - k + ts, :]
    yb_ref[...] = (b_ref[...] * u).astype(yb_ref.dtype)
    cbuf[0:SUBLANES, :] = cbuf[ts:ts + SUBLANES, :]

    @pl.when(j == pl.num_programs(1) - 1)
    def _():
        tail_ref[0] = cbuf[0:SUBLANES, :]


def _sc_prompt(proj, w, *, batch, seq, width, col, ts):
    nt = seq // ts
    row = lambda b, j: b * nt + j
    return pl.pallas_call(
        _sc_prompt_kernel,
        grid=(batch, nt),
        in_specs=[
            pl.BlockSpec((ts, width), lambda b, j: (row(b, j), col["sc_b"] // width)),
            pl.BlockSpec((ts, width), lambda b, j: (row(b, j), col["sc_c"] // width)),
            pl.BlockSpec((ts, width), lambda b, j: (row(b, j), col["sc_h"] // width)),
            pl.BlockSpec(w.shape, lambda b, j: (0, 0)),
        ],
        out_specs=[
            pl.BlockSpec((ts, width), lambda b, j: (row(b, j), 0)),
            pl.BlockSpec((1, SUBLANES, width), lambda b, j: (b, 0, 0)),
        ],
        out_shape=[
            jax.ShapeDtypeStruct((proj.shape[0], width), BF16),
            jax.ShapeDtypeStruct((batch, SUBLANES, width), F32),
        ],
        scratch_shapes=[pltpu.VMEM((ts + SUBLANES, width), F32)],
        compiler_params=_cparams("parallel", "arbitrary"),
        name="shortconv_prompt",
    )(proj, proj, proj, w)


def _sc_sample_kernel(b_ref, c_ref, h_ref, st_ref, w_ref, yb_all_ref, yb_ref, ch_ref):
    taps = w_ref.shape[0]
    ch = c_ref[...] * h_ref[...]
    u = w_ref[taps - 1:taps, :] * ch
    for k in range(taps - 1):
        u = u + w_ref[k:k + 1, :] * st_ref[k]
    yb_ref[...] = (b_ref[...] * u).astype(yb_ref.dtype)
    ch_ref[...] = ch


def _sc_sample(proj, st, w, yb_all, *, row0, nb, width, col):
    r0 = row0 // nb
    return pl.pallas_call(
        _sc_sample_kernel,
        grid=(1,),
        in_specs=[
            pl.BlockSpec((nb, width), lambda i: (r0, col["sc_b"] // width)),
            pl.BlockSpec((nb, width), lambda i: (r0, col["sc_c"] // width)),
            pl.BlockSpec((nb, width), lambda i: (r0, col["sc_h"] // width)),
            pl.BlockSpec(st.shape, lambda i: (0, 0, 0)),
            pl.BlockSpec(w.shape, lambda i: (0, 0)),
            pl.BlockSpec(memory_space=pl.ANY),
        ],
        out_specs=[pl.BlockSpec((nb, width), lambda i: (r0, 0)),
                   pl.BlockSpec((nb, width), lambda i: (0, 0))],
        out_shape=[jax.ShapeDtypeStruct(yb_all.shape, BF16),
                   jax.ShapeDtypeStruct((nb, width), F32)],
        input_output_aliases={5: 0},
        compiler_params=_cparams("arbitrary"),
        name="shortconv_sample",
    )(proj, proj, proj, st, w, yb_all)


def _branch_kernel(ya_ref, yb_ref, wa_ref, wb_ref, ga_ref, gb_ref, o_ref):
    ta = _dot(ya_ref[...], wa_ref[...])
    tb = _dot(yb_ref[...], wb_ref[...])
    o_ref[...] = (_sigmoid(ga_ref[...]) * ta + _sigmoid(gb_ref[...]) * tb).astype(o_ref.dtype)


def _branch(ya, yb, wa, wb, proj, *, col, tm, tn):
    m, k = ya.shape
    n = wa.shape[1]
    return pl.pallas_call(
        _branch_kernel,
        grid=(m // tm, n // tn),
        in_specs=[
            pl.BlockSpec((tm, k), lambda i, j: (i, 0)),
            pl.BlockSpec((tm, yb.shape[1]), lambda i, j: (i, 0)),
            pl.BlockSpec((k, tn), lambda i, j: (0, j)),
            pl.BlockSpec((yb.shape[1], tn), lambda i, j: (0, j)),
            pl.BlockSpec((tm, tn), lambda i, j: (i, col["g_a"] // tn + j)),
            pl.BlockSpec((tm, tn), lambda i, j: (i, col["g_b"] // tn + j)),
        ],
        out_specs=pl.BlockSpec((tm, tn), lambda i, j: (i, j)),
        out_shape=jax.ShapeDtypeStruct((m, n), BF16),
        compiler_params=_cparams("parallel", "parallel"),
        name="branch_mix",
    )(ya, yb, wa, wb, proj, proj)


def _x1_kernel(x_ref, mix_ref, w_ref, g_ref, b_ref, o_ref, ob_ref, *, alpha):
    t = alpha * x_ref[...] + _dot(mix_ref[...], w_ref[...])
    x1 = _layer_norm(t, g_ref[...], b_ref[...])
    o_ref[...] = x1
    ob_ref[...] = pltpu.bitcast(x1.astype(BF16), U32)


def _x1(x, mix, w, g, b, *, alpha, tm):
    m, d = x.shape
    return pl.pallas_call(
        functools.partial(_x1_kernel, alpha=alpha),
        grid=(m // tm,),
        in_specs=[
            pl.BlockSpec((tm, d), lambda i: (i, 0)),
            pl.BlockSpec((tm, d), lambda i: (i, 0)),
            pl.BlockSpec(w.shape, lambda i: (0, 0)),
            pl.BlockSpec(g.shape, lambda i: (0, 0)),
            pl.BlockSpec(b.shape, lambda i: (0, 0)),
        ],
        out_specs=[pl.BlockSpec((tm, d), lambda i: (i, 0)), pl.BlockSpec((_words(tm), d), lambda i: (i, 0))],
        out_shape=[jax.ShapeDtypeStruct((m, d), F32), jax.ShapeDtypeStruct((_words(m), d), U32)],
        compiler_params=_cparams("parallel"),
        name="x1_out_ln",
    )(x, mix, w, g, b)


def _top_rows(s, k, with_rank=False):
    rows = []
    cur = s
    rank = jnp.full(s.shape, float(k), F32)
    for a in range(k):
        m = jnp.max(cur, axis=0, keepdims=True)
        rows.append(m)
        hit = cur == m
        if with_rank:
            rank = jnp.where(hit, float(a), rank)
        cur = jnp.where(hit, -jnp.inf, cur)
    return (rows, rank) if with_rank else rows


def _route_kernel(x_ref, wq_ref, k1_ref, k2_ref, c1_ref, w1_ref, r2_ref, e2_ref, q_scr, s1_scr, s2_scr):
    n_heads = k1_ref.shape[0]
    dk = k1_ref.shape[2]
    tm = q_scr.shape[0]
    K = PEER_TOPK
    q_scr[...] = _dot(pltpu.bitcast(x_ref[...], BF16), wq_ref[...]).astype(BF16)
    for h in range(n_heads):
        q1 = q_scr[:, (2 * h) * dk:(2 * h + 1) * dk]
        q2 = q_scr[:, (2 * h + 1) * dk:(2 * h + 2) * dk]
        s1_scr[...] = lax.dot_general(k1_ref[h], q1, NT_DIMS, preferred_element_type=F32)
        s2_scr[...] = lax.dot_general(k2_ref[h], q2, NT_DIMS, preferred_element_type=F32)
        for c in range(tm // LANES):
            cs = slice(c * LANES, (c + 1) * LANES)
            s1 = s1_scr[:, cs]
            s2 = s2_scr[:, cs]
            t1 = jnp.concatenate(_top_rows(s1, K), axis=0)
            rows2, rank2 = _top_rows(s2, K, with_rank=True)
            t2 = jnp.concatenate(rows2, axis=0)
            assert K == 2 * SUBLANES
            cand = jnp.concatenate(
                [t1[0:1] + t2] + [t1[a:a + 1] + t2[0:SUBLANES] for a in range(1, SUBLANES)]
                + [t1[SUBLANES:] + t2[0:1]], axis=0)
            thr = _top_rows(cand, K)[-1]
            m1 = t1[0:1]
            m2 = t2[0:1]
            zsum = jnp.sum(jnp.where(cand >= thr, jnp.exp(cand - (m1 + m2)), 0.0), axis=0, keepdims=True)
            cnt = jnp.zeros(s1.shape, F32)
            for b in range(K):
                cnt = cnt + jnp.where(s1 + t2[b:b + 1] >= thr, 1.0, 0.0)
            c1_ref[h, :, cs] = cnt
            w1_ref[h, :, cs] = jnp.exp(s1 - m1) / zsum
            r2_ref[h, :, cs] = pltpu.bitcast(rank2.astype(BF16), U32)
            e2_ref[h, :, cs] = pltpu.bitcast(jnp.exp(s2 - m2).astype(BF16), U32)


def _route(x1p, wq, k1, k2, *, m, tm):
    d = x1p.shape[1]
    n_heads, n_keys, _ = k1.shape
    tok = lambda rows: pl.BlockSpec((n_heads, rows, tm), lambda i: (0, 0, i))
    shp = lambda rows, dt: jax.ShapeDtypeStruct((n_heads, rows, m), dt)
    return pl.pallas_call(
        _route_kernel,
        grid=(m // tm,),
        in_specs=[
            pl.BlockSpec((_words(tm), d), lambda i: (i, 0)),
            pl.BlockSpec(wq.shape, lambda i: (0, 0)),
            pl.BlockSpec(k1.shape, lambda i: (0, 0, 0)),
            pl.BlockSpec(k2.shape, lambda i: (0, 0, 0)),
        ],
        out_specs=[tok(n_keys), tok(n_keys), tok(_words(n_keys)), tok(_words(n_keys))],
        out_shape=[shp(n_keys, F32), shp(n_keys, F32), shp(_words(n_keys), U32), shp(_words(n_keys), U32)],
        scratch_shapes=[pltpu.VMEM((tm, wq.shape[1]), BF16),
                        pltpu.VMEM((n_keys, tm), F32), pltpu.VMEM((n_keys, tm), F32)],
        compiler_params=_cparams("parallel"),
        name="peer_route",
    )(x1p, wq, k1, k2)


def _expert_kernel(x_ref, u_ref, vt_ref, c1_ref, w1_ref, r2_ref, e2_ref, o_ref, acc, coef, hid):
    e = pl.program_id(1)
    n_heads, per, tm = c1_ref.shape
    n_keys = hid.shape[0] // per
    half = _words(n_keys)

    @pl.when(e == 0)
    def _():
        acc[...] = jnp.zeros_like(acc)

    hid[...] = lax.dot_general(pltpu.bitcast(u_ref[...], BF16), pltpu.bitcast(x_ref[...], BF16), NT_DIMS,
                               preferred_element_type=F32)
    for k in range(per):
        for c in range(tm // LANES):
            cs = slice(c * LANES, (c + 1) * LANES)
            gate = jnp.zeros((n_keys, LANES), BF16)
            for h in range(n_heads):
                cnt = c1_ref[h, k:k + 1, cs].astype(BF16)
                w1 = w1_ref[h, k:k + 1, cs].astype(BF16)
                r2 = pltpu.bitcast(r2_ref[h, :, cs], BF16)
                e2 = pltpu.bitcast(e2_ref[h, :, cs], BF16)
                gate = gate + jnp.where(r2 < cnt, e2 * w1, jnp.zeros_like(gate))
            hk = hid[k * n_keys:(k + 1) * n_keys, cs]
            gelu = 0.5 * hk * (1.0 + lax.erf(hk * (1.0 / math.sqrt(2.0))))
            coef[k * half:(k + 1) * half, cs] = pltpu.bitcast(gate * gelu.astype(BF16), U32)

    acc[...] += _dot(pltpu.bitcast(vt_ref[...], BF16), pltpu.bitcast(coef[...], BF16))

    @pl.when(e == pl.num_programs(1) - 1)
    def _():
        o_ref[...] = acc[...].T.astype(o_ref.dtype)


def _experts(x1p, u, vt, c1, w1, r2, e2, *, m, tm, te):
    d = x1p.shape[1]
    n_tiles = u.shape[0] // _words(te)
    n_heads, n_keys, _ = c1.shape
    tok = lambda rows: pl.BlockSpec((n_heads, rows, tm), lambda i, e: (0, 0, i))
    per = te // n_keys
    assert per % SUBLANES == 0
    row = pl.BlockSpec((n_heads, per, tm), lambda i, e: (0, e, i))
    return pl.pallas_call(
        _expert_kernel,
        grid=(m // tm, n_tiles),
        in_specs=[
            pl.BlockSpec((_words(tm), d), lambda i, e: (i, 0)),
            pl.BlockSpec((_words(te), d), lambda i, e: (e, 0)),
            pl.BlockSpec((_words(d), te), lambda i, e: (0, e)),
            row, row, tok(_words(n_keys)), tok(_words(n_keys)),
        ],
        out_specs=pl.BlockSpec((tm, d), lambda i, e: (i, 0)),
        out_shape=jax.ShapeDtypeStruct((m, d), BF16),
        scratch_shapes=[pltpu.VMEM((d, tm), F32),
                        pltpu.VMEM((_words(te), tm), U32),
                        pltpu.VMEM((te, tm), F32)],
        compiler_params=_cparams("parallel", "arbitrary"),
        name="peer_experts",
    )(x1p, u, vt, c1, w1, r2, e2)


def _final_kernel(x1_ref, peer_ref, p_ref, wg_ref, wp_ref, g_ref, b_ref, o_ref, *, alpha):
    x2 = _layer_norm(alpha * x1_ref[...] + peer_ref[...], g_ref[...], b_ref[...])
    gate = _sigmoid(_dot(x2.astype(BF16), wg_ref[...]))
    y = x2 + gate * _dot(p_ref[...].astype(BF16), wp_ref[...])
    if len(o_ref.shape) == 3:
        o_ref[:, 0, :] = y
    else:
        o_ref[...] = y


def _final(x1, peer, p, wg, wp, g, b, *, alpha, row0, tm, per_step):
    d = x1.shape[1]
    n = p.shape[0]
    r0 = row0 // tm
    rows = lambda i: (r0 + i, 0)
    const = lambda i: (0, 0)
    if per_step:
        out_shape, out_spec = (n, 1, d), pl.BlockSpec((tm, 1, d), lambda i: (i, 0, 0))
    else:
        out_shape, out_spec = (n, d), pl.BlockSpec((tm, d), lambda i: (i, 0))
    return pl.pallas_call(
        functools.partial(_final_kernel, alpha=alpha),
        grid=(n // tm,),
        in_specs=[
            pl.BlockSpec((tm, d), rows),
            pl.BlockSpec((tm, d), rows),
            pl.BlockSpec((tm, p.shape[1]), lambda i: (i, 0)),
            pl.BlockSpec(wg.shape, const),
            pl.BlockSpec(wp.shape, const),
            pl.BlockSpec(g.shape, const),
            pl.BlockSpec(b.shape, const),
        ],
        out_specs=out_spec,
        out_shape=jax.ShapeDtypeStruct(out_shape, F32),
        compiler_params=_cparams("parallel"),
        name="ln2_ple",
    )(x1, peer, p, wg, wp, g, b)


TOKEN_TILE = 640
ROW_TILE = 320
FINAL_TILE = 512
EXPERT_TILE = 1024
PEER_TOKEN_TILE = 768


def _pack_kernel(w_ref, o_ref, *, transpose):
    w = w_ref[...]
    o_ref[...] = pltpu.bitcast((w.T if transpose else w).astype(BF16), U32)


def _pack_rows(w, transpose=False, tr=512):
    r, c = w.shape
    if transpose:
        out_shape, out_spec = (_words(c), r), pl.BlockSpec((_words(c), tr), lambda i: (0, i))
    else:
        out_shape, out_spec = (_words(r), c), pl.BlockSpec((_words(tr), c), lambda i: (i, 0))
    return pl.pallas_call(
        functools.partial(_pack_kernel, transpose=transpose),
        grid=(r // tr,),
        in_specs=[pl.BlockSpec((tr, c), lambda i: (i, 0))],
        out_specs=out_spec,
        out_shape=jax.ShapeDtypeStruct(out_shape, U32),
        compiler_params=_cparams("parallel"),
        name="pack_weight",
    )(w)


def _pad_lanes(v):
    return jnp.pad(v.astype(F32), (0, LANES - v.shape[0])).reshape(1, LANES)


def _layer(x, p_p, p_s, ssm_h, conv_buf, sc_buf, n_prompt, batch, seq, depth,
           w_in, ssd_conv_w, ssd_conv_b, ssd_dt_bias, ssd_a_log, ssd_d, ssd_norm_w,
           sc_conv_w, w_branch_ssd, w_branch_sc, w_out, ln1_g, ln1_b,
           peer_wq, peer_keys1, peer_keys2, peer_u, peer_v, ln2_g, ln2_b,
           ple_gate_w, ple_proj_w):
    m, d = x.shape
    nb = m - n_prompt
    n_heads = ssd_dt_bias.shape[0]
    hw, n_state = ssm_h.shape[1] * ssm_h.shape[2], ssm_h.shape[3]
    head_dim = ssm_h.shape[2]
    conv_dim = ssd_conv_w.shape[1]
    gn = (conv_dim - hw) // 2
    n_groups = gn // n_state
    scw = sc_conv_w.shape[1]
    alpha = (2.0 * depth) ** 0.25
    assert hw == scw == d and n_heads <= LANES and LANES % head_dim == 0 and n_state == LANES

    o_dt = hw + conv_dim
    o_scb = o_dt + n_heads
    col = {"z": 0, "xs": hw, "B": 2 * hw, "C": 2 * hw + gn}
    col2 = {"sc_b": 0, "sc_c": scw, "sc_h": 2 * scw, "g_a": 3 * scw, "g_b": 3 * scw + d}
    tm = _tile(m, TOKEN_TILE, LANES)
    proj = _in_proj(x, w_in, col0=0, n=o_dt, tm=tm)
    proj2 = _in_proj(x, w_in, col0=o_scb, n=3 * scw + 2 * d, tm=tm)
    dtraw = _in_proj(x, w_in, col0=o_dt, n=LANES, tm=tm)

    ch_head = jnp.arange(hw) // head_dim
    e1 = (jnp.arange(LANES)[:, None] == ch_head[None, :]).astype(BF16)
    e2 = (jnp.arange(LANES)[:, None] == (jnp.arange(n_heads * LANES) // LANES)[None, :]).astype(BF16)
    convb = ssd_conv_b.reshape(1, conv_dim)
    dtb, alog = _pad_lanes(ssd_dt_bias), _pad_lanes(ssd_a_log)
    de = jnp.repeat(ssd_d.astype(F32), head_dim).reshape(1, hw)
    normw = ssd_norm_w.reshape(1, hw)
    shp = dict(hw=hw, gn=gn, n_groups=n_groups, n_state=n_state, col=col)
    ya_p, h_p = _ssd_prompt(proj, dtraw, ssd_conv_w, convb, dtb, alog, de, normw, e1, e2,
                            batch=batch, seq=seq, head_dim=head_dim, **shp)
    cst = jnp.transpose(conv_buf, (1, 0, 2))
    ya, h_s = _ssd_sample(proj, dtraw, cst, ssm_h.reshape(nb, hw, n_state), ssd_conv_w, convb,
                          dtb, alog, de, normw, e1, ya_p, row0=n_prompt, nb=nb, **shp)

    yb_p, sc_tail = _sc_prompt(proj2, sc_conv_w, batch=batch, seq=seq, width=scw, col=col2,
                               ts=min(seq, 256))
    yb, ch_s = _sc_sample(proj2, jnp.transpose(sc_buf, (1, 0, 2)), sc_conv_w, yb_p,
                          row0=n_prompt, nb=nb, width=scw, col=col2)

    mix = _branch(ya, yb, w_branch_ssd.astype(BF16), w_branch_sc.astype(BF16), proj2,
                  col=col2, tm=tm, tn=_tile(d, 512, LANES))
    tm2 = _tile(m, ROW_TILE, SUBLANES)
    x1, x1b = _x1(x, mix, w_out.astype(BF16), ln1_g.reshape(1, d), ln1_b.reshape(1, d),
                  alpha=alpha, tm=tm2)

    tmp = PEER_TOKEN_TILE
    mp = m + (-m % tmp)
    x1p = jnp.pad(x1b, ((0, _words(mp - m)), (0, 0)))
    route = _route(x1p, peer_wq.astype(BF16), peer_keys1.astype(BF16), peer_keys2.astype(BF16), m=mp, tm=tmp)
    peer = _experts(x1p, _pack_rows(peer_u), _pack_rows(peer_v, transpose=True), *route,
                    m=mp, tm=tmp, te=EXPERT_TILE)

    fin = functools.partial(_final, x1, peer, wg=ple_gate_w.astype(BF16), wp=ple_proj_w.astype(BF16),
                            g=ln2_g.reshape(1, d), b=ln2_b.reshape(1, d), alpha=alpha)
    assert n_prompt % nb == 0
    y_p = fin(p=p_p, row0=0, tm=_tile(n_prompt, FINAL_TILE, SUBLANES), per_step=False)
    y_s = fin(p=p_s, row0=n_prompt, tm=nb, per_step=True)

    k_ssd = ssd_conv_w.shape[0] - 1
    k_sc = sc_conv_w.shape[0] - 1
    xbc_cols = lambda rows: rows[..., hw:hw + conv_dim]
    conv_p = xbc_cols(jnp.stack([proj[(b + 1) * seq - k_ssd:(b + 1) * seq] for b in range(batch)]))
    conv_s = jnp.concatenate([conv_buf[:, 1:, :], xbc_cols(proj[n_prompt:])[:, None, :]], axis=1)
    sc_p = sc_tail[:, SUBLANES - k_sc:, :]
    sc_s = jnp.concatenate([sc_buf[:, 1:, :], ch_s[:, None, :]], axis=1)
    hshape = (-1, n_heads, head_dim, n_state)
    return y_p, y_s, conv_p, h_p.reshape(hshape), sc_p, conv_s, h_s.reshape(hshape), sc_s


def kernel(x_prompt, x_sample, p_prompt, p_sample, state_ssm, state_ssd_conv, state_shortconv, w_in, ssd_conv_w, ssd_conv_b, ssd_dt_bias, ssd_a_log, ssd_d, ssd_norm_w, sc_conv_w, w_branch_ssd, w_branch_sc, w_out, ln1_g, ln1_b, peer_wq, peer_keys1, peer_keys2, peer_u, peer_v, ln2_g, ln2_b, ple_gate_w, ple_proj_w):
    batch, seq, d = x_prompt.shape
    nb, dec_seq, _ = x_sample.shape
    assert dec_seq == 1 and seq % SSD_CHUNK == 0
    depth = w_in.shape[0]
    n_prompt = batch * seq
    x = jnp.concatenate([x_prompt.reshape(n_prompt, d), x_sample.reshape(nb, d)], axis=0)
    weights = (w_in, ssd_conv_w, ssd_conv_b, ssd_dt_bias, ssd_a_log, ssd_d, ssd_norm_w,
               sc_conv_w, w_branch_ssd, w_branch_sc, w_out, ln1_g, ln1_b,
               peer_wq, peer_keys1, peer_keys2, peer_u, peer_v, ln2_g, ln2_b,
               ple_gate_w, ple_proj_w)
    outs = [[] for _ in range(6)]
    for i in range(depth):
        y_p, y_s, conv_p, h_p, sc_p, conv_s, h_s, sc_s = _layer(
            x, p_prompt[i].reshape(n_prompt, -1), p_sample[i].reshape(nb, -1),
            state_ssm[i], state_ssd_conv[i], state_shortconv[i], n_prompt, batch, seq, depth,
            *[w[i] for w in weights])
        for lst, val in zip(outs, (h_p, conv_p, sc_p, h_s, conv_s, sc_s)):
            lst.append(val)
        if i + 1 < depth:
            x = jnp.concatenate([y_p, y_s.reshape(nb, d)], axis=0)
    y_prompt = y_p.reshape(batch, seq, d)
    y_sample = y_s
    return (y_prompt, y_sample) + tuple(jnp.stack(lst) for lst in outs)
```

```python
import functools
import math

import jax
import jax.numpy as jnp
from jax import lax
from jax.experimental import pallas as pl
from jax.experimental.pallas import tpu as pltpu

F32 = jnp.float32
BF16 = jnp.bfloat16
U32 = jnp.uint32

LANES = 128
SUBLANES = 8
BF16_ROWS = 16
PEER_TOPK = 16
SSD_CHUNK = 128
LN_EPS = 1e-5
RMS_EPS = 1e-5
VMEM_LIMIT = 56 * 1024 * 1024

NT_DIMS = (((1,), (1,)), ((), ()))
TN_DIMS = (((0,), (0,)), ((), ()))


def _cparams(*sem, flags=None):
    return pltpu.CompilerParams(dimension_semantics=sem, vmem_limit_bytes=VMEM_LIMIT, flags=flags)


def _dot(a, b):
    return jnp.dot(a, b, preferred_element_type=F32)


def _split3(v):
    hi = v.astype(BF16)
    r = v - hi.astype(F32)
    mid = r.astype(BF16)
    lo = (r - mid.astype(F32)).astype(BF16)
    return hi, mid, lo


def _dot3_lhs(v, rhs_bf16):
    hi, mid, lo = _split3(v)
    return _dot(hi, rhs_bf16) + _dot(mid, rhs_bf16) + _dot(lo, rhs_bf16)


def _dot3_rhs(lhs_bf16, v):
    hi, mid, lo = _split3(v)
    return _dot(lhs_bf16, hi) + _dot(lhs_bf16, mid) + _dot(lhs_bf16, lo)


def _words(rows):
    return rows * jnp.dtype(BF16).itemsize // jnp.dtype(U32).itemsize


def _sigmoid(x):
    return 1.0 / (1.0 + jnp.exp(-x))


def _silu(x):
    return x * _sigmoid(x)


def _softplus(x):
    return jnp.maximum(x, 0.0) + jnp.log1p(jnp.exp(-jnp.abs(x)))


def _layer_norm(x, g, b):
    mu = jnp.mean(x, axis=-1, keepdims=True)
    xc = x - mu
    var = jnp.mean(xc * xc, axis=-1, keepdims=True)
    return xc * lax.rsqrt(var + LN_EPS) * g + b


def _tile(m, cap, mult):
    best = None
    for t in range(mult, min(m, cap) + 1, mult):
        if m % t == 0:
            best = t
    assert best is not None, (m, cap, mult)
    return best


def _in_proj_kernel(x_ref, wa_ref, wb_ref, o_ref, w_scr, *, shift):
    tn = w_scr.shape[0]

    @pl.when(pl.program_id(1) == 0)
    def _():
        if shift:
            w = jnp.concatenate([wa_ref[...], wb_ref[...]], axis=0)[shift:shift + tn]
        else:
            w = wa_ref[...]
        w_scr[...] = w.astype(BF16)

    o_ref[...] = lax.dot_general(x_ref[...].astype(BF16), w_scr[...], NT_DIMS, preferred_element_type=F32)


def _in_proj(x, wt, *, col0, n, tm):
    m, k = x.shape
    shift = col0 % LANES
    assert shift % SUBLANES == 0
    base = col0 - shift
    tn = _tile(math.gcd(n, base) if base else n, 1024, LANES)
    last = pl.cdiv(wt.shape[0], LANES) - 1
    return pl.pallas_call(
        functools.partial(_in_proj_kernel, shift=shift),
        grid=(n // tn, m // tm),
        in_specs=[pl.BlockSpec((tm, k), lambda j, i: (i, 0)),
                  pl.BlockSpec((tn, k), lambda j, i: (base // tn + j, 0)),
                  pl.BlockSpec((LANES, k), lambda j, i: (jnp.minimum((base + (j + 1) * tn) // LANES, last), 0))],
        out_specs=pl.BlockSpec((tm, tn), lambda j, i: (i, j)),
        out_shape=jax.ShapeDtypeStruct((m, n), F32),
        scratch_shapes=[pltpu.VMEM((tn, k), BF16)],
        compiler_params=_cparams("parallel", "arbitrary"),
        name="in_proj",
    )(x, wt, wt)


def _ssd_gate_norm(y, z, normw_ref, out_ref, n_groups):
    y = y * _silu(z)
    gw = y.shape[1] // n_groups
    for g in range(n_groups):
        sl = slice(g * gw, (g + 1) * gw)
        yg = y[:, sl]
        ms = jnp.mean(yg * yg, axis=-1, keepdims=True)
        out_ref[:, sl] = (yg * lax.rsqrt(ms + RMS_EPS) * normw_ref[:, sl]).astype(out_ref.dtype)


def _ssd_prompt_kernel(z_ref, xs_ref, b_ref, c_ref, dt_ref, convw_ref, convb_ref, dtb_ref,
                       alog_ref, de_ref, normw_ref, e1_ref, e2_ref,
                       ya_ref, hout_ref,
                       cbuf, act, h_scr, x_scr, xd_scr, eae_scr, acsb_scr, acst_scr, y_scr,
                       *, n_groups, n_state, head_dim):
    c = pl.program_id(1)
    L = SSD_CHUNK
    hw = xs_ref.shape[1]
    gn = b_ref.shape[1]
    w_all = hw + 2 * gn
    n_heads = hw // head_dim
    hpl = LANES // head_dim
    n_blk = n_heads // hpl
    blk_per_group = n_blk // n_groups
    taps = convw_ref.shape[0]

    @pl.when(c == 0)
    def _():
        h_scr[...] = jnp.zeros_like(h_scr)
        cbuf[0:SUBLANES, :] = jnp.zeros((SUBLANES, w_all), F32)

    cbuf[SUBLANES:SUBLANES + L, 0:hw] = xs_ref[...]
    cbuf[SUBLANES:SUBLANES + L, hw:hw + gn] = b_ref[...]
    cbuf[SUBLANES:SUBLANES + L, hw + gn:] = c_ref[...]

    cw = math.gcd(w_all, 512)
    for blk in range(w_all // cw):
        sl = slice(blk * cw, (blk + 1) * cw)
        acc = convb_ref[:, sl] + convw_ref[taps - 1:taps, sl] * cbuf[SUBLANES:SUBLANES + L, sl]
        for j in range(1, taps):
            acc = acc + convw_ref[taps - 1 - j:taps - j, sl] * cbuf[SUBLANES - j:SUBLANES - j + L, sl]
        act[:, sl] = _silu(acc)
    cbuf[0:SUBLANES, :] = cbuf[L:L + SUBLANES, :]

    dt = _softplus(dt_ref[...] + dtb_ref[...])
    a_neg = -jnp.exp(alog_ref[...])
    dta = dt * a_neg
    ri = lax.broadcasted_iota(jnp.int32, (L, L), 0)
    ci = lax.broadcasted_iota(jnp.int32, (L, L), 1)
    causal = ri >= ci
    tri = jnp.where(causal, 1.0, 0.0).astype(BF16)
    acs = _dot3_rhs(tri, dta)
    e1 = e1_ref[...]
    dte = _dot3_lhs(dt, e1)
    acs_p = _split3(acs)
    acse = _dot(acs_p[0], e1) + _dot(acs_p[1], e1) + _dot(acs_p[2], e1)
    e2 = e2_ref[...]
    acsb_scr[...] = _dot(acs_p[0], e2) + _dot(acs_p[1], e2) + _dot(acs_p[2], e2)
    acst_scr[...] = acs.T

    xdt = act[:, 0:hw] * dte
    x_scr[...] = xdt.astype(BF16)
    xd_scr[...] = (xdt * jnp.exp(acse[L - 1:L, :] - acse)).astype(BF16)
    eae_scr[...] = jnp.exp(acse)

    lane = lax.broadcasted_iota(jnp.int32, (L, LANES), 1)
    cb = None
    for j in range(n_blk):
        g = j // blk_per_group
        bsl = slice(hw + g * n_state, hw + (g + 1) * n_state)
        csl = slice(hw + gn + g * n_state, hw + gn + (g + 1) * n_state)
        bg = act[:, bsl].astype(BF16)
        cg = act[:, csl].astype(BF16)
        if j % blk_per_group == 0:
            cb = lax.dot_general(cg, bg, NT_DIMS, preferred_element_type=F32)
        psl = slice(j * LANES, (j + 1) * LANES)
        xp = x_scr[:, psl]
        ydiag = None
        cds = []
        for q in range(hpl):
            r = j * hpl + q
            ab = acsb_scr[:, r * LANES:(r + 1) * LANES]
            at = jnp.broadcast_to(acst_scr[r:r + 1, :], (L, L))
            lm = jnp.where(causal, jnp.exp(ab - at), 0.0)
            m = (cb * lm).astype(BF16)
            inhead = (lane >= q * head_dim) & (lane < (q + 1) * head_dim)
            xq = jnp.where(inhead, xp, jnp.zeros_like(xp))
            yq = _dot(m, xq)
            ydiag = yq if ydiag is None else ydiag + yq
            cds.append(jnp.broadcast_to(jnp.exp(acsb_scr[L - 1:L, r * LANES:(r + 1) * LANES]),
                                        (head_dim, LANES)))
        cd = jnp.concatenate(cds, axis=0)
        hp = h_scr[psl, :]
        yoff = lax.dot_general(cg, hp.astype(BF16), NT_DIMS, preferred_element_type=F32)
        yoff = yoff * eae_scr[:, psl]
        st = lax.dot_general(xd_scr[:, psl], bg, TN_DIMS, preferred_element_type=F32)
        h_scr[psl, :] = hp * cd + st
        y_scr[:, psl] = ydiag + yoff + act[:, psl] * de_ref[:, psl]

    _ssd_gate_norm(y_scr[...], z_ref[...], normw_ref, ya_ref, n_groups)

    @pl.when(c == pl.num_programs(1) - 1)
    def _():
        hout_ref[0] = h_scr[...]


def _ssd_prompt(proj, dtraw, convw, convb, dtb, alog, de, normw, e1, e2, *, batch, seq,
                hw, gn, n_groups, n_state, head_dim, col):
    L = SSD_CHUNK
    nc = seq // L
    n_heads = hw // head_dim
    w_all = hw + 2 * gn
    row = lambda b, c: b * nc + c
    const = lambda b, c: (0, 0)
    kern = functools.partial(_ssd_prompt_kernel, n_groups=n_groups, n_state=n_state, head_dim=head_dim)
    return pl.pallas_call(
        kern,
        grid=(batch, nc),
        in_specs=[
            pl.BlockSpec((L, hw), lambda b, c: (row(b, c), col["z"] // hw)),
            pl.BlockSpec((L, hw), lambda b, c: (row(b, c), col["xs"] // hw)),
            pl.BlockSpec((L, gn), lambda b, c: (row(b, c), col["B"] // gn)),
            pl.BlockSpec((L, gn), lambda b, c: (row(b, c), col["C"] // gn)),
            pl.BlockSpec((L, LANES), lambda b, c: (row(b, c), 0)),
            pl.BlockSpec(convw.shape, const),
            pl.BlockSpec(convb.shape, const),
            pl.BlockSpec(dtb.shape, const),
            pl.BlockSpec(alog.shape, const),
            pl.BlockSpec(de.shape, const),
            pl.BlockSpec(normw.shape, const),
            pl.BlockSpec(e1.shape, const),
            pl.BlockSpec(e2.shape, const),
        ],
        out_specs=[
            pl.BlockSpec((L, hw), lambda b, c: (row(b, c), 0)),
            pl.BlockSpec((1, hw, n_state), lambda b, c: (b, 0, 0)),
        ],
        out_shape=[
            jax.ShapeDtypeStruct((proj.shape[0], hw), BF16),
            jax.ShapeDtypeStruct((batch, hw, n_state), F32),
        ],
        scratch_shapes=[
            pltpu.VMEM((L + SUBLANES, w_all), F32),
            pltpu.VMEM((L, w_all), F32),
            pltpu.VMEM((hw, n_state), F32),
            pltpu.VMEM((L, hw), BF16),
            pltpu.VMEM((L, hw), BF16),
            pltpu.VMEM((L, hw), F32),
            pltpu.VMEM((L, n_heads * LANES), F32),
            pltpu.VMEM((L, L), F32),
            pltpu.VMEM((L, hw), F32),
        ],
        compiler_params=_cparams("parallel", "arbitrary"),
        name="ssd_prompt",
    )(proj, proj, proj, proj, dtraw, convw, convb, dtb, alog, de, normw, e1, e2)


def _ssd_sample_kernel(z_ref, xs_ref, b_ref, c_ref, dt_ref, cst_ref, h_ref, convw_ref, convb_ref,
                       dtb_ref, alog_ref, de_ref, normw_ref, e1_ref, ya_all_ref,
                       ya_ref, hout_ref, xbc, y_scr,
                       *, n_groups, n_state):
    bb = xs_ref.shape[0]
    hw = xs_ref.shape[1]
    gn = b_ref.shape[1]
    taps = convw_ref.shape[0]
    gw = hw // n_groups

    xbc[:, 0:hw] = xs_ref[...]
    xbc[:, hw:hw + gn] = b_ref[...]
    xbc[:, hw + gn:] = c_ref[...]
    acc = convb_ref[...] + convw_ref[taps - 1:taps, :] * xbc[...]
    for j in range(taps - 1):
        acc = acc + convw_ref[j:j + 1, :] * cst_ref[j]
    act = _silu(acc)
    xs = act[:, 0:hw]

    dt = _softplus(dt_ref[...] + dtb_ref[...])
    dec = jnp.exp(dt * (-jnp.exp(alog_ref[...])))
    e1 = e1_ref[...]
    dte = _dot3_lhs(dt, e1)
    dece = _dot3_lhs(dec, e1)
    xdt = xs * dte

    pieces = [p.astype(F32) for p in _split3(dece)] + [p.astype(F32) for p in _split3(xdt)]
    npc = len(pieces)
    stack = jnp.concatenate(pieces + [jnp.zeros((LANES - npc * bb, hw), F32)], axis=0)
    lt = stack.T.astype(BF16)

    krow = lax.broadcasted_iota(jnp.int32, (LANES, LANES), 0)
    rowid = lax.broadcasted_iota(jnp.int32, (bb, gw), 0)
    half = npc // 2
    y_scr[...] = jnp.zeros_like(y_scr)
    for s in range(bb):
        is_s = (krow % bb) == s
        sel_dec = jnp.where(is_s & (krow < half * bb), 1.0, 0.0).astype(BF16)
        sel_x = jnp.where(is_s & (krow >= half * bb) & (krow < npc * bb), 1.0, 0.0).astype(BF16)
        dec_b = _dot(lt, sel_dec)
        x_b = _dot(lt, sel_x)
        for g in range(n_groups):
            rows = slice(g * gw, (g + 1) * gw)
            brow = act[s:s + 1, hw + g * n_state:hw + (g + 1) * n_state]
            hn = h_ref[s, rows, :] * dec_b[rows, :] + x_b[rows, :] * brow
            hout_ref[s, rows, :] = hn
            cg = act[:, hw + gn + g * n_state:hw + gn + (g + 1) * n_state].astype(BF16)
            yg = lax.dot_general(cg, hn.astype(BF16), NT_DIMS, preferred_element_type=F32)
            y_scr[:, rows] = y_scr[:, rows] + jnp.where(rowid == s, yg, 0.0)

    y = y_scr[...] + xs * de_ref[...]
    _ssd_gate_norm(y, z_ref[...], normw_ref, ya_ref, n_groups)


def _ssd_sample(proj, dtraw, cst, h0, convw, convb, dtb, alog, de, normw, e1, ya_all, *, row0, nb,
                hw, gn, n_groups, n_state, col):
    bb = SUBLANES
    w_all = hw + 2 * gn
    r0 = row0 // bb
    const = lambda i: (0, 0)
    kern = functools.partial(_ssd_sample_kernel, n_groups=n_groups, n_state=n_state)
    return pl.pallas_call(
        kern,
        grid=(nb // bb,),
        in_specs=[
            pl.BlockSpec((bb, hw), lambda i: (r0 + i, col["z"] // hw)),
            pl.BlockSpec((bb, hw), lambda i: (r0 + i, col["xs"] // hw)),
            pl.BlockSpec((bb, gn), lambda i: (r0 + i, col["B"] // gn)),
            pl.BlockSpec((bb, gn), lambda i: (r0 + i, col["C"] // gn)),
            pl.BlockSpec((bb, LANES), lambda i: (r0 + i, 0)),
            pl.BlockSpec((cst.shape[0], bb, w_all), lambda i: (0, i, 0)),
            pl.BlockSpec((bb, hw, n_state), lambda i: (i, 0, 0)),
            pl.BlockSpec(convw.shape, const),
            pl.BlockSpec(convb.shape, const),
            pl.BlockSpec(dtb.shape, const),
            pl.BlockSpec(alog.shape, const),
            pl.BlockSpec(de.shape, const),
            pl.BlockSpec(normw.shape, const),
            pl.BlockSpec(e1.shape, const),
            pl.BlockSpec(memory_space=pl.ANY),
        ],
        out_specs=[
            pl.BlockSpec((bb, hw), lambda i: (r0 + i, 0)),
            pl.BlockSpec((bb, hw, n_state), lambda i: (i, 0, 0)),
        ],
        out_shape=[
            jax.ShapeDtypeStruct(ya_all.shape, BF16),
            jax.ShapeDtypeStruct((nb, hw, n_state), F32),
        ],
        scratch_shapes=[pltpu.VMEM((bb, w_all), F32), pltpu.VMEM((bb, hw), F32)],
        compiler_params=_cparams("parallel"),
        input_output_aliases={14: 0},
        name="ssd_sample",
    )(proj, proj, proj, proj, dtraw, cst, h0, convw, convb, dtb, alog, de, normw, e1, ya_all)


def _sc_prompt_kernel(b_ref, c_ref, h_ref, w_ref, yb_ref, tail_ref, cbuf):
    j = pl.program_id(1)
    ts = b_ref.shape[0]
    taps = w_ref.shape[0]

    @pl.when(j == 0)
    def _():
        cbuf[0:SUBLANES, :] = jnp.zeros((SUBLANES, cbuf.shape[1]), F32)

    cbuf[SUBLANES:SUBLANES + ts, :] = c_ref[...] * h_ref[...]
    u = w_ref[taps - 1:taps, :] * cbuf[SUBLANES:SUBLANES + ts, :]
    for k in range(1, taps):
        u = u + w_ref[taps - 1 - k:taps - k, :] * cbuf[SUBLANES - k:SUBLANES - k + ts, :]
    yb_ref[...] = (b_ref[...] * u).astype(yb_ref.dtype)
    cbuf[0:SUBLANES, :] = cbuf[ts:ts + SUBLANES, :]

    @pl.when(j == pl.num_programs(1) - 1)
    def _():
        tail_ref[0] = cbuf[0:SUBLANES, :]


def _sc_prompt(proj, w, *, batch, seq, width, col, ts):
    nt = seq // ts
    row = lambda b, j: b * nt + j
    return pl.pallas_call(
        _sc_prompt_kernel,
        grid=(batch, nt),
        in_specs=[
            pl.BlockSpec((ts, width), lambda b, j: (row(b, j), col["sc_b"] // width)),
            pl.BlockSpec((ts, width), lambda b, j: (row(b, j), col["sc_c"] // width)),
            pl.BlockSpec((ts, width), lambda b, j: (row(b, j), col["sc_h"] // width)),
            pl.BlockSpec(w.shape, lambda b, j: (0, 0)),
        ],
        out_specs=[
            pl.BlockSpec((ts, width), lambda b, j: (row(b, j), 0)),
            pl.BlockSpec((1, SUBLANES, width), lambda b, j: (b, 0, 0)),
        ],
        out_shape=[
            jax.ShapeDtypeStruct((proj.shape[0], width), BF16),
            jax.ShapeDtypeStruct((batch, SUBLANES, width), F32),
        ],
        scratch_shapes=[pltpu.VMEM((ts + SUBLANES, width), F32)],
        compiler_params=_cparams("parallel", "arbitrary"),
        name="shortconv_prompt",
    )(proj, proj, proj, w)


def _sc_sample_kernel(b_ref, c_ref, h_ref, st_ref, w_ref, yb_all_ref, yb_ref, ch_ref):
    taps = w_ref.shape[0]
    ch = c_ref[...] * h_ref[...]
    u = w_ref[taps - 1:taps, :] * ch
    for k in range(taps - 1):
        u = u + w_ref[k:k + 1, :] * st_ref[k]
    yb_ref[...] = (b_ref[...] * u).astype(yb_ref.dtype)
    ch_ref[...] = ch


def _sc_sample(proj, st, w, yb_all, *, row0, nb, width, col):
    r0 = row0 // nb
    return pl.pallas_call(
        _sc_sample_kernel,
        grid=(1,),
        in_specs=[
            pl.BlockSpec((nb, width), lambda i: (r0, col["sc_b"] // width)),
            pl.BlockSpec((nb, width), lambda i: (r0, col["sc_c"] // width)),
            pl.BlockSpec((nb, width), lambda i: (r0, col["sc_h"] // width)),
            pl.BlockSpec(st.shape, lambda i: (0, 0, 0)),
            pl.BlockSpec(w.shape, lambda i: (0, 0)),
            pl.BlockSpec(memory_space=pl.ANY),
        ],
        out_specs=[pl.BlockSpec((nb, width), lambda i: (r0, 0)),
                   pl.BlockSpec((nb, width), lambda i: (0, 0))],
        out_shape=[jax.ShapeDtypeStruct(yb_all.shape, BF16),
                   jax.ShapeDtypeStruct((nb, width), F32)],
        input_output_aliases={5: 0},
        compiler_params=_cparams("arbitrary"),
        name="shortconv_sample",
    )(proj, proj, proj, st, w, yb_all)


def _branch_kernel(ya_ref, yb_ref, wa_ref, wb_ref, ga_ref, gb_ref, o_ref):
    ta = _dot(ya_ref[...], wa_ref[...])
    tb = _dot(yb_ref[...], wb_ref[...])
    o_ref[...] = (_sigmoid(ga_ref[...]) * ta + _sigmoid(gb_ref[...]) * tb).astype(o_ref.dtype)


def _branch(ya, yb, wa, wb, proj, *, col, tm, tn):
    m, k = ya.shape
    n = wa.shape[1]
    return pl.pallas_call(
        _branch_kernel,
        grid=(m // tm, n // tn),
        in_specs=[
            pl.BlockSpec((tm, k), lambda i, j: (i, 0)),
            pl.BlockSpec((tm, yb.shape[1]), lambda i, j: (i, 0)),
            pl.BlockSpec((k, tn), lambda i, j: (0, j)),
            pl.BlockSpec((yb.shape[1], tn), lambda i, j: (0, j)),
            pl.BlockSpec((tm, tn), lambda i, j: (i, col["g_a"] // tn + j)),
            pl.BlockSpec((tm, tn), lambda i, j: (i, col["g_b"] // tn + j)),
        ],
        out_specs=pl.BlockSpec((tm, tn), lambda i, j: (i, j)),
        out_shape=jax.ShapeDtypeStruct((m, n), BF16),
        compiler_params=_cparams("parallel", "parallel"),
        name="branch_mix",
    )(ya, yb, wa, wb, proj, proj)


def _x1_kernel(x_ref, mix_ref, w_ref, g_ref, b_ref, o_ref, ob_ref, *, alpha):
    t = alpha * x_ref[...] + _dot(mix_ref[...], w_ref[...])
    x1 = _layer_norm(t, g_ref[...], b_ref[...])
    o_ref[...] = x1
    ob_ref[...] = pltpu.bitcast(x1.astype(BF16), U32)


def _x1(x, mix, w, g, b, *, alpha, tm):
    m, d = x.shape
    return pl.pallas_call(
        functools.partial(_x1_kernel, alpha=alpha),
        grid=(m // tm,),
        in_specs=[
            pl.BlockSpec((tm, d), lambda i: (i, 0)),
            pl.BlockSpec((tm, d), lambda i: (i, 0)),
            pl.BlockSpec(w.shape, lambda i: (0, 0)),
            pl.BlockSpec(g.shape, lambda i: (0, 0)),
            pl.BlockSpec(b.shape, lambda i: (0, 0)),
        ],
        out_specs=[pl.BlockSpec((tm, d), lambda i: (i, 0)), pl.BlockSpec((_words(tm), d), lambda i: (i, 0))],
        out_shape=[jax.ShapeDtypeStruct((m, d), F32), jax.ShapeDtypeStruct((_words(m), d), U32)],
        compiler_params=_cparams("parallel"),
        name="x1_out_ln",
    )(x, mix, w, g, b)


def _top_rows(s, k, with_rank=False):
    rows = []
    cur = s
    rank = jnp.full(s.shape, float(k), F32)
    for a in range(k):
        m = jnp.max(cur, axis=0, keepdims=True)
        rows.append(m)
        hit = cur == m
        if with_rank:
            rank = jnp.where(hit, float(a), rank)
        cur = jnp.where(hit, -jnp.inf, cur)
    return (rows, rank) if with_rank else rows


def _route_kernel(x_ref, wq_ref, k1_ref, k2_ref, c1_ref, w1_ref, r2_ref, e2_ref, q_scr, s1_scr, s2_scr):
    n_heads = k1_ref.shape[0]
    dk = k1_ref.shape[2]
    tm = q_scr.shape[0]
    K = PEER_TOPK
    q_scr[...] = _dot(pltpu.bitcast(x_ref[...], BF16), wq_ref[...]).astype(BF16)
    for h in range(n_heads):
        q1 = q_scr[:, (2 * h) * dk:(2 * h + 1) * dk]
        q2 = q_scr[:, (2 * h + 1) * dk:(2 * h + 2) * dk]
        s1_scr[...] = lax.dot_general(k1_ref[h], q1, NT_DIMS, preferred_element_type=F32)
        s2_scr[...] = lax.dot_general(k2_ref[h], q2, NT_DIMS, preferred_element_type=F32)
        for c in range(tm // LANES):
            cs = slice(c * LANES, (c + 1) * LANES)
            s1 = s1_scr[:, cs]
            s2 = s2_scr[:, cs]
            t1 = jnp.concatenate(_top_rows(s1, K), axis=0)
            rows2, rank2 = _top_rows(s2, K, with_rank=True)
            t2 = jnp.concatenate(rows2, axis=0)
            assert K == 2 * SUBLANES
            cand = jnp.concatenate(
                [t1[0:1] + t2] + [t1[a:a + 1] + t2[0:SUBLANES] for a in range(1, SUBLANES)]
                + [t1[SUBLANES:] + t2[0:1]], axis=0)
            thr = _top_rows(cand, K)[-1]
            m1 = t1[0:1]
            m2 = t2[0:1]
            zsum = jnp.sum(jnp.where(cand >= thr, jnp.exp(cand - (m1 + m2)), 0.0), axis=0, keepdims=True)
            cnt = jnp.zeros(s1.shape, F32)
            for b in range(K):
                cnt = cnt + jnp.where(s1 + t2[b:b + 1] >= thr, 1.0, 0.0)
            c1_ref[h, :, cs] = cnt
            w1_ref[h, :, cs] = jnp.exp(s1 - m1) / zsum
            r2_ref[h, :, cs] = pltpu.bitcast(rank2.astype(BF16), U32)
            e2_ref[h, :, cs] = pltpu.bitcast(jnp.exp(s2 - m2).astype(BF16), U32)


def _route(x1p, wq, k1, k2, *, m, tm):
    d = x1p.shape[1]
    n_heads, n_keys, _ = k1.shape
    tok = lambda rows: pl.BlockSpec((n_heads, rows, tm), lambda i: (0, 0, i))
    shp = lambda rows, dt: jax.ShapeDtypeStruct((n_heads, rows, m), dt)
    return pl.pallas_call(
        _route_kernel,
        grid=(m // tm,),
        in_specs=[
            pl.BlockSpec((_words(tm), d), lambda i: (i, 0)),
            pl.BlockSpec(wq.shape, lambda i: (0, 0)),
            pl.BlockSpec(k1.shape, lambda i: (0, 0, 0)),
            pl.BlockSpec(k2.shape, lambda i: (0, 0, 0)),
        ],
        out_specs=[tok(n_keys), tok(n_keys), tok(_words(n_keys)), tok(_words(n_keys))],
        out_shape=[shp(n_keys, F32), shp(n_keys, F32), shp(_words(n_keys), U32), shp(_words(n_keys), U32)],
        scratch_shapes=[pltpu.VMEM((tm, wq.shape[1]), BF16),
                        pltpu.VMEM((n_keys, tm), F32), pltpu.VMEM((n_keys, tm), F32)],
        compiler_params=_cparams("parallel"),
        name="peer_route",
    )(x1p, wq, k1, k2)


def _expert_kernel(x_ref, u_ref, vt_ref, c1_ref, w1_ref, r2_ref, e2_ref, o_ref, acc, coef, hid):
    e = pl.program_id(1)
    n_heads, per, tm = c1_ref.shape
    n_keys = hid.shape[0] // per
    half = _words(n_keys)

    @pl.when(e == 0)
    def _():
        acc[...] = jnp.zeros_like(acc)

    hid[...] = lax.dot_general(pltpu.bitcast(u_ref[...], BF16), pltpu.bitcast(x_ref[...], BF16), NT_DIMS,
                               preferred_element_type=F32)
    for k in range(per):
        for c in range(tm // LANES):
            cs = slice(c * LANES, (c + 1) * LANES)
            gate = jnp.zeros((n_keys, LANES), BF16)
            for h in range(n_heads):
                cnt = c1_ref[h, k:k + 1, cs].astype(BF16)
                w1 = w1_ref[h, k:k + 1, cs].astype(BF16)
                r2 = pltpu.bitcast(r2_ref[h, :, cs], BF16)
                e2 = pltpu.bitcast(e2_ref[h, :, cs], BF16)
                gate = gate + jnp.where(r2 < cnt, e2 * w1, jnp.zeros_like(gate))
            hk = hid[k * n_keys:(k + 1) * n_keys, cs]
            gelu = 0.5 * hk * (1.0 + lax.erf(hk * (1.0 / math.sqrt(2.0))))
            coef[k * half:(k + 1) * half, cs] = pltpu.bitcast(gate * gelu.astype(BF16), U32)

    acc[...] += _dot(pltpu.bitcast(vt_ref[...], BF16), pltpu.bitcast(coef[...], BF16))

    @pl.when(e == pl.num_programs(1) - 1)
    def _():
        o_ref[...] = acc[...].T.astype(o_ref.dtype)


def _experts(x1p, u, vt, c1, w1, r2, e2, *, m, tm, te):
    d = x1p.shape[1]
    n_tiles = u.shape[0] // _words(te)
    n_heads, n_keys, _ = c1.shape
    tok = lambda rows: pl.BlockSpec((n_heads, rows, tm), lambda i, e: (0, 0, i))
    per = te // n_keys
    assert per % SUBLANES == 0
    row = pl.BlockSpec((n_heads, per, tm), lambda i, e: (0, e, i))
    return pl.pallas_call(
        _expert_kernel,
        grid=(m // tm, n_tiles),
        in_specs=[
            pl.BlockSpec((_words(tm), d), lambda i, e: (i, 0)),
            pl.BlockSpec((_words(te), d), lambda i, e: (e, 0)),
            pl.BlockSpec((_words(d), te), lambda i, e: (0, e)),
            row, row, tok(_words(n_keys)), tok(_words(n_keys)),
        ],
        out_specs=pl.BlockSpec((tm, d), lambda i, e: (i, 0)),
        out_shape=jax.ShapeDtypeStruct((m, d), BF16),
        scratch_shapes=[pltpu.VMEM((d, tm), F32),
                        pltpu.VMEM((_words(te), tm), U32),
                        pltpu.VMEM((te, tm), F32)],
        compiler_params=_cparams("parallel", "arbitrary"),
        name="peer_experts",
    )(x1p, u, vt, c1, w1, r2, e2)


def _final_kernel(x1_ref, peer_ref, p_ref, wg_ref, wp_ref, g_ref, b_ref, o_ref, *, alpha):
    x2 = _layer_norm(alpha * x1_ref[...] + peer_ref[...], g_ref[...], b_ref[...])
    gate = _sigmoid(_dot(x2.astype(BF16), wg_ref[...]))
    y = x2 + gate * _dot(p_ref[...].astype(BF16), wp_ref[...])
    if len(o_ref.shape) == 3:
        o_ref[:, 0, :] = y
    else:
        o_ref[...] = y


def _final(x1, peer, p, wg, wp, g, b, *, alpha, row0, tm, per_step):
    d = x1.shape[1]
    n = p.shape[0]
    r0 = row0 // tm
    rows = lambda i: (r0 + i, 0)
    const = lambda i: (0, 0)
    if per_step:
        out_shape, out_spec = (n, 1, d), pl.BlockSpec((tm, 1, d), lambda i: (i, 0, 0))
    else:
        out_shape, out_spec = (n, d), pl.BlockSpec((tm, d), lambda i: (i, 0))
    return pl.pallas_call(
        functools.partial(_final_kernel, alpha=alpha),
        grid=(n // tm,),
        in_specs=[
            pl.BlockSpec((tm, d), rows),
            pl.BlockSpec((tm, d), rows),
            pl.BlockSpec((tm, p.shape[1]), lambda i: (i, 0)),
            pl.BlockSpec(wg.shape, const),
            pl.BlockSpec(wp.shape, const),
            pl.BlockSpec(g.shape, const),
            pl.BlockSpec(b.shape, const),
        ],
        out_specs=out_spec,
        out_shape=jax.ShapeDtypeStruct(out_shape, F32),
        compiler_params=_cparams("parallel"),
        name="ln2_ple",
    )(x1, peer, p, wg, wp, g, b)


TOKEN_TILE = 640
ROW_TILE = 320
FINAL_TILE = 512
EXPERT_TILE = 1024
PEER_TOKEN_TILE = 768


def _pack_kernel(w_ref, o_ref, *, transpose):
    w = w_ref[...]
    o_ref[...] = pltpu.bitcast((w.T if transpose else w).astype(BF16), U32)


def _pack_rows(w, transpose=False, tr=512):
    r, c = w.shape
    if transpose:
        out_shape, out_spec = (_words(c), r), pl.BlockSpec((_words(c), tr), lambda i: (0, i))
    else:
        out_shape, out_spec = (_words(r), c), pl.BlockSpec((_words(tr), c), lambda i: (i, 0))
    return pl.pallas_call(
        functools.partial(_pack_kernel, transpose=transpose),
        grid=(r // tr,),
        in_specs=[pl.BlockSpec((tr, c), lambda i: (i, 0))],
        out_specs=out_spec,
        out_shape=jax.ShapeDtypeStruct(out_shape, U32),
        compiler_params=_cparams("parallel"),
        name="pack_weight",
    )(w)


def _pad_lanes(v):
    return jnp.pad(v.astype(F32), (0, LANES - v.shape[0])).reshape(1, LANES)


def _layer(x, p_p, p_s, ssm_h, conv_buf, sc_buf, n_prompt, batch, seq, depth,
           w_in, ssd_conv_w, ssd_conv_b, ssd_dt_bias, ssd_a_log, ssd_d, ssd_norm_w,
           sc_conv_w, w_branch_ssd, w_branch_sc, w_out, ln1_g, ln1_b,
           peer_wq, peer_keys1, peer_keys2, peer_u, peer_v, ln2_g, ln2_b,
           ple_gate_w, ple_proj_w):
    m, d = x.shape
    nb = m - n_prompt
    n_heads = ssd_dt_bias.shape[0]
    hw, n_state = ssm_h.shape[1] * ssm_h.shape[2], ssm_h.shape[3]
    head_dim = ssm_h.shape[2]
    conv_dim = ssd_conv_w.shape[1]
    gn = (conv_dim - hw) // 2
    n_groups = gn // n_state
    scw = sc_conv_w.shape[1]
    alpha = (2.0 * depth) ** 0.25
    assert hw == scw == d and n_heads <= LANES and LANES % head_dim == 0 and n_state == LANES

    o_dt = hw + conv_dim
    o_scb = o_dt + n_heads
    col = {"z": 0, "xs": hw, "B": 2 * hw, "C": 2 * hw + gn}
    col2 = {"sc_b": 0, "sc_c": scw, "sc_h": 2 * scw, "g_a": 3 * scw, "g_b": 3 * scw + d}
    tm = _tile(m, TOKEN_TILE, LANES)
    w_in_t = w_in.T
    proj = _in_proj(x, w_in_t, col0=0, n=o_dt, tm=tm)
    proj2 = _in_proj(x, w_in_t, col0=o_scb, n=3 * scw + 2 * d, tm=tm)
    dtraw = _in_proj(x, w_in_t, col0=o_dt, n=LANES, tm=tm)

    ch_head = jnp.arange(hw) // head_dim
    e1 = (jnp.arange(LANES)[:, None] == ch_head[None, :]).astype(BF16)
    e2 = (jnp.arange(LANES)[:, None] == (jnp.arange(n_heads * LANES) // LANES)[None, :]).astype(BF16)
    convb = ssd_conv_b.reshape(1, conv_dim)
    dtb, alog = _pad_lanes(ssd_dt_bias), _pad_lanes(ssd_a_log)
    de = jnp.repeat(ssd_d.astype(F32), head_dim).reshape(1, hw)
    normw = ssd_norm_w.reshape(1, hw)
    shp = dict(hw=hw, gn=gn, n_groups=n_groups, n_state=n_state, col=col)
    ya_p, h_p = _ssd_prompt(proj, dtraw, ssd_conv_w, convb, dtb, alog, de, normw, e1, e2,
                            batch=batch, seq=seq, head_dim=head_dim, **shp)
    cst = jnp.transpose(conv_buf, (1, 0, 2))
    ya, h_s = _ssd_sample(proj, dtraw, cst, ssm_h.reshape(nb, hw, n_state), ssd_conv_w, convb,
                          dtb, alog, de, normw, e1, ya_p, row0=n_prompt, nb=nb, **shp)

    yb_p, sc_tail = _sc_prompt(proj2, sc_conv_w, batch=batch, seq=seq, width=scw, col=col2,
                               ts=min(seq, 256))
    yb, ch_s = _sc_sample(proj2, jnp.transpose(sc_buf, (1, 0, 2)), sc_conv_w, yb_p,
                          row0=n_prompt, nb=nb, width=scw, col=col2)

    mix = _branch(ya, yb, w_branch_ssd.astype(BF16), w_branch_sc.astype(BF16), proj2,
                  col=col2, tm=tm, tn=_tile(d, 512, LANES))
    tm2 = _tile(m, ROW_TILE, SUBLANES)
    x1, x1b = _x1(x, mix, w_out.astype(BF16), ln1_g.reshape(1, d), ln1_b.reshape(1, d),
                  alpha=alpha, tm=tm2)

    tmp = PEER_TOKEN_TILE
    mp = m + (-m % tmp)
    x1p = jnp.pad(x1b, ((0, _words(mp - m)), (0, 0)))
    route = _route(x1p, peer_wq.astype(BF16), peer_keys1.astype(BF16), peer_keys2.astype(BF16), m=mp, tm=tmp)
    peer = _experts(x1p, _pack_rows(peer_u), _pack_rows(peer_v, transpose=True), *route,
                    m=mp, tm=tmp, te=EXPERT_TILE)

    fin = functools.partial(_final, x1, peer, wg=ple_gate_w.astype(BF16), wp=ple_proj_w.astype(BF16),
                            g=ln2_g.reshape(1, d), b=ln2_b.reshape(1, d), alpha=alpha)
    assert n_prompt % nb == 0
    y_p = fin(p=p_p, row0=0, tm=_tile(n_prompt, FINAL_TILE, SUBLANES), per_step=False)
    y_s = fin(p=p_s, row0=n_prompt, tm=nb, per_step=True)

    k_ssd = ssd_conv_w.shape[0] - 1
    k_sc = sc_conv_w.shape[0] - 1
    xbc_cols = lambda rows: rows[..., hw:hw + conv_dim]
    conv_p = xbc_cols(jnp.stack([proj[(b + 1) * seq - k_ssd:(b + 1) * seq] for b in range(batch)]))
    conv_s = jnp.concatenate([conv_buf[:, 1:, :], xbc_cols(proj[n_prompt:])[:, None, :]], axis=1)
    sc_p = sc_tail[:, SUBLANES - k_sc:, :]
    sc_s = jnp.concatenate([sc_buf[:, 1:, :], ch_s[:, None, :]], axis=1)
    hshape = (-1, n_heads, head_dim, n_state)
    return y_p, y_s, conv_p, h_p.reshape(hshape), sc_p, conv_s, h_s.reshape(hshape), sc_s


def kernel(x_prompt, x_sample, p_prompt, p_sample, state_ssm, state_ssd_conv, state_shortconv, w_in, ssd_conv_w, ssd_conv_b, ssd_dt_bias, ssd_a_log, ssd_d, ssd_norm_w, sc_conv_w, w_branch_ssd, w_branch_sc, w_out, ln1_g, ln1_b, peer_wq, peer_keys1, peer_keys2, peer_u, peer_v, ln2_g, ln2_b, ple_gate_w, ple_proj_w):
    batch, seq, d = x_prompt.shape
    nb, dec_seq, _ = x_sample.shape
    assert dec_seq == 1 and seq % SSD_CHUNK == 0
    depth = w_in.shape[0]
    n_prompt = batch * seq
    x = jnp.concatenate([x_prompt.reshape(n_prompt, d), x_sample.reshape(nb, d)], axis=0)
    weights = (w_in, ssd_conv_w, ssd_conv_b, ssd_dt_bias, ssd_a_log, ssd_d, ssd_norm_w,
               sc_conv_w, w_branch_ssd, w_branch_sc, w_out, ln1_g, ln1_b,
               peer_wq, peer_keys1, peer_keys2, peer_u, peer_v, ln2_g, ln2_b,
               ple_gate_w, ple_proj_w)
    outs = [[] for _ in range(6)]
    for i in range(depth):
        y_p, y_s, conv_p, h_p, sc_p, conv_s, h_s, sc_s = _layer(
            x, p_prompt[i].reshape(n_prompt, -1), p_sample[i].reshape(nb, -1),
            state_ssm[i], state_ssd_conv[i], state_shortconv[i], n_prompt, batch, seq, depth,
            *[w[i] for w in weights])
        for lst, val in zip(outs, (h_p, conv_p, sc_p, h_s, conv_s, sc_s)):
            lst.append(val)
        if i + 1 < depth:
            x = jnp.concatenate([y_p, y_s.reshape(nb, d)], axis=0)
    y_prompt = y_p.reshape(batch, seq, d)
    y_sample = y_s
    return (y_prompt, y_sample) + tuple(jnp.stack(lst) for lst in outs)
```

```python
import functools
import math

import jax
import jax.numpy as jnp
from jax import lax
from jax.experimental import pallas as pl
from jax.experimental.pallas import tpu as pltpu

F32 = jnp.float32
BF16 = jnp.bfloat16
U32 = jnp.uint32

LANES = 128
SUBLANES = 8
BF16_ROWS = 16
PEER_TOPK = 16
SSD_CHUNK = 128
LN_EPS = 1e-5
RMS_EPS = 1e-5
VMEM_LIMIT = 56 * 1024 * 1024

NT_DIMS = (((1,), (1,)), ((), ()))
TN_DIMS = (((0,), (0,)), ((), ()))


def _cparams(*sem, flags=None):
    return pltpu.CompilerParams(dimension_semantics=sem, vmem_limit_bytes=VMEM_LIMIT, flags=flags)


def _dot(a, b):
    return jnp.dot(a, b, preferred_element_type=F32)


def _split3(v):
    hi = v.astype(BF16)
    r = v - hi.astype(F32)
    mid = r.astype(BF16)
    lo = (r - mid.astype(F32)).astype(BF16)
    return hi, mid, lo


def _dot3_lhs(v, rhs_bf16):
    hi, mid, lo = _split3(v)
    return _dot(hi, rhs_bf16) + _dot(mid, rhs_bf16) + _dot(lo, rhs_bf16)


def _dot3_rhs(lhs_bf16, v):
    hi, mid, lo = _split3(v)
    return _dot(lhs_bf16, hi) + _dot(lhs_bf16, mid) + _dot(lhs_bf16, lo)


def _words(rows):
    return rows * jnp.dtype(BF16).itemsize // jnp.dtype(U32).itemsize


def _sigmoid(x):
    return 1.0 / (1.0 + jnp.exp(-x))


def _silu(x):
    return x * _sigmoid(x)


def _softplus(x):
    return jnp.maximum(x, 0.0) + jnp.log1p(jnp.exp(-jnp.abs(x)))


def _layer_norm(x, g, b):
    mu = jnp.mean(x, axis=-1, keepdims=True)
    xc = x - mu
    var = jnp.mean(xc * xc, axis=-1, keepdims=True)
    return xc * lax.rsqrt(var + LN_EPS) * g + b


def _tile(m, cap, mult):
    best = None
    for t in range(mult, min(m, cap) + 1, mult):
        if m % t == 0:
            best = t
    assert best is not None, (m, cap, mult)
    return best


def _in_proj_kernel(x_ref, wa_ref, wb_ref, o_ref, w_scr, *, shift):
    tn = w_scr.shape[0]

    @pl.when(pl.program_id(1) == 0)
    def _():
        if shift:
            w = jnp.concatenate([wa_ref[...], wb_ref[...]], axis=0)[shift:shift + tn]
        else:
            w = wa_ref[...]
        w_scr[...] = w.astype(BF16)

    o_ref[...] = lax.dot_general(x_ref[...].astype(BF16), w_scr[...], NT_DIMS, preferred_element_type=F32)


def _in_proj(x, wt, *, col0, n, tm):
    m, k = x.shape
    shift = col0 % LANES
    assert shift % SUBLANES == 0
    base = col0 - shift
    tn = _tile(math.gcd(n, base) if base else n, 1024, LANES)
    last = pl.cdiv(wt.shape[0], LANES) - 1
    return pl.pallas_call(
        functools.partial(_in_proj_kernel, shift=shift),
        grid=(n // tn, m // tm),
        in_specs=[pl.BlockSpec((tm, k), lambda j, i: (i, 0)),
                  pl.BlockSpec((tn, k), lambda j, i: (base // tn + j, 0)),
                  pl.BlockSpec((LANES, k), lambda j, i: (jnp.minimum((base + (j + 1) * tn) // LANES, last), 0))],
        out_specs=pl.BlockSpec((tm, tn), lambda j, i: (i, j)),
        out_shape=jax.ShapeDtypeStruct((m, n), F32),
        scratch_shapes=[pltpu.VMEM((tn, k), BF16)],
        compiler_params=_cparams("parallel", "arbitrary"),
        name="in_proj",
    )(x, wt, wt)


def _ssd_gate_norm(y, z, normw_ref, out_ref, n_groups):
    y = y * _silu(z)
    gw = y.shape[1] // n_groups
    for g in range(n_groups):
        sl = slice(g * gw, (g + 1) * gw)
        yg = y[:, sl]
        ms = jnp.mean(yg * yg, axis=-1, keepdims=True)
        out_ref[:, sl] = (yg * lax.rsqrt(ms + RMS_EPS) * normw_ref[:, sl]).astype(out_ref.dtype)


def _ssd_prompt_kernel(z_ref, xs_ref, b_ref, c_ref, dt_ref, convw_ref, convb_ref, dtb_ref,
                       alog_ref, de_ref, normw_ref, e1_ref, e2_ref,
                       ya_ref, hout_ref,
                       cbuf, act, h_scr, x_scr, xd_scr, eae_scr, acsb_scr, acst_scr, y_scr,
                       *, n_groups, n_state, head_dim):
    c = pl.program_id(1)
    L = SSD_CHUNK
    hw = xs_ref.shape[1]
    gn = b_ref.shape[1]
    w_all = hw + 2 * gn
    n_heads = hw // head_dim
    hpl = LANES // head_dim
    n_blk = n_heads // hpl
    blk_per_group = n_blk // n_groups
    taps = convw_ref.shape[0]

    @pl.when(c == 0)
    def _():
        h_scr[...] = jnp.zeros_like(h_scr)
        cbuf[0:SUBLANES, :] = jnp.zeros((SUBLANES, w_all), F32)

    cbuf[SUBLANES:SUBLANES + L, 0:hw] = xs_ref[...]
    cbuf[SUBLANES:SUBLANES + L, hw:hw + gn] = b_ref[...]
    cbuf[SUBLANES:SUBLANES + L, hw + gn:] = c_ref[...]

    cw = math.gcd(w_all, 512)
    for blk in range(w_all // cw):
        sl = slice(blk * cw, (blk + 1) * cw)
        acc = convb_ref[:, sl] + convw_ref[taps - 1:taps, sl] * cbuf[SUBLANES:SUBLANES + L, sl]
        for j in range(1, taps):
            acc = acc + convw_ref[taps - 1 - j:taps - j, sl] * cbuf[SUBLANES - j:SUBLANES - j + L, sl]
        act[:, sl] = _silu(acc)
    cbuf[0:SUBLANES, :] = cbuf[L:L + SUBLANES, :]

    dt = _softplus(dt_ref[...] + dtb_ref[...])
    a_neg = -jnp.exp(alog_ref[...])
    dta = dt * a_neg
    ri = lax.broadcasted_iota(jnp.int32, (L, L), 0)
    ci = lax.broadcasted_iota(jnp.int32, (L, L), 1)
    causal = ri >= ci
    tri = jnp.where(causal, 1.0, 0.0).astype(BF16)
    acs = _dot3_rhs(tri, dta)
    e1 = e1_ref[...]
    dte = _dot3_lhs(dt, e1)
    acs_p = _split3(acs)
    acse = _dot(acs_p[0], e1) + _dot(acs_p[1], e1) + _dot(acs_p[2], e1)
    e2 = e2_ref[...]
    acsb_scr[...] = _dot(acs_p[0], e2) + _dot(acs_p[1], e2) + _dot(acs_p[2], e2)
    acst_scr[...] = acs.T

    xdt = act[:, 0:hw] * dte
    x_scr[...] = xdt.astype(BF16)
    xd_scr[...] = (xdt * jnp.exp(acse[L - 1:L, :] - acse)).astype(BF16)
    eae_scr[...] = jnp.exp(acse)

    lane = lax.broadcasted_iota(jnp.int32, (L, LANES), 1)
    cb = None
    for j in range(n_blk):
        g = j // blk_per_group
        bsl = slice(hw + g * n_state, hw + (g + 1) * n_state)
        csl = slice(hw + gn + g * n_state, hw + gn + (g + 1) * n_state)
        bg = act[:, bsl].astype(BF16)
        cg = act[:, csl].astype(BF16)
        if j % blk_per_group == 0:
            cb = lax.dot_general(cg, bg, NT_DIMS, preferred_element_type=F32)
        psl = slice(j * LANES, (j + 1) * LANES)
        xp = x_scr[:, psl]
        ydiag = None
        cds = []
        for q in range(hpl):
            r = j * hpl + q
            ab = acsb_scr[:, r * LANES:(r + 1) * LANES]
            at = jnp.broadcast_to(acst_scr[r:r + 1, :], (L, L))
            lm = jnp.where(causal, jnp.exp(ab - at), 0.0)
            m = (cb * lm).astype(BF16)
            inhead = (lane >= q * head_dim) & (lane < (q + 1) * head_dim)
            xq = jnp.where(inhead, xp, jnp.zeros_like(xp))
            yq = _dot(m, xq)
            ydiag = yq if ydiag is None else ydiag + yq
            cds.append(jnp.broadcast_to(jnp.exp(acsb_scr[L - 1:L, r * LANES:(r + 1) * LANES]),
                                        (head_dim, LANES)))
        cd = jnp.concatenate(cds, axis=0)
        hp = h_scr[psl, :]
        yoff = lax.dot_general(cg, hp.astype(BF16), NT_DIMS, preferred_element_type=F32)
        yoff = yoff * eae_scr[:, psl]
        st = lax.dot_general(xd_scr[:, psl], bg, TN_DIMS, preferred_element_type=F32)
        h_scr[psl, :] = hp * cd + st
        y_scr[:, psl] = ydiag + yoff + act[:, psl] * de_ref[:, psl]

    _ssd_gate_norm(y_scr[...], z_ref[...], normw_ref, ya_ref, n_groups)

    @pl.when(c == pl.num_programs(1) - 1)
    def _():
        hout_ref[0] = h_scr[...]


def _ssd_prompt(proj, dtraw, convw, convb, dtb, alog, de, normw, e1, e2, *, batch, seq,
                hw, gn, n_groups, n_state, head_dim, col):
    L = SSD_CHUNK
    nc = seq // L
    n_heads = hw // head_dim
    w_all = hw + 2 * gn
    row = lambda b, c: b * nc + c
    const = lambda b, c: (0, 0)
    kern = functools.partial(_ssd_prompt_kernel, n_groups=n_groups, n_state=n_state, head_dim=head_dim)
    return pl.pallas_call(
        kern,
        grid=(batch, nc),
        in_specs=[
            pl.BlockSpec((L, hw), lambda b, c: (row(b, c), col["z"] // hw)),
            pl.BlockSpec((L, hw), lambda b, c: (row(b, c), col["xs"] // hw)),
            pl.BlockSpec((L, gn), lambda b, c: (row(b, c), col["B"] // gn)),
            pl.BlockSpec((L, gn), lambda b, c: (row(b, c), col["C"] // gn)),
            pl.BlockSpec((L, LANES), lambda b, c: (row(b, c), 0)),
            pl.BlockSpec(convw.shape, const),
            pl.BlockSpec(convb.shape, const),
            pl.BlockSpec(dtb.shape, const),
            pl.BlockSpec(alog.shape, const),
            pl.BlockSpec(de.shape, const),
            pl.BlockSpec(normw.shape, const),
            pl.BlockSpec(e1.shape, const),
            pl.BlockSpec(e2.shape, const),
        ],
        out_specs=[
            pl.BlockSpec((L, hw), lambda b, c: (row(b, c), 0)),
            pl.BlockSpec((1, hw, n_state), lambda b, c: (b, 0, 0)),
        ],
        out_shape=[
            jax.ShapeDtypeStruct((proj.shape[0], hw), BF16),
            jax.ShapeDtypeStruct((batch, hw, n_state), F32),
        ],
        scratch_shapes=[
            pltpu.VMEM((L + SUBLANES, w_all), F32),
            pltpu.VMEM((L, w_all), F32),
            pltpu.VMEM((hw, n_state), F32),
            pltpu.VMEM((L, hw), BF16),
            pltpu.VMEM((L, hw), BF16),
            pltpu.VMEM((L, hw), F32),
            pltpu.VMEM((L, n_heads * LANES), F32),
            pltpu.VMEM((L, L), F32),
            pltpu.VMEM((L, hw), F32),
        ],
        compiler_params=_cparams("parallel", "arbitrary"),
        name="ssd_prompt",
    )(proj, proj, proj, proj, dtraw, convw, convb, dtb, alog, de, normw, e1, e2)


def _ssd_sample_kernel(z_ref, xs_ref, b_ref, c_ref, dt_ref, cst_ref, h_ref, convw_ref, convb_ref,
                       dtb_ref, alog_ref, de_ref, normw_ref, e1_ref, ya_all_ref,
                       ya_ref, hout_ref, xbc, y_scr,
                       *, n_groups, n_state):
    bb = xs_ref.shape[0]
    hw = xs_ref.shape[1]
    gn = b_ref.shape[1]
    taps = convw_ref.shape[0]
    gw = hw // n_groups

    xbc[:, 0:hw] = xs_ref[...]
    xbc[:, hw:hw + gn] = b_ref[...]
    xbc[:, hw + gn:] = c_ref[...]
    acc = convb_ref[...] + convw_ref[taps - 1:taps, :] * xbc[...]
    for j in range(taps - 1):
        acc = acc + convw_ref[j:j + 1, :] * cst_ref[j]
    act = _silu(acc)
    xs = act[:, 0:hw]

    dt = _softplus(dt_ref[...] + dtb_ref[...])
    dec = jnp.exp(dt * (-jnp.exp(alog_ref[...])))
    e1 = e1_ref[...]
    dte = _dot3_lhs(dt, e1)
    dece = _dot3_lhs(dec, e1)
    xdt = xs * dte

    pieces = [p.astype(F32) for p in _split3(dece)] + [p.astype(F32) for p in _split3(xdt)]
    npc = len(pieces)
    stack = jnp.concatenate(pieces + [jnp.zeros((LANES - npc * bb, hw), F32)], axis=0)
    lt = stack.T.astype(BF16)

    krow = lax.broadcasted_iota(jnp.int32, (LANES, LANES), 0)
    rowid = lax.broadcasted_iota(jnp.int32, (bb, gw), 0)
    half = npc // 2
    y_scr[...] = jnp.zeros_like(y_scr)
    for s in range(bb):
        is_s = (krow % bb) == s
        sel_dec = jnp.where(is_s & (krow < half * bb), 1.0, 0.0).astype(BF16)
        sel_x = jnp.where(is_s & (krow >= half * bb) & (krow < npc * bb), 1.0, 0.0).astype(BF16)
        dec_b = _dot(lt, sel_dec)
        x_b = _dot(lt, sel_x)
        for g in range(n_groups):
            rows = slice(g * gw, (g + 1) * gw)
            brow = act[s:s + 1, hw + g * n_state:hw + (g + 1) * n_state]
            hn = h_ref[s, rows, :] * dec_b[rows, :] + x_b[rows, :] * brow
            hout_ref[s, rows, :] = hn
            cg = act[:, hw + gn + g * n_state:hw + gn + (g + 1) * n_state].astype(BF16)
            yg = lax.dot_general(cg, hn.astype(BF16), NT_DIMS, preferred_element_type=F32)
            y_scr[:, rows] = y_scr[:, rows] + jnp.where(rowid == s, yg, 0.0)

    y = y_scr[...] + xs * de_ref[...]
    _ssd_gate_norm(y, z_ref[...], normw_ref, ya_ref, n_groups)


def _ssd_sample(proj, dtraw, cst, h0, convw, convb, dtb, alog, de, normw, e1, ya_all, *, row0, nb,
                hw, gn, n_groups, n_state, col):
    bb = SUBLANES
    w_all = hw + 2 * gn
    r0 = row0 // bb
    const = lambda i: (0, 0)
    kern = functools.partial(_ssd_sample_kernel, n_groups=n_groups, n_state=n_state)
    return pl.pallas_call(
        kern,
        grid=(nb // bb,),
        in_specs=[
            pl.BlockSpec((bb, hw), lambda i: (r0 + i, col["z"] // hw)),
            pl.BlockSpec((bb, hw), lambda i: (r0 + i, col["xs"] // hw)),
            pl.BlockSpec((bb, gn), lambda i: (r0 + i, col["B"] // gn)),
            pl.BlockSpec((bb, gn), lambda i: (r0 + i, col["C"] // gn)),
            pl.BlockSpec((bb, LANES), lambda i: (r0 + i, 0)),
            pl.BlockSpec((cst.shape[0], bb, w_all), lambda i: (0, i, 0)),
            pl.BlockSpec((bb, hw, n_state), lambda i: (i, 0, 0)),
            pl.BlockSpec(convw.shape, const),
            pl.BlockSpec(convb.shape, const),
            pl.BlockSpec(dtb.shape, const),
            pl.BlockSpec(alog.shape, const),
            pl.BlockSpec(de.shape, const),
            pl.BlockSpec(normw.shape, const),
            pl.BlockSpec(e1.shape, const),
            pl.BlockSpec(memory_space=pl.ANY),
        ],
        out_specs=[
            pl.BlockSpec((bb, hw), lambda i: (r0 + i, 0)),
            pl.BlockSpec((bb, hw, n_state), lambda i: (i, 0, 0)),
        ],
        out_shape=[
            jax.ShapeDtypeStruct(ya_all.shape, BF16),
            jax.ShapeDtypeStruct((nb, hw, n_state), F32),
        ],
        scratch_shapes=[pltpu.VMEM((bb, w_all), F32), pltpu.VMEM((bb, hw), F32)],
        compiler_params=_cparams("parallel"),
        input_output_aliases={14: 0},
        name="ssd_sample",
    )(proj, proj, proj, proj, dtraw, cst, h0, convw, convb, dtb, alog, de, normw, e1, ya_all)


def _sc_prompt_kernel(b_ref, c_ref, h_ref, w_ref, yb_ref, tail_ref, cbuf):
    j = pl.program_id(1)
    ts = b_ref.shape[0]
    taps = w_ref.shape[0]

    @pl.when(j == 0)
    def _():
        cbuf[0:SUBLANES, :] = jnp.zeros((SUBLANES, cbuf.shape[1]), F32)

    cbuf[SUBLANES:SUBLANES + ts, :] = c_ref[...] * h_ref[...]
    u = w_ref[taps - 1:taps, :] * cbuf[SUBLANES:SUBLANES + ts, :]
    for k in range(1, taps):
        u = u + w_ref[taps - 1 - k:taps - k, :] * cbuf[SUBLANES - k:SUBLANES - k + ts, :]
    yb_ref[...] = (b_ref[...] * u).astype(yb_ref.dtype)
    cbuf[0:SUBLANES, :] = cbuf[ts:ts + SUBLANES, :]

    @pl.when(j == pl.num_programs(1) - 1)
    def _():
        tail_ref[0] = cbuf[0:SUBLANES, :]


def _sc_prompt(proj, w, *, batch, seq, width, col, ts):
    nt = seq // ts
    row = lambda b, j: b * nt + j
    return pl.pallas_call(
        _sc_prompt_kernel,
        grid=(batch, nt),
        in_specs=[
            pl.BlockSpec((ts, width), lambda b, j: (row(b, j), col["sc_b"] // width)),
            pl.BlockSpec((ts, width), lambda b, j: (row(b, j), col["sc_c"] // width)),
            pl.BlockSpec((ts, width), lambda b, j: (row(b, j), col["sc_h"] // width)),
            pl.BlockSpec(w.shape, lambda b, j: (0, 0)),
        ],
        out_specs=[
            pl.BlockSpec((ts, width), lambda b, j: (row(b, j), 0)),
            pl.BlockSpec((1, SUBLANES, width), lambda b, j: (b, 0, 0)),
        ],
        out_shape=[
            jax.ShapeDtypeStruct((proj.shape[0], width), BF16),
            jax.ShapeDtypeStruct((batch, SUBLANES, width), F32),
        ],
        scratch_shapes=[pltpu.VMEM((ts + SUBLANES, width), F32)],
        compiler_params=_cparams("parallel", "arbitrary"),
        name="shortconv_prompt",
    )(proj, proj, proj, w)


def _sc_sample_kernel(b_ref, c_ref, h_ref, st_ref, w_ref, yb_all_ref, yb_ref, ch_ref):
    taps = w_ref.shape[0]
    ch = c_ref[...] * h_ref[...]
    u = w_ref[taps - 1:taps, :] * ch
    for k in range(taps - 1):
        u = u + w_ref[k:k + 1, :] * st_ref[k]
    yb_ref[...] = (b_ref[...] * u).astype(yb_ref.dtype)
    ch_ref[...] = ch


def _sc_sample(proj, st, w, yb_all, *, row0, nb, width, col):
    r0 = row0 // nb
    return pl.pallas_call(
        _sc_sample_kernel,
        grid=(1,),
        in_specs=[
            pl.BlockSpec((nb, width), lambda i: (r0, col["sc_b"] // width)),
            pl.BlockSpec((nb, width), lambda i: (r0, col["sc_c"] // width)),
            pl.BlockSpec((nb, width), lambda i: (r0, col["sc_h"] // width)),
            pl.BlockSpec(st.shape, lambda i: (0, 0, 0)),
            pl.BlockSpec(w.shape, lambda i: (0, 0)),
            pl.BlockSpec(memory_space=pl.ANY),
        ],
        out_specs=[pl.BlockSpec((nb, width), lambda i: (r0, 0)),
                   pl.BlockSpec((nb, width), lambda i: (0, 0))],
        out_shape=[jax.ShapeDtypeStruct(yb_all.shape, BF16),
                   jax.ShapeDtypeStruct((nb, width), F32)],
        input_output_aliases={5: 0},
        compiler_params=_cparams("arbitrary"),
        name="shortconv_sample",
    )(proj, proj, proj, st, w, yb_all)


def _branch_kernel(ya_ref, yb_ref, wa_ref, wb_ref, ga_ref, gb_ref, o_ref):
    ta = _dot(ya_ref[...], wa_ref[...])
    tb = _dot(yb_ref[...], wb_ref[...])
    o_ref[...] = (_sigmoid(ga_ref[...]) * ta + _sigmoid(gb_ref[...]) * tb).astype(o_ref.dtype)


def _branch(ya, yb, wa, wb, proj, *, col, tm, tn):
    m, k = ya.shape
    n = wa.shape[1]
    return pl.pallas_call(
        _branch_kernel,
        grid=(m // tm, n // tn),
        in_specs=[
            pl.BlockSpec((tm, k), lambda i, j: (i, 0)),
            pl.BlockSpec((tm, yb.shape[1]), lambda i, j: (i, 0)),
            pl.BlockSpec((k, tn), lambda i, j: (0, j)),
            pl.BlockSpec((yb.shape[1], tn), lambda i, j: (0, j)),
            pl.BlockSpec((tm, tn), lambda i, j: (i, col["g_a"] // tn + j)),
            pl.BlockSpec((tm, tn), lambda i, j: (i, col["g_b"] // tn + j)),
        ],
        out_specs=pl.BlockSpec((tm, tn), lambda i, j: (i, j)),
        out_shape=jax.ShapeDtypeStruct((m, n), BF16),
        compiler_params=_cparams("parallel", "parallel"),
        name="branch_mix",
    )(ya, yb, wa, wb, proj, proj)


def _x1_kernel(x_ref, mix_ref, w_ref, g_ref, b_ref, o_ref, ob_ref, *, alpha):
    t = alpha * x_ref[...] + _dot(mix_ref[...], w_ref[...])
    x1 = _layer_norm(t, g_ref[...], b_ref[...])
    o_ref[...] = x1
    ob_ref[...] = pltpu.bitcast(x1.astype(BF16), U32)


def _x1(x, mix, w, g, b, *, alpha, tm):
    m, d = x.shape
    return pl.pallas_call(
        functools.partial(_x1_kernel, alpha=alpha),
        grid=(m // tm,),
        in_specs=[
            pl.BlockSpec((tm, d), lambda i: (i, 0)),
            pl.BlockSpec((tm, d), lambda i: (i, 0)),
            pl.BlockSpec(w.shape, lambda i: (0, 0)),
            pl.BlockSpec(g.shape, lambda i: (0, 0)),
            pl.BlockSpec(b.shape, lambda i: (0, 0)),
        ],
        out_specs=[pl.BlockSpec((tm, d), lambda i: (i, 0)), pl.BlockSpec((_words(tm), d), lambda i: (i, 0))],
        out_shape=[jax.ShapeDtypeStruct((m, d), F32), jax.ShapeDtypeStruct((_words(m), d), U32)],
        compiler_params=_cparams("parallel"),
        name="x1_out_ln",
    )(x, mix, w, g, b)


def _sort_pairs(lo, hi):
    def merge(lo, hi, r):
        step = r * 2
        if step < hi - lo:
            yield from merge(lo, hi, step)
            yield from merge(lo + r, hi, step)
            yield from [(i, i + r) for i in range(lo + r, hi - r, step)]
        else:
            yield (lo, lo + r)
    if hi - lo >= 1:
        mid = lo + (hi - lo) // 2
        yield from _sort_pairs(lo, mid)
        yield from _sort_pairs(mid + 1, hi)
        yield from merge(lo, hi, 1)


def _exchange(v, i, j):
    v[i], v[j] = jnp.maximum(v[i], v[j]), jnp.minimum(v[i], v[j])


def _sublane_all(op, x):
    dist = SUBLANES // 2
    while dist:
        x = op(x, pltpu.roll(x, dist, 0))
        dist //= 2
    return x


def _top_sorted(s, k):
    assert s.shape[0] == k * SUBLANES and k & (k - 1) == 0
    v = [s[j * SUBLANES:(j + 1) * SUBLANES, :] for j in range(k)]
    for i, j in _sort_pairs(0, k - 1):
        _exchange(v, i, j)
    dist = SUBLANES // 2
    while dist:
        w = [pltpu.roll(x, dist, 0) for x in v]
        v = [jnp.maximum(v[j], w[k - 1 - j]) for j in range(k)]
        stride = k // 2
        while stride:
            for i in range(k):
                if not i & stride:
                    _exchange(v, i, i + stride)
            stride //= 2
        dist //= 2
    return v


def _prefix_count(pred, t):
    w = jnp.where
    m1 = pred(t[7])
    m2 = pred(w(m1, t[11], t[3]))
    m3 = pred(w(m1, w(m2, t[13], t[9]), w(m2, t[5], t[1])))
    m4 = pred(w(m1, w(m2, w(m3, t[14], t[12]), w(m3, t[10], t[8])),
                w(m2, w(m3, t[6], t[4]), w(m3, t[2], t[0]))))
    return (w(m1, 8.0, 0.0) + w(m2, 4.0, 0.0) + w(m3, 2.0, 0.0) + w(m4, 1.0, 0.0)
            + w(pred(t[15]), 1.0, 0.0))


def _route_kernel(x_ref, wq_ref, k1_ref, k2_ref, c1_ref, w1_ref, r2_ref, e2_ref, q_scr, s1_scr, s2_scr):
    n_heads, n_keys, dk = k1_ref.shape
    tm = q_scr.shape[0]
    K = PEER_TOPK
    S = SUBLANES
    assert K == 16 and n_keys == K * S
    sub = lax.broadcasted_iota(jnp.int32, (S, LANES), 0)
    q_scr[...] = _dot(pltpu.bitcast(x_ref[...], BF16), wq_ref[...]).astype(BF16)
    for h in range(n_heads):
        q1 = q_scr[:, (2 * h) * dk:(2 * h + 1) * dk]
        q2 = q_scr[:, (2 * h + 1) * dk:(2 * h + 2) * dk]
        s1_scr[...] = lax.dot_general(k1_ref[h], q1, NT_DIMS, preferred_element_type=F32)
        s2_scr[...] = lax.dot_general(k2_ref[h], q2, NT_DIMS, preferred_element_type=F32)
        for c in range(tm // LANES):
            cs = slice(c * LANES, (c + 1) * LANES)
            s1 = [s1_scr[j * S:(j + 1) * S, cs] for j in range(K)]
            s2 = [s2_scr[j * S:(j + 1) * S, cs] for j in range(K)]
            t1 = _top_sorted(s1_scr[:, cs], K)
            t2 = _top_sorted(s2_scr[:, cs], K)
            rows = lambda t: functools.reduce(lambda acc, b: jnp.where(sub == b, t[b], acc), range(1, S), t[0])
            t2lo, t2hi, t1hi = rows(t2[:S]), rows(t2[S:]), rows(t1[S:])
            cand = [t1[0] + t2lo, t1[0] + t2hi] + [t1[a] + t2lo for a in range(1, S)] + [t1hi + t2[0]]
            cur = cand
            for r in range(K):
                thr = _sublane_all(jnp.maximum, functools.reduce(jnp.maximum, cur))
                if r + 1 < K:
                    cur = [jnp.where(x == thr, -jnp.inf, x) for x in cur]
            top = t1[0] + t2[0]
            zsum = _sublane_all(jnp.add, sum(jnp.where(x >= thr, jnp.exp(x - top), 0.0) for x in cand))
            inv_z = 1.0 / zsum
            rank2 = []
            for j in range(K):
                rows8 = slice(j * S, (j + 1) * S)
                c1_ref[h, rows8, cs] = _prefix_count(lambda tv, x=s1[j]: x + tv >= thr, t2)
                w1_ref[h, rows8, cs] = jnp.exp(s1[j] - t1[0]) * inv_z
                rank2.append(_prefix_count(lambda tv, x=s2[j]: tv > x, t2))
            r2_ref[h, :, cs] = pltpu.bitcast(jnp.concatenate(rank2, axis=0).astype(BF16), U32)
            e2 = jnp.concatenate([jnp.exp(x - t2[0]) for x in s2], axis=0)
            e2_ref[h, :, cs] = pltpu.bitcast(e2.astype(BF16), U32)


def _route(x1p, wq, k1, k2, *, m, tm):
    d = x1p.shape[1]
    n_heads, n_keys, _ = k1.shape
    tok = lambda rows: pl.BlockSpec((n_heads, rows, tm), lambda i: (0, 0, i))
    shp = lambda rows, dt: jax.ShapeDtypeStruct((n_heads, rows, m), dt)
    return pl.pallas_call(
        _route_kernel,
        grid=(m // tm,),
        in_specs=[
            pl.BlockSpec((_words(tm), d), lambda i: (i, 0)),
            pl.BlockSpec(wq.shape, lambda i: (0, 0)),
            pl.BlockSpec(k1.shape, lambda i: (0, 0, 0)),
            pl.BlockSpec(k2.shape, lambda i: (0, 0, 0)),
        ],
        out_specs=[tok(n_keys), tok(n_keys), tok(_words(n_keys)), tok(_words(n_keys))],
        out_shape=[shp(n_keys, F32), shp(n_keys, F32), shp(_words(n_keys), U32), shp(_words(n_keys), U32)],
        scratch_shapes=[pltpu.VMEM((tm, wq.shape[1]), BF16),
                        pltpu.VMEM((n_keys, tm), F32), pltpu.VMEM((n_keys, tm), F32)],
        compiler_params=_cparams("parallel"),
        name="peer_route",
    )(x1p, wq, k1, k2)


def _expert_kernel(x_ref, u_ref, vt_ref, c1_ref, w1_ref, r2_ref, e2_ref, o_ref, acc, coef, hid):
    e = pl.program_id(1)
    n_heads, per, tm = c1_ref.shape
    n_keys = hid.shape[0] // per
    half = _words(n_keys)

    @pl.when(e == 0)
    def _():
        acc[...] = jnp.zeros_like(acc)

    hid[...] = lax.dot_general(pltpu.bitcast(u_ref[...], BF16), pltpu.bitcast(x_ref[...], BF16), NT_DIMS,
                               preferred_element_type=F32)
    for k in range(per):
        for c in range(tm // LANES):
            cs = slice(c * LANES, (c + 1) * LANES)
            gate = jnp.zeros((n_keys, LANES), BF16)
            for h in range(n_heads):
                cnt = c1_ref[h, k:k + 1, cs].astype(BF16)
                w1 = w1_ref[h, k:k + 1, cs].astype(BF16)
                r2 = pltpu.bitcast(r2_ref[h, :, cs], BF16)
                e2 = pltpu.bitcast(e2_ref[h, :, cs], BF16)
                gate = gate + jnp.where(r2 < cnt, e2 * w1, jnp.zeros_like(gate))
            hk = hid[k * n_keys:(k + 1) * n_keys, cs]
            gelu = 0.5 * hk * (1.0 + lax.erf(hk * (1.0 / math.sqrt(2.0))))
            coef[k * half:(k + 1) * half, cs] = pltpu.bitcast(gate * gelu.astype(BF16), U32)

    acc[...] += _dot(pltpu.bitcast(vt_ref[...], BF16), pltpu.bitcast(coef[...], BF16))

    @pl.when(e == pl.num_programs(1) - 1)
    def _():
        o_ref[...] = acc[...].T.astype(o_ref.dtype)


def _experts(x1p, u, vt, c1, w1, r2, e2, *, m, tm, te):
    d = x1p.shape[1]
    n_tiles = u.shape[0] // _words(te)
    n_heads, n_keys, _ = c1.shape
    tok = lambda rows: pl.BlockSpec((n_heads, rows, tm), lambda i, e: (0, 0, i))
    per = te // n_keys
    assert per % SUBLANES == 0
    row = pl.BlockSpec((n_heads, per, tm), lambda i, e: (0, e, i))
    return pl.pallas_call(
        _expert_kernel,
        grid=(m // tm, n_tiles),
        in_specs=[
            pl.BlockSpec((_words(tm), d), lambda i, e: (i, 0)),
            pl.BlockSpec((_words(te), d), lambda i, e: (e, 0)),
            pl.BlockSpec((_words(d), te), lambda i, e: (0, e)),
            row, row, tok(_words(n_keys)), tok(_words(n_keys)),
        ],
        out_specs=pl.BlockSpec((tm, d), lambda i, e: (i, 0)),
        out_shape=jax.ShapeDtypeStruct((m, d), BF16),
        scratch_shapes=[pltpu.VMEM((d, tm), F32),
                        pltpu.VMEM((_words(te), tm), U32),
                        pltpu.VMEM((te, tm), F32)],
        compiler_params=_cparams("parallel", "arbitrary"),
        name="peer_experts",
    )(x1p, u, vt, c1, w1, r2, e2)


def _final_kernel(x1_ref, peer_ref, p_ref, wg_ref, wp_ref, g_ref, b_ref, o_ref, *, alpha):
    x2 = _layer_norm(alpha * x1_ref[...] + peer_ref[...], g_ref[...], b_ref[...])
    gate = _sigmoid(_dot(x2.astype(BF16), wg_ref[...]))
    y = x2 + gate * _dot(p_ref[...].astype(BF16), wp_ref[...])
    if len(o_ref.shape) == 3:
        o_ref[:, 0, :] = y
    else:
        o_ref[...] = y


def _final(x1, peer, p, wg, wp, g, b, *, alpha, row0, tm, per_step):
    d = x1.shape[1]
    n = p.shape[0]
    r0 = row0 // tm
    rows = lambda i: (r0 + i, 0)
    const = lambda i: (0, 0)
    if per_step:
        out_shape, out_spec = (n, 1, d), pl.BlockSpec((tm, 1, d), lambda i: (i, 0, 0))
    else:
        out_shape, out_spec = (n, d), pl.BlockSpec((tm, d), lambda i: (i, 0))
    return pl.pallas_call(
        functools.partial(_final_kernel, alpha=alpha),
        grid=(n // tm,),
        in_specs=[
            pl.BlockSpec((tm, d), rows),
            pl.BlockSpec((tm, d), rows),
            pl.BlockSpec((tm, p.shape[1]), lambda i: (i, 0)),
            pl.BlockSpec(wg.shape, const),
            pl.BlockSpec(wp.shape, const),
            pl.BlockSpec(g.shape, const),
            pl.BlockSpec(b.shape, const),
        ],
        out_specs=out_spec,
        out_shape=jax.ShapeDtypeStruct(out_shape, F32),
        compiler_params=_cparams("parallel"),
        name="ln2_ple",
    )(x1, peer, p, wg, wp, g, b)


TOKEN_TILE = 640
ROW_TILE = 320
FINAL_TILE = 512
EXPERT_TILE = 1024
PEER_TOKEN_TILE = 768


def _pack_kernel(w_ref, o_ref, *, transpose):
    w = w_ref[...]
    o_ref[...] = pltpu.bitcast((w.T if transpose else w).astype(BF16), U32)


def _pack_rows(w, transpose=False, tr=512):
    r, c = w.shape
    if transpose:
        out_shape, out_spec = (_words(c), r), pl.BlockSpec((_words(c), tr), lambda i: (0, i))
    else:
        out_shape, out_spec = (_words(r), c), pl.BlockSpec((_words(tr), c), lambda i: (i, 0))
    return pl.pallas_call(
        functools.partial(_pack_kernel, transpose=transpose),
        grid=(r // tr,),
        in_specs=[pl.BlockSpec((tr, c), lambda i: (i, 0))],
        out_specs=out_spec,
        out_shape=jax.ShapeDtypeStruct(out_shape, U32),
        compiler_params=_cparams("parallel"),
        name="pack_weight",
    )(w)


def _pad_lanes(v):
    return jnp.pad(v.astype(F32), (0, LANES - v.shape[0])).reshape(1, LANES)


def _layer(x, p_p, p_s, ssm_h, conv_buf, sc_buf, n_prompt, batch, seq, depth,
           w_in, ssd_conv_w, ssd_conv_b, ssd_dt_bias, ssd_a_log, ssd_d, ssd_norm_w,
           sc_conv_w, w_branch_ssd, w_branch_sc, w_out, ln1_g, ln1_b,
           peer_wq, peer_keys1, peer_keys2, peer_u, peer_v, ln2_g, ln2_b,
           ple_gate_w, ple_proj_w):
    m, d = x.shape
    nb = m - n_prompt
    n_heads = ssd_dt_bias.shape[0]
    hw, n_state = ssm_h.shape[1] * ssm_h.shape[2], ssm_h.shape[3]
    head_dim = ssm_h.shape[2]
    conv_dim = ssd_conv_w.shape[1]
    gn = (conv_dim - hw) // 2
    n_groups = gn // n_state
    scw = sc_conv_w.shape[1]
    alpha = (2.0 * depth) ** 0.25
    assert hw == scw == d and n_heads <= LANES and LANES % head_dim == 0 and n_state == LANES

    o_dt = hw + conv_dim
    o_scb = o_dt + n_heads
    col = {"z": 0, "xs": hw, "B": 2 * hw, "C": 2 * hw + gn}
    col2 = {"sc_b": 0, "sc_c": scw, "sc_h": 2 * scw, "g_a": 3 * scw, "g_b": 3 * scw + d}
    tm = _tile(m, TOKEN_TILE, LANES)
    w_in_t = w_in.T
    proj = _in_proj(x, w_in_t, col0=0, n=o_dt, tm=tm)
    proj2 = _in_proj(x, w_in_t, col0=o_scb, n=3 * scw + 2 * d, tm=tm)
    dtraw = _in_proj(x, w_in_t, col0=o_dt, n=LANES, tm=tm)

    ch_head = jnp.arange(hw) // head_dim
    e1 = (jnp.arange(LANES)[:, None] == ch_head[None, :]).astype(BF16)
    e2 = (jnp.arange(LANES)[:, None] == (jnp.arange(n_heads * LANES) // LANES)[None, :]).astype(BF16)
    convb = ssd_conv_b.reshape(1, conv_dim)
    dtb, alog = _pad_lanes(ssd_dt_bias), _pad_lanes(ssd_a_log)
    de = jnp.repeat(ssd_d.astype(F32), head_dim).reshape(1, hw)
    normw = ssd_norm_w.reshape(1, hw)
    shp = dict(hw=hw, gn=gn, n_groups=n_groups, n_state=n_state, col=col)
    ya_p, h_p = _ssd_prompt(proj, dtraw, ssd_conv_w, convb, dtb, alog, de, normw, e1, e2,
                            batch=batch, seq=seq, head_dim=head_dim, **shp)
    cst = jnp.transpose(conv_buf, (1, 0, 2))
    ya, h_s = _ssd_sample(proj, dtraw, cst, ssm_h.reshape(nb, hw, n_state), ssd_conv_w, convb,
                          dtb, alog, de, normw, e1, ya_p, row0=n_prompt, nb=nb, **shp)

    yb_p, sc_tail = _sc_prompt(proj2, sc_conv_w, batch=batch, seq=seq, width=scw, col=col2,
                               ts=min(seq, 256))
    yb, ch_s = _sc_sample(proj2, jnp.transpose(sc_buf, (1, 0, 2)), sc_conv_w, yb_p,
                          row0=n_prompt, nb=nb, width=scw, col=col2)

    mix = _branch(ya, yb, w_branch_ssd.astype(BF16), w_branch_sc.astype(BF16), proj2,
                  col=col2, tm=tm, tn=_tile(d, 512, LANES))
    tm2 = _tile(m, ROW_TILE, SUBLANES)
    x1, x1b = _x1(x, mix, w_out.astype(BF16), ln1_g.reshape(1, d), ln1_b.reshape(1, d),
                  alpha=alpha, tm=tm2)

    tmp = PEER_TOKEN_TILE
    mp = m + (-m % tmp)
    x1p = jnp.pad(x1b, ((0, _words(mp - m)), (0, 0)))
    route = _route(x1p, peer_wq.astype(BF16), peer_keys1.astype(BF16), peer_keys2.astype(BF16), m=mp, tm=tmp)
    peer = _experts(x1p, _pack_rows(peer_u), _pack_rows(peer_v, transpose=True), *route,
                    m=mp, tm=tmp, te=EXPERT_TILE)

    fin = functools.partial(_final, x1, peer, wg=ple_gate_w.astype(BF16), wp=ple_proj_w.astype(BF16),
                            g=ln2_g.reshape(1, d), b=ln2_b.reshape(1, d), alpha=alpha)
    assert n_prompt % nb == 0
    y_p = fin(p=p_p, row0=0, tm=_tile(n_prompt, FINAL_TILE, SUBLANES), per_step=False)
    y_s = fin(p=p_s, row0=n_prompt, tm=nb, per_step=True)

    k_ssd = ssd_conv_w.shape[0] - 1
    k_sc = sc_conv_w.shape[0] - 1
    xbc_cols = lambda rows: rows[..., hw:hw + conv_dim]
    conv_p = xbc_cols(jnp.stack([proj[(b + 1) * seq - k_ssd:(b + 1) * seq] for b in range(batch)]))
    conv_s = jnp.concatenate([conv_buf[:, 1:, :], xbc_cols(proj[n_prompt:])[:, None, :]], axis=1)
    sc_p = sc_tail[:, SUBLANES - k_sc:, :]
    sc_s = jnp.concatenate([sc_buf[:, 1:, :], ch_s[:, None, :]], axis=1)
    hshape = (-1, n_heads, head_dim, n_state)
    return y_p, y_s, conv_p, h_p.reshape(hshape), sc_p, conv_s, h_s.reshape(hshape), sc_s


def kernel(x_prompt, x_sample, p_prompt, p_sample, state_ssm, state_ssd_conv, state_shortconv, w_in, ssd_conv_w, ssd_conv_b, ssd_dt_bias, ssd_a_log, ssd_d, ssd_norm_w, sc_conv_w, w_branch_ssd, w_branch_sc, w_out, ln1_g, ln1_b, peer_wq, peer_keys1, peer_keys2, peer_u, peer_v, ln2_g, ln2_b, ple_gate_w, ple_proj_w):
    batch, seq, d = x_prompt.shape
    nb, dec_seq, _ = x_sample.shape
    assert dec_seq == 1 and seq % SSD_CHUNK == 0
    depth = w_in.shape[0]
    n_prompt = batch * seq
    x = jnp.concatenate([x_prompt.reshape(n_prompt, d), x_sample.reshape(nb, d)], axis=0)
    weights = (w_in, ssd_conv_w, ssd_conv_b, ssd_dt_bias, ssd_a_log, ssd_d, ssd_norm_w,
               sc_conv_w, w_branch_ssd, w_branch_sc, w_out, ln1_g, ln1_b,
               peer_wq, peer_keys1, peer_keys2, peer_u, peer_v, ln2_g, ln2_b,
               ple_gate_w, ple_proj_w)
    outs = [[] for _ in range(6)]
    for i in range(depth):
        y_p, y_s, conv_p, h_p, sc_p, conv_s, h_s, sc_s = _layer(
            x, p_prompt[i].reshape(n_prompt, -1), p_sample[i].reshape(nb, -1),
            state_ssm[i], state_ssd_conv[i], state_shortconv[i], n_prompt, batch, seq, depth,
            *[w[i] for w in weights])
        for lst, val in zip(outs, (h_p, conv_p, sc_p, h_s, conv_s, sc_s)):
            lst.append(val)
        if i + 1 < depth:
            x = jnp.concatenate([y_p, y_s.reshape(nb, d)], axis=0)
    y_prompt = y_p.reshape(batch, seq, d)
    y_sample = y_s
    return (y_prompt, y_sample) + tuple(jnp.stack(lst) for lst in outs)
```

```python
import functools
import math

import jax
import jax.numpy as jnp
from jax import lax
from jax.experimental import pallas as pl
from jax.experimental.pallas import tpu as pltpu

F32 = jnp.float32
BF16 = jnp.bfloat16
U32 = jnp.uint32

LANES = 128
SUBLANES = 8
BF16_ROWS = 16
PEER_TOPK = 16
SSD_CHUNK = 128
LN_EPS = 1e-5
RMS_EPS = 1e-5
VMEM_LIMIT = 56 * 1024 * 1024

NT_DIMS = (((1,), (1,)), ((), ()))
TN_DIMS = (((0,), (0,)), ((), ()))


def _cparams(*sem, flags=None):
    return pltpu.CompilerParams(dimension_semantics=sem, vmem_limit_bytes=VMEM_LIMIT, flags=flags)


def _dot(a, b):
    return jnp.dot(a, b, preferred_element_type=F32)


def _split3(v):
    hi = v.astype(BF16)
    r = v - hi.astype(F32)
    mid = r.astype(BF16)
    lo = (r - mid.astype(F32)).astype(BF16)
    return hi, mid, lo


def _dot3_lhs(v, rhs_bf16):
    hi, mid, lo = _split3(v)
    return _dot(hi, rhs_bf16) + _dot(mid, rhs_bf16) + _dot(lo, rhs_bf16)


def _dot3_rhs(lhs_bf16, v):
    hi, mid, lo = _split3(v)
    return _dot(lhs_bf16, hi) + _dot(lhs_bf16, mid) + _dot(lhs_bf16, lo)


def _words(rows):
    return rows * jnp.dtype(BF16).itemsize // jnp.dtype(U32).itemsize


def _sigmoid(x):
    return 1.0 / (1.0 + jnp.exp(-x))


def _silu(x):
    return x * _sigmoid(x)


def _softplus(x):
    return jnp.maximum(x, 0.0) + jnp.log1p(jnp.exp(-jnp.abs(x)))


def _layer_norm(x, g, b):
    mu = jnp.mean(x, axis=-1, keepdims=True)
    xc = x - mu
    var = jnp.mean(xc * xc, axis=-1, keepdims=True)
    return xc * lax.rsqrt(var + LN_EPS) * g + b


def _tile(m, cap, mult):
    best = None
    for t in range(mult, min(m, cap) + 1, mult):
        if m % t == 0:
            best = t
    assert best is not None, (m, cap, mult)
    return best


def _in_proj_kernel(x_ref, wa_ref, wb_ref, o_ref, w_scr, *, shift):
    tn = w_scr.shape[0]

    @pl.when(pl.program_id(1) == 0)
    def _():
        if shift:
            w = jnp.concatenate([wa_ref[...], wb_ref[...]], axis=0)[shift:shift + tn]
        else:
            w = wa_ref[...]
        w_scr[...] = w.astype(BF16)

    o_ref[...] = lax.dot_general(x_ref[...], w_scr[...], NT_DIMS, preferred_element_type=F32)


def _in_proj(x, wt, *, col0, n, tm):
    m, k = x.shape
    shift = col0 % LANES
    assert shift % SUBLANES == 0
    base = col0 - shift
    tn = _tile(math.gcd(n, base) if base else n, PROJ_COLS, LANES)
    last = pl.cdiv(wt.shape[0], LANES) - 1
    return pl.pallas_call(
        functools.partial(_in_proj_kernel, shift=shift),
        grid=(n // tn, m // tm),
        in_specs=[pl.BlockSpec((tm, k), lambda j, i: (i, 0)),
                  pl.BlockSpec((tn, k), lambda j, i: (base // tn + j, 0)),
                  pl.BlockSpec((LANES, k), lambda j, i: (jnp.minimum((base + (j + 1) * tn) // LANES, last), 0))],
        out_specs=pl.BlockSpec((tm, tn), lambda j, i: (i, j)),
        out_shape=jax.ShapeDtypeStruct((m, n), F32),
        scratch_shapes=[pltpu.VMEM((tn, k), BF16)],
        compiler_params=_cparams("parallel", "arbitrary"),
        name="in_proj",
    )(x, wt, wt)


def _ssd_gate_norm(y, z, normw_ref, out_ref, n_groups):
    y = y * _silu(z)
    gw = y.shape[1] // n_groups
    for g in range(n_groups):
        sl = slice(g * gw, (g + 1) * gw)
        yg = y[:, sl]
        ms = jnp.mean(yg * yg, axis=-1, keepdims=True)
        out_ref[:, sl] = (yg * lax.rsqrt(ms + RMS_EPS) * normw_ref[:, sl]).astype(out_ref.dtype)


def _ssd_prompt_kernel(z_ref, xs_ref, b_ref, c_ref, dt_ref, convw_ref, convb_ref, dtb_ref,
                       alog_ref, de_ref, normw_ref, e1_ref, e2_ref,
                       ya_ref, hout_ref,
                       cbuf, act, h_scr, x_scr, xd_scr, eae_scr, acsb_scr, acst_scr, y_scr,
                       *, n_groups, n_state, head_dim):
    c = pl.program_id(1)
    L = SSD_CHUNK
    hw = xs_ref.shape[1]
    gn = b_ref.shape[1]
    w_all = hw + 2 * gn
    n_heads = hw // head_dim
    hpl = LANES // head_dim
    n_blk = n_heads // hpl
    blk_per_group = n_blk // n_groups
    taps = convw_ref.shape[0]

    @pl.when(c == 0)
    def _():
        h_scr[...] = jnp.zeros_like(h_scr)
        cbuf[0:SUBLANES, :] = jnp.zeros((SUBLANES, w_all), F32)

    cbuf[SUBLANES:SUBLANES + L, 0:hw] = xs_ref[...]
    cbuf[SUBLANES:SUBLANES + L, hw:hw + gn] = b_ref[...]
    cbuf[SUBLANES:SUBLANES + L, hw + gn:] = c_ref[...]

    cw = math.gcd(w_all, 512)
    for blk in range(w_all // cw):
        sl = slice(blk * cw, (blk + 1) * cw)
        acc = convb_ref[:, sl] + convw_ref[taps - 1:taps, sl] * cbuf[SUBLANES:SUBLANES + L, sl]
        for j in range(1, taps):
            acc = acc + convw_ref[taps - 1 - j:taps - j, sl] * cbuf[SUBLANES - j:SUBLANES - j + L, sl]
        act[:, sl] = _silu(acc)
    cbuf[0:SUBLANES, :] = cbuf[L:L + SUBLANES, :]

    dt = _softplus(dt_ref[...] + dtb_ref[...])
    a_neg = -jnp.exp(alog_ref[...])
    dta = dt * a_neg
    ri = lax.broadcasted_iota(jnp.int32, (L, L), 0)
    ci = lax.broadcasted_iota(jnp.int32, (L, L), 1)
    causal = ri >= ci
    tri = jnp.where(causal, 1.0, 0.0).astype(BF16)
    acs = _dot3_rhs(tri, dta)
    e1 = e1_ref[...]
    dte = _dot3_lhs(dt, e1)
    acs_p = _split3(acs)
    acse = _dot(acs_p[0], e1) + _dot(acs_p[1], e1) + _dot(acs_p[2], e1)
    e2 = e2_ref[...]
    acsb_scr[...] = _dot(acs_p[0], e2) + _dot(acs_p[1], e2) + _dot(acs_p[2], e2)
    acst_scr[...] = acs.T

    xdt = act[:, 0:hw] * dte
    x_scr[...] = xdt.astype(BF16)
    xd_scr[...] = (xdt * jnp.exp(acse[L - 1:L, :] - acse)).astype(BF16)
    eae_scr[...] = jnp.exp(acse)

    lane = lax.broadcasted_iota(jnp.int32, (L, LANES), 1)
    cb = None
    for j in range(n_blk):
        g = j // blk_per_group
        bsl = slice(hw + g * n_state, hw + (g + 1) * n_state)
        csl = slice(hw + gn + g * n_state, hw + gn + (g + 1) * n_state)
        bg = act[:, bsl].astype(BF16)
        cg = act[:, csl].astype(BF16)
        if j % blk_per_group == 0:
            cb = lax.dot_general(cg, bg, NT_DIMS, preferred_element_type=F32)
        psl = slice(j * LANES, (j + 1) * LANES)
        xp = x_scr[:, psl]
        ydiag = None
        cds = []
        for q in range(hpl):
            r = j * hpl + q
            ab = acsb_scr[:, r * LANES:(r + 1) * LANES]
            at = jnp.broadcast_to(acst_scr[r:r + 1, :], (L, L))
            lm = jnp.where(causal, jnp.exp(ab - at), 0.0)
            m = (cb * lm).astype(BF16)
            inhead = (lane >= q * head_dim) & (lane < (q + 1) * head_dim)
            xq = jnp.where(inhead, xp, jnp.zeros_like(xp))
            yq = _dot(m, xq)
            ydiag = yq if ydiag is None else ydiag + yq
            cds.append(jnp.broadcast_to(jnp.exp(acsb_scr[L - 1:L, r * LANES:(r + 1) * LANES]),
                                        (head_dim, LANES)))
        cd = jnp.concatenate(cds, axis=0)
        hp = h_scr[psl, :]
        yoff = lax.dot_general(cg, hp.astype(BF16), NT_DIMS, preferred_element_type=F32)
        yoff = yoff * eae_scr[:, psl]
        st = lax.dot_general(xd_scr[:, psl], bg, TN_DIMS, preferred_element_type=F32)
        h_scr[psl, :] = hp * cd + st
        y_scr[:, psl] = ydiag + yoff + act[:, psl] * de_ref[:, psl]

    _ssd_gate_norm(y_scr[...], z_ref[...], normw_ref, ya_ref, n_groups)

    @pl.when(c == pl.num_programs(1) - 1)
    def _():
        hout_ref[0] = h_scr[...]


def _ssd_prompt(proj, dtraw, convw, convb, dtb, alog, de, normw, e1, e2, *, batch, seq,
                hw, gn, n_groups, n_state, head_dim, col):
    L = SSD_CHUNK
    nc = seq // L
    n_heads = hw // head_dim
    w_all = hw + 2 * gn
    row = lambda b, c: b * nc + c
    const = lambda b, c: (0, 0)
    kern = functools.partial(_ssd_prompt_kernel, n_groups=n_groups, n_state=n_state, head_dim=head_dim)
    return pl.pallas_call(
        kern,
        grid=(batch, nc),
        in_specs=[
            pl.BlockSpec((L, hw), lambda b, c: (row(b, c), col["z"] // hw)),
            pl.BlockSpec((L, hw), lambda b, c: (row(b, c), col["xs"] // hw)),
            pl.BlockSpec((L, gn), lambda b, c: (row(b, c), col["B"] // gn)),
            pl.BlockSpec((L, gn), lambda b, c: (row(b, c), col["C"] // gn)),
            pl.BlockSpec((L, LANES), lambda b, c: (row(b, c), 0)),
            pl.BlockSpec(convw.shape, const),
            pl.BlockSpec(convb.shape, const),
            pl.BlockSpec(dtb.shape, const),
            pl.BlockSpec(alog.shape, const),
            pl.BlockSpec(de.shape, const),
            pl.BlockSpec(normw.shape, const),
            pl.BlockSpec(e1.shape, const),
            pl.BlockSpec(e2.shape, const),
        ],
        out_specs=[
            pl.BlockSpec((L, hw), lambda b, c: (row(b, c), 0)),
            pl.BlockSpec((1, hw, n_state), lambda b, c: (b, 0, 0)),
        ],
        out_shape=[
            jax.ShapeDtypeStruct((proj.shape[0], hw), BF16),
            jax.ShapeDtypeStruct((batch, hw, n_state), F32),
        ],
        scratch_shapes=[
            pltpu.VMEM((L + SUBLANES, w_all), F32),
            pltpu.VMEM((L, w_all), F32),
            pltpu.VMEM((hw, n_state), F32),
            pltpu.VMEM((L, hw), BF16),
            pltpu.VMEM((L, hw), BF16),
            pltpu.VMEM((L, hw), F32),
            pltpu.VMEM((L, n_heads * LANES), F32),
            pltpu.VMEM((L, L), F32),
            pltpu.VMEM((L, hw), F32),
        ],
        compiler_params=_cparams("parallel", "arbitrary"),
        name="ssd_prompt",
    )(proj, proj, proj, proj, dtraw, convw, convb, dtb, alog, de, normw, e1, e2)


def _ssd_sample_kernel(z_ref, xs_ref, b_ref, c_ref, dt_ref, cst_ref, h_ref, convw_ref, convb_ref,
                       dtb_ref, alog_ref, de_ref, normw_ref, e1_ref, ya_all_ref,
                       ya_ref, hout_ref, xbc, y_scr,
                       *, n_groups, n_state):
    bb = xs_ref.shape[0]
    hw = xs_ref.shape[1]
    gn = b_ref.shape[1]
    taps = convw_ref.shape[0]
    gw = hw // n_groups

    xbc[:, 0:hw] = xs_ref[...]
    xbc[:, hw:hw + gn] = b_ref[...]
    xbc[:, hw + gn:] = c_ref[...]
    acc = convb_ref[...] + convw_ref[taps - 1:taps, :] * xbc[...]
    for j in range(taps - 1):
        acc = acc + convw_ref[j:j + 1, :] * cst_ref[j]
    act = _silu(acc)
    xs = act[:, 0:hw]

    dt = _softplus(dt_ref[...] + dtb_ref[...])
    dec = jnp.exp(dt * (-jnp.exp(alog_ref[...])))
    e1 = e1_ref[...]
    dte = _dot3_lhs(dt, e1)
    dece = _dot3_lhs(dec, e1)
    xdt = xs * dte

    pieces = [p.astype(F32) for p in _split3(dece)] + [p.astype(F32) for p in _split3(xdt)]
    npc = len(pieces)
    stack = jnp.concatenate(pieces + [jnp.zeros((LANES - npc * bb, hw), F32)], axis=0)
    lt = stack.T.astype(BF16)

    krow = lax.broadcasted_iota(jnp.int32, (LANES, LANES), 0)
    rowid = lax.broadcasted_iota(jnp.int32, (bb, gw), 0)
    half = npc // 2
    y_scr[...] = jnp.zeros_like(y_scr)
    for s in range(bb):
        is_s = (krow % bb) == s
        sel_dec = jnp.where(is_s & (krow < half * bb), 1.0, 0.0).astype(BF16)
        sel_x = jnp.where(is_s & (krow >= half * bb) & (krow < npc * bb), 1.0, 0.0).astype(BF16)
        dec_b = _dot(lt, sel_dec)
        x_b = _dot(lt, sel_x)
        for g in range(n_groups):
            rows = slice(g * gw, (g + 1) * gw)
            brow = act[s:s + 1, hw + g * n_state:hw + (g + 1) * n_state]
            hn = h_ref[s, rows, :] * dec_b[rows, :] + x_b[rows, :] * brow
            hout_ref[s, rows, :] = hn
            cg = act[:, hw + gn + g * n_state:hw + gn + (g + 1) * n_state].astype(BF16)
            yg = lax.dot_general(cg, hn.astype(BF16), NT_DIMS, preferred_element_type=F32)
            y_scr[:, rows] = y_scr[:, rows] + jnp.where(rowid == s, yg, 0.0)

    y = y_scr[...] + xs * de_ref[...]
    _ssd_gate_norm(y, z_ref[...], normw_ref, ya_ref, n_groups)


def _ssd_sample(proj, dtraw, cst, h0, convw, convb, dtb, alog, de, normw, e1, ya_all, *, row0, nb,
                hw, gn, n_groups, n_state, col):
    bb = SUBLANES
    w_all = hw + 2 * gn
    r0 = row0 // bb
    const = lambda i: (0, 0)
    kern = functools.partial(_ssd_sample_kernel, n_groups=n_groups, n_state=n_state)
    return pl.pallas_call(
        kern,
        grid=(nb // bb,),
        in_specs=[
            pl.BlockSpec((bb, hw), lambda i: (r0 + i, col["z"] // hw)),
            pl.BlockSpec((bb, hw), lambda i: (r0 + i, col["xs"] // hw)),
            pl.BlockSpec((bb, gn), lambda i: (r0 + i, col["B"] // gn)),
            pl.BlockSpec((bb, gn), lambda i: (r0 + i, col["C"] // gn)),
            pl.BlockSpec((bb, LANES), lambda i: (r0 + i, 0)),
            pl.BlockSpec((cst.shape[0], bb, w_all), lambda i: (0, i, 0)),
            pl.BlockSpec((bb, hw, n_state), lambda i: (i, 0, 0)),
            pl.BlockSpec(convw.shape, const),
            pl.BlockSpec(convb.shape, const),
            pl.BlockSpec(dtb.shape, const),
            pl.BlockSpec(alog.shape, const),
            pl.BlockSpec(de.shape, const),
            pl.BlockSpec(normw.shape, const),
            pl.BlockSpec(e1.shape, const),
            pl.BlockSpec(memory_space=pl.ANY),
        ],
        out_specs=[
            pl.BlockSpec((bb, hw), lambda i: (r0 + i, 0)),
            pl.BlockSpec((bb, hw, n_state), lambda i: (i, 0, 0)),
        ],
        out_shape=[
            jax.ShapeDtypeStruct(ya_all.shape, BF16),
            jax.ShapeDtypeStruct((nb, hw, n_state), F32),
        ],
        scratch_shapes=[pltpu.VMEM((bb, w_all), F32), pltpu.VMEM((bb, hw), F32)],
        compiler_params=_cparams("parallel"),
        input_output_aliases={14: 0},
        name="ssd_sample",
    )(proj, proj, proj, proj, dtraw, cst, h0, convw, convb, dtb, alog, de, normw, e1, ya_all)


def _sc_prompt_kernel(b_ref, c_ref, h_ref, w_ref, yb_ref, tail_ref, cbuf):
    j = pl.program_id(1)
    ts = b_ref.shape[0]
    taps = w_ref.shape[0]

    @pl.when(j == 0)
    def _():
        cbuf[0:SUBLANES, :] = jnp.zeros((SUBLANES, cbuf.shape[1]), F32)

    cbuf[SUBLANES:SUBLANES + ts, :] = c_ref[...] * h_ref[...]
    u = w_ref[taps - 1:taps, :] * cbuf[SUBLANES:SUBLANES + ts, :]
    for k in range(1, taps):
        u = u + w_ref[taps - 1 - k:taps - k, :] * cbuf[SUBLANES - k:SUBLANES - k + ts, :]
    yb_ref[...] = (b_ref[...] * u).astype(yb_ref.dtype)
    cbuf[0:SUBLANES, :] = cbuf[ts:ts + SUBLANES, :]

    @pl.when(j == pl.num_programs(1) - 1)
    def _():
        tail_ref[0] = cbuf[0:SUBLANES, :]


def _sc_prompt(proj, w, *, batch, seq, width, col, ts):
    nt = seq // ts
    row = lambda b, j: b * nt + j
    return pl.pallas_call(
        _sc_prompt_kernel,
        grid=(batch, nt),
        in_specs=[
            pl.BlockSpec((ts, width), lambda b, j: (row(b, j), col["sc_b"] // width)),
            pl.BlockSpec((ts, width), lambda b, j: (row(b, j), col["sc_c"] // width)),
            pl.BlockSpec((ts, width), lambda b, j: (row(b, j), col["sc_h"] // width)),
            pl.BlockSpec(w.shape, lambda b, j: (0, 0)),
        ],
        out_specs=[
            pl.BlockSpec((ts, width), lambda b, j: (row(b, j), 0)),
            pl.BlockSpec((1, SUBLANES, width), lambda b, j: (b, 0, 0)),
        ],
        out_shape=[
            jax.ShapeDtypeStruct((proj.shape[0], width), BF16),
            jax.ShapeDtypeStruct((batch, SUBLANES, width), F32),
        ],
        scratch_shapes=[pltpu.VMEM((ts + SUBLANES, width), F32)],
        compiler_params=_cparams("parallel", "arbitrary"),
        name="shortconv_prompt",
    )(proj, proj, proj, w)


def _sc_sample_kernel(b_ref, c_ref, h_ref, st_ref, w_ref, yb_all_ref, yb_ref, ch_ref):
    taps = w_ref.shape[0]
    ch = c_ref[...] * h_ref[...]
    u = w_ref[taps - 1:taps, :] * ch
    for k in range(taps - 1):
        u = u + w_ref[k:k + 1, :] * st_ref[k]
    yb_ref[...] = (b_ref[...] * u).astype(yb_ref.dtype)
    ch_ref[...] = ch


def _sc_sample(proj, st, w, yb_all, *, row0, nb, width, col):
    r0 = row0 // nb
    return pl.pallas_call(
        _sc_sample_kernel,
        grid=(1,),
        in_specs=[
            pl.BlockSpec((nb, width), lambda i: (r0, col["sc_b"] // width)),
            pl.BlockSpec((nb, width), lambda i: (r0, col["sc_c"] // width)),
            pl.BlockSpec((nb, width), lambda i: (r0, col["sc_h"] // width)),
            pl.BlockSpec(st.shape, lambda i: (0, 0, 0)),
            pl.BlockSpec(w.shape, lambda i: (0, 0)),
            pl.BlockSpec(memory_space=pl.ANY),
        ],
        out_specs=[pl.BlockSpec((nb, width), lambda i: (r0, 0)),
                   pl.BlockSpec((nb, width), lambda i: (0, 0))],
        out_shape=[jax.ShapeDtypeStruct(yb_all.shape, BF16),
                   jax.ShapeDtypeStruct((nb, width), F32)],
        input_output_aliases={5: 0},
        compiler_params=_cparams("arbitrary"),
        name="shortconv_sample",
    )(proj, proj, proj, st, w, yb_all)


def _branch_kernel(ya_ref, yb_ref, wa_ref, wb_ref, ga_ref, gb_ref, o_ref):
    ta = _dot(ya_ref[...], wa_ref[...])
    tb = _dot(yb_ref[...], wb_ref[...])
    o_ref[...] = (_sigmoid(ga_ref[...]) * ta + _sigmoid(gb_ref[...]) * tb).astype(o_ref.dtype)


def _branch(ya, yb, wa, wb, proj, *, col, tm, tn):
    m, k = ya.shape
    n = wa.shape[1]
    return pl.pallas_call(
        _branch_kernel,
        grid=(m // tm, n // tn),
        in_specs=[
            pl.BlockSpec((tm, k), lambda i, j: (i, 0)),
            pl.BlockSpec((tm, yb.shape[1]), lambda i, j: (i, 0)),
            pl.BlockSpec((k, tn), lambda i, j: (0, j)),
            pl.BlockSpec((yb.shape[1], tn), lambda i, j: (0, j)),
            pl.BlockSpec((tm, tn), lambda i, j: (i, col["g_a"] // tn + j)),
            pl.BlockSpec((tm, tn), lambda i, j: (i, col["g_b"] // tn + j)),
        ],
        out_specs=pl.BlockSpec((tm, tn), lambda i, j: (i, j)),
        out_shape=jax.ShapeDtypeStruct((m, n), BF16),
        compiler_params=_cparams("parallel", "parallel"),
        name="branch_mix",
    )(ya, yb, wa, wb, proj, proj)


def _x1_kernel(x_ref, mix_ref, w_ref, g_ref, b_ref, *rest, alpha):
    o_ref, ob_ref = rest[-2:]
    t = alpha * x_ref[...] + _dot(mix_ref[...], w_ref[...])
    x1 = _layer_norm(t, g_ref[...], b_ref[...])
    o_ref[...] = x1
    ob_ref[...] = pltpu.bitcast(x1.astype(BF16), U32)


def _x1(x, mix, w, g, b, prev, *, alpha, row0, tm):
    n, d = x.shape
    m = mix.shape[0]
    r0 = row0 // tm
    const = lambda i: (0, 0)
    alias = [pl.BlockSpec(memory_space=pl.ANY)] * len(prev)
    return pl.pallas_call(
        functools.partial(_x1_kernel, alpha=alpha),
        grid=(n // tm,),
        in_specs=[
            pl.BlockSpec((tm, d), lambda i: (i, 0)),
            pl.BlockSpec((tm, d), lambda i: (r0 + i, 0)),
            pl.BlockSpec(w.shape, const),
            pl.BlockSpec(g.shape, const),
            pl.BlockSpec(b.shape, const),
        ] + alias,
        out_specs=[pl.BlockSpec((tm, d), lambda i: (r0 + i, 0)),
                   pl.BlockSpec((_words(tm), d), lambda i: (r0 + i, 0))],
        out_shape=[jax.ShapeDtypeStruct((m, d), F32), jax.ShapeDtypeStruct((_words(m), d), U32)],
        input_output_aliases={5 + k: k for k in range(len(prev))},
        compiler_params=_cparams("parallel"),
        name="x1_out_ln",
    )(x, mix, w, g, b, *prev)


def _sort_pairs(lo, hi):
    def merge(lo, hi, r):
        step = r * 2
        if step < hi - lo:
            yield from merge(lo, hi, step)
            yield from merge(lo + r, hi, step)
            yield from [(i, i + r) for i in range(lo + r, hi - r, step)]
        else:
            yield (lo, lo + r)
    if hi - lo >= 1:
        mid = lo + (hi - lo) // 2
        yield from _sort_pairs(lo, mid)
        yield from _sort_pairs(mid + 1, hi)
        yield from merge(lo, hi, 1)


def _exchange(v, i, j):
    v[i], v[j] = jnp.maximum(v[i], v[j]), jnp.minimum(v[i], v[j])


def _sublane_all(op, x):
    dist = SUBLANES // 2
    while dist:
        x = op(x, pltpu.roll(x, dist, 0))
        dist //= 2
    return x


def _top_sorted(s, k):
    assert s.shape[0] == k * SUBLANES and k & (k - 1) == 0
    v = [s[j * SUBLANES:(j + 1) * SUBLANES, :] for j in range(k)]
    for i, j in _sort_pairs(0, k - 1):
        _exchange(v, i, j)
    dist = SUBLANES // 2
    while dist:
        w = [pltpu.roll(x, dist, 0) for x in v]
        v = [jnp.maximum(v[j], w[k - 1 - j]) for j in range(k)]
        stride = k // 2
        while stride:
            for i in range(k):
                if not i & stride:
                    _exchange(v, i, i + stride)
            stride //= 2
        dist //= 2
    return v


def _prefix_count(pred, t):
    w = jnp.where
    m1 = pred(t[7])
    m2 = pred(w(m1, t[11], t[3]))
    m3 = pred(w(m1, w(m2, t[13], t[9]), w(m2, t[5], t[1])))
    m4 = pred(w(m1, w(m2, w(m3, t[14], t[12]), w(m3, t[10], t[8])),
                w(m2, w(m3, t[6], t[4]), w(m3, t[2], t[0]))))
    return (w(m1, 8.0, 0.0) + w(m2, 4.0, 0.0) + w(m3, 2.0, 0.0) + w(m4, 1.0, 0.0)
            + w(pred(t[15]), 1.0, 0.0))


def _route_kernel(x_ref, wq_ref, k1_ref, k2_ref, c1_ref, w1_ref, r2_ref, e2_ref, q_scr, s1_scr, s2_scr):
    n_heads, n_keys, dk = k1_ref.shape
    tm = q_scr.shape[0]
    K = PEER_TOPK
    S = SUBLANES
    assert K == 16 and n_keys == K * S
    sub = lax.broadcasted_iota(jnp.int32, (S, LANES), 0)
    q_scr[...] = _dot(pltpu.bitcast(x_ref[...], BF16), wq_ref[...]).astype(BF16)
    for h in range(n_heads):
        q1 = q_scr[:, (2 * h) * dk:(2 * h + 1) * dk]
        q2 = q_scr[:, (2 * h + 1) * dk:(2 * h + 2) * dk]
        s1_scr[...] = lax.dot_general(k1_ref[h], q1, NT_DIMS, preferred_element_type=F32)
        s2_scr[...] = lax.dot_general(k2_ref[h], q2, NT_DIMS, preferred_element_type=F32)
        for c in range(tm // LANES):
            cs = slice(c * LANES, (c + 1) * LANES)
            s1 = [s1_scr[j * S:(j + 1) * S, cs] for j in range(K)]
            s2 = [s2_scr[j * S:(j + 1) * S, cs] for j in range(K)]
            t1 = _top_sorted(s1_scr[:, cs], K)
            t2 = _top_sorted(s2_scr[:, cs], K)
            rows = lambda t: functools.reduce(lambda acc, b: jnp.where(sub == b, t[b], acc), range(1, S), t[0])
            t2lo, t2hi, t1hi = rows(t2[:S]), rows(t2[S:]), rows(t1[S:])
            cand = [t1[0] + t2lo, t1[0] + t2hi] + [t1[a] + t2lo for a in range(1, S)] + [t1hi + t2[0]]
            cur = cand
            for r in range(K):
                thr = _sublane_all(jnp.maximum, functools.reduce(jnp.maximum, cur))
                if r + 1 < K:
                    cur = [jnp.where(x == thr, -jnp.inf, x) for x in cur]
            top = t1[0] + t2[0]
            zsum = _sublane_all(jnp.add, sum(jnp.where(x >= thr, jnp.exp(x - top), 0.0) for x in cand))
            inv_z = 1.0 / zsum
            rank2 = []
            for j in range(K):
                rows8 = slice(j * S, (j + 1) * S)
                c1_ref[h, rows8, cs] = _prefix_count(lambda tv, x=s1[j]: x + tv >= thr, t2)
                w1_ref[h, rows8, cs] = jnp.exp(s1[j] - t1[0]) * inv_z
                rank2.append(_prefix_count(lambda tv, x=s2[j]: tv > x, t2))
            r2_ref[h, :, cs] = pltpu.bitcast(jnp.concatenate(rank2, axis=0).astype(BF16), U32)
            e2 = jnp.concatenate([jnp.exp(x - t2[0]) for x in s2], axis=0)
            e2_ref[h, :, cs] = pltpu.bitcast(e2.astype(BF16), U32)


def _route(x1p, wq, k1, k2, *, m, tm):
    d = x1p.shape[1]
    n_heads, n_keys, _ = k1.shape
    tok = lambda rows: pl.BlockSpec((n_heads, rows, tm), lambda i: (0, 0, i))
    shp = lambda rows, dt: jax.ShapeDtypeStruct((n_heads, rows, m), dt)
    return pl.pallas_call(
        _route_kernel,
        grid=(m // tm,),
        in_specs=[
            pl.BlockSpec((_words(tm), d), lambda i: (i, 0)),
            pl.BlockSpec(wq.shape, lambda i: (0, 0)),
            pl.BlockSpec(k1.shape, lambda i: (0, 0, 0)),
            pl.BlockSpec(k2.shape, lambda i: (0, 0, 0)),
        ],
        out_specs=[tok(n_keys), tok(n_keys), tok(_words(n_keys)), tok(_words(n_keys))],
        out_shape=[shp(n_keys, F32), shp(n_keys, F32), shp(_words(n_keys), U32), shp(_words(n_keys), U32)],
        scratch_shapes=[pltpu.VMEM((tm, wq.shape[1]), BF16),
                        pltpu.VMEM((n_keys, tm), F32), pltpu.VMEM((n_keys, tm), F32)],
        compiler_params=_cparams("parallel"),
        name="peer_route",
    )(x1p, wq, k1, k2)


def _expert_kernel(x_ref, u_ref, vt_ref, c1_ref, w1_ref, r2_ref, e2_ref, o_ref, acc, coef, hid):
    e = pl.program_id(1)
    n_heads, per, tm = c1_ref.shape
    n_keys = hid.shape[0] // per
    half = _words(n_keys)

    @pl.when(e == 0)
    def _():
        acc[...] = jnp.zeros_like(acc)

    hid[...] = lax.dot_general(pltpu.bitcast(u_ref[...], BF16), pltpu.bitcast(x_ref[...], BF16), NT_DIMS,
                               preferred_element_type=F32)
    for k in range(per):
        for c in range(tm // LANES):
            cs = slice(c * LANES, (c + 1) * LANES)
            gate = jnp.zeros((n_keys, LANES), BF16)
            for h in range(n_heads):
                cnt = c1_ref[h, k:k + 1, cs].astype(BF16)
                w1 = w1_ref[h, k:k + 1, cs].astype(BF16)
                r2 = pltpu.bitcast(r2_ref[h, :, cs], BF16)
                e2 = pltpu.bitcast(e2_ref[h, :, cs], BF16)
                gate = gate + jnp.where(r2 < cnt, e2 * w1, jnp.zeros_like(gate))
            hk = hid[k * n_keys:(k + 1) * n_keys, cs]
            gelu = 0.5 * hk * (1.0 + lax.erf(hk * (1.0 / math.sqrt(2.0))))
            coef[k * half:(k + 1) * half, cs] = pltpu.bitcast(gate * gelu.astype(BF16), U32)

    acc[...] += _dot(pltpu.bitcast(vt_ref[...], BF16), pltpu.bitcast(coef[...], BF16))

    @pl.when(e == pl.num_programs(1) - 1)
    def _():
        o_ref[...] = acc[...].T.astype(o_ref.dtype)


def _experts(x1p, u, vt, c1, w1, r2, e2, *, m, tm, te):
    d = x1p.shape[1]
    n_tiles = u.shape[0] // _words(te)
    n_heads, n_keys, _ = c1.shape
    tok = lambda rows: pl.BlockSpec((n_heads, rows, tm), lambda i, e: (0, 0, i))
    per = te // n_keys
    assert per % SUBLANES == 0
    row = pl.BlockSpec((n_heads, per, tm), lambda i, e: (0, e, i))
    return pl.pallas_call(
        _expert_kernel,
        grid=(m // tm, n_tiles),
        in_specs=[
            pl.BlockSpec((_words(tm), d), lambda i, e: (i, 0)),
            pl.BlockSpec((_words(te), d), lambda i, e: (e, 0)),
            pl.BlockSpec((_words(d), te), lambda i, e: (0, e)),
            row, row, tok(_words(n_keys)), tok(_words(n_keys)),
        ],
        out_specs=pl.BlockSpec((tm, d), lambda i, e: (i, 0)),
        out_shape=jax.ShapeDtypeStruct((m, d), BF16),
        scratch_shapes=[pltpu.VMEM((d, tm), F32),
                        pltpu.VMEM((_words(te), tm), U32),
                        pltpu.VMEM((te, tm), F32)],
        compiler_params=_cparams("parallel", "arbitrary"),
        name="peer_experts",
    )(x1p, u, vt, c1, w1, r2, e2)


def _final_kernel(x1_ref, peer_ref, p_ref, wg_ref, wp_ref, g_ref, b_ref, o_ref, *, alpha):
    x2 = _layer_norm(alpha * x1_ref[...] + peer_ref[...], g_ref[...], b_ref[...])
    gate = _sigmoid(_dot(x2.astype(BF16), wg_ref[...]))
    y = x2 + gate * _dot(p_ref[...].astype(BF16), wp_ref[...])
    if len(o_ref.shape) == 3:
        o_ref[:, 0, :] = y
    else:
        o_ref[...] = y


def _final(x1, peer, p, wg, wp, g, b, *, alpha, row0, tm, per_step):
    d = x1.shape[1]
    n = p.shape[0]
    r0 = row0 // tm
    rows = lambda i: (r0 + i, 0)
    const = lambda i: (0, 0)
    if per_step:
        out_shape, out_spec = (n, 1, d), pl.BlockSpec((tm, 1, d), lambda i: (i, 0, 0))
    else:
        out_shape, out_spec = (n, d), pl.BlockSpec((tm, d), lambda i: (i, 0))
    return pl.pallas_call(
        functools.partial(_final_kernel, alpha=alpha),
        grid=(n // tm,),
        in_specs=[
            pl.BlockSpec((tm, d), rows),
            pl.BlockSpec((tm, d), rows),
            pl.BlockSpec((tm, p.shape[1]), lambda i: (i, 0)),
            pl.BlockSpec(wg.shape, const),
            pl.BlockSpec(wp.shape, const),
            pl.BlockSpec(g.shape, const),
            pl.BlockSpec(b.shape, const),
        ],
        out_specs=out_spec,
        out_shape=jax.ShapeDtypeStruct(out_shape, F32),
        compiler_params=_cparams("parallel"),
        name="ln2_ple",
    )(x1, peer, p, wg, wp, g, b)


TOKEN_TILE = 640
PROJ_TILE = 1664
PROJ_COLS = 512
ROW_TILE = 320
FINAL_TILE = 512
EXPERT_TILE = 1024
PEER_TOKEN_TILE = 768


def _pack_kernel(w_ref, o_ref, *, transpose):
    w = w_ref[...]
    o_ref[...] = pltpu.bitcast((w.T if transpose else w).astype(BF16), U32)


def _pack_rows(w, transpose=False, tr=512):
    r, c = w.shape
    if transpose:
        out_shape, out_spec = (_words(c), r), pl.BlockSpec((_words(c), tr), lambda i: (0, i))
    else:
        out_shape, out_spec = (_words(r), c), pl.BlockSpec((_words(tr), c), lambda i: (i, 0))
    return pl.pallas_call(
        functools.partial(_pack_kernel, transpose=transpose),
        grid=(r // tr,),
        in_specs=[pl.BlockSpec((tr, c), lambda i: (i, 0))],
        out_specs=out_spec,
        out_shape=jax.ShapeDtypeStruct(out_shape, U32),
        compiler_params=_cparams("parallel"),
        name="pack_weight",
    )(w)


def _pad_lanes(v):
    return jnp.pad(v.astype(F32), (0, LANES - v.shape[0])).reshape(1, LANES)


def _layer(x_p, x_s, p_p, p_s, ssm_h, conv_buf, sc_buf, batch, seq, depth,
           w_in, ssd_conv_w, ssd_conv_b, ssd_dt_bias, ssd_a_log, ssd_d, ssd_norm_w,
           sc_conv_w, w_branch_ssd, w_branch_sc, w_out, ln1_g, ln1_b,
           peer_wq, peer_keys1, peer_keys2, peer_u, peer_v, ln2_g, ln2_b,
           ple_gate_w, ple_proj_w):
    n_prompt, d = x_p.shape
    nb = x_s.shape[0]
    m = n_prompt + nb
    x = jnp.concatenate([x_p, x_s], axis=0).astype(BF16)
    n_heads = ssd_dt_bias.shape[0]
    hw, n_state = ssm_h.shape[1] * ssm_h.shape[2], ssm_h.shape[3]
    head_dim = ssm_h.shape[2]
    conv_dim = ssd_conv_w.shape[1]
    gn = (conv_dim - hw) // 2
    n_groups = gn // n_state
    scw = sc_conv_w.shape[1]
    alpha = (2.0 * depth) ** 0.25
    assert hw == scw == d and n_heads <= LANES and LANES % head_dim == 0 and n_state == LANES

    o_dt = hw + conv_dim
    o_scb = o_dt + n_heads
    col = {"z": 0, "xs": hw, "B": 2 * hw, "C": 2 * hw + gn}
    col2 = {"sc_b": 0, "sc_c": scw, "sc_h": 2 * scw, "g_a": 3 * scw, "g_b": 3 * scw + d}
    tm = _tile(m, TOKEN_TILE, LANES)
    tmi = _tile(m, PROJ_TILE, LANES)
    w_in_t = w_in.T
    proj = _in_proj(x, w_in_t, col0=0, n=o_dt, tm=tmi)
    proj2 = _in_proj(x, w_in_t, col0=o_scb, n=3 * scw + 2 * d, tm=tmi)
    dtraw = _in_proj(x, w_in_t, col0=o_dt, n=LANES, tm=tmi)

    ch_head = jnp.arange(hw) // head_dim
    e1 = (jnp.arange(LANES)[:, None] == ch_head[None, :]).astype(BF16)
    e2 = (jnp.arange(LANES)[:, None] == (jnp.arange(n_heads * LANES) // LANES)[None, :]).astype(BF16)
    convb = ssd_conv_b.reshape(1, conv_dim)
    dtb, alog = _pad_lanes(ssd_dt_bias), _pad_lanes(ssd_a_log)
    de = jnp.repeat(ssd_d.astype(F32), head_dim).reshape(1, hw)
    normw = ssd_norm_w.reshape(1, hw)
    shp = dict(hw=hw, gn=gn, n_groups=n_groups, n_state=n_state, col=col)
    ya_p, h_p = _ssd_prompt(proj, dtraw, ssd_conv_w, convb, dtb, alog, de, normw, e1, e2,
                            batch=batch, seq=seq, head_dim=head_dim, **shp)
    cst = jnp.transpose(conv_buf, (1, 0, 2))
    ya, h_s = _ssd_sample(proj, dtraw, cst, ssm_h.reshape(nb, hw, n_state), ssd_conv_w, convb,
                          dtb, alog, de, normw, e1, ya_p, row0=n_prompt, nb=nb, **shp)

    yb_p, sc_tail = _sc_prompt(proj2, sc_conv_w, batch=batch, seq=seq, width=scw, col=col2,
                               ts=min(seq, 256))
    yb, ch_s = _sc_sample(proj2, jnp.transpose(sc_buf, (1, 0, 2)), sc_conv_w, yb_p,
                          row0=n_prompt, nb=nb, width=scw, col=col2)

    mix = _branch(ya, yb, w_branch_ssd.astype(BF16), w_branch_sc.astype(BF16), proj2,
                  col=col2, tm=tm, tn=_tile(d, 1024, LANES))
    assert n_prompt % nb == 0
    ln1 = functools.partial(_x1, mix=mix, w=w_out.astype(BF16), g=ln1_g.reshape(1, d), b=ln1_b.reshape(1, d),
                            alpha=alpha)
    x1_parts = ln1(x_p, prev=(), row0=0, tm=_tile(n_prompt, FINAL_TILE, BF16_ROWS))
    x1, x1b = ln1(x_s, prev=tuple(x1_parts), row0=n_prompt, tm=nb)

    tmp = PEER_TOKEN_TILE
    mp = m + (-m % tmp)
    x1p = jnp.pad(x1b, ((0, _words(mp - m)), (0, 0)))
    route = _route(x1p, peer_wq.astype(BF16), peer_keys1.astype(BF16), peer_keys2.astype(BF16), m=mp, tm=tmp)
    peer = _experts(x1p, _pack_rows(peer_u), _pack_rows(peer_v, transpose=True), *route,
                    m=mp, tm=tmp, te=EXPERT_TILE)

    fin = functools.partial(_final, x1, peer, wg=ple_gate_w.astype(BF16), wp=ple_proj_w.astype(BF16),
                            g=ln2_g.reshape(1, d), b=ln2_b.reshape(1, d), alpha=alpha)
    y_p = fin(p=p_p, row0=0, tm=_tile(n_prompt, FINAL_TILE, SUBLANES), per_step=False)
    y_s = fin(p=p_s, row0=n_prompt, tm=nb, per_step=True)

    k_ssd = ssd_conv_w.shape[0] - 1
    k_sc = sc_conv_w.shape[0] - 1
    xbc_cols = lambda rows: rows[..., hw:hw + conv_dim]
    conv_p = xbc_cols(jnp.stack([proj[(b + 1) * seq - k_ssd:(b + 1) * seq] for b in range(batch)]))
    conv_s = jnp.concatenate([conv_buf[:, 1:, :], xbc_cols(proj[n_prompt:])[:, None, :]], axis=1)
    sc_p = sc_tail[:, SUBLANES - k_sc:, :]
    sc_s = jnp.concatenate([sc_buf[:, 1:, :], ch_s[:, None, :]], axis=1)
    hshape = (-1, n_heads, head_dim, n_state)
    return y_p, y_s, conv_p, h_p.reshape(hshape), sc_p, conv_s, h_s.reshape(hshape), sc_s


def kernel(x_prompt, x_sample, p_prompt, p_sample, state_ssm, state_ssd_conv, state_shortconv, w_in, ssd_conv_w, ssd_conv_b, ssd_dt_bias, ssd_a_log, ssd_d, ssd_norm_w, sc_conv_w, w_branch_ssd, w_branch_sc, w_out, ln1_g, ln1_b, peer_wq, peer_keys1, peer_keys2, peer_u, peer_v, ln2_g, ln2_b, ple_gate_w, ple_proj_w):
    batch, seq, d = x_prompt.shape
    nb, dec_seq, _ = x_sample.shape
    assert dec_seq == 1 and seq % SSD_CHUNK == 0
    depth = w_in.shape[0]
    n_prompt = batch * seq
    x_p, x_s = x_prompt.reshape(n_prompt, d), x_sample.reshape(nb, d)
    weights = (w_in, ssd_conv_w, ssd_conv_b, ssd_dt_bias, ssd_a_log, ssd_d, ssd_norm_w,
               sc_conv_w, w_branch_ssd, w_branch_sc, w_out, ln1_g, ln1_b,
               peer_wq, peer_keys1, peer_keys2, peer_u, peer_v, ln2_g, ln2_b,
               ple_gate_w, ple_proj_w)
    outs = [[] for _ in range(6)]
    for i in range(depth):
        y_p, y_s, conv_p, h_p, sc_p, conv_s, h_s, sc_s = _layer(
            x_p, x_s, p_prompt[i].reshape(n_prompt, -1), p_sample[i].reshape(nb, -1),
            state_ssm[i], state_ssd_conv[i], state_shortconv[i], batch, seq, depth,
            *[w[i] for w in weights])
        for lst, val in zip(outs, (h_p, conv_p, sc_p, h_s, conv_s, sc_s)):
            lst.append(val)
        x_p, x_s = y_p, y_s.reshape(nb, d)
    y_prompt = y_p.reshape(batch, seq, d)
    y_sample = y_s
    return (y_prompt, y_sample) + tuple(jnp.stack(lst) for lst in outs)
```

```python
import functools
import math

import jax
import jax.numpy as jnp
from jax import lax
from jax.experimental import pallas as pl
from jax.experimental.pallas import tpu as pltpu

F32 = jnp.float32
BF16 = jnp.bfloat16
U32 = jnp.uint32

LANES = 128
SUBLANES = 8
BF16_ROWS = 16
PEER_TOPK = 16
SSD_CHUNK = 128
LN_EPS = 1e-5
RMS_EPS = 1e-5
VMEM_LIMIT = 56 * 1024 * 1024

NT_DIMS = (((1,), (1,)), ((), ()))
TN_DIMS = (((0,), (0,)), ((), ()))


def _cparams(*sem, flags=None):
    return pltpu.CompilerParams(dimension_semantics=sem, vmem_limit_bytes=VMEM_LIMIT, flags=flags)


def _dot(a, b):
    return jnp.dot(a, b, preferred_element_type=F32)


def _split3(v):
    hi = v.astype(BF16)
    r = v - hi.astype(F32)
    mid = r.astype(BF16)
    lo = (r - mid.astype(F32)).astype(BF16)
    return hi, mid, lo


def _dot3_lhs(v, rhs_bf16):
    hi, mid, lo = _split3(v)
    return _dot(hi, rhs_bf16) + _dot(mid, rhs_bf16) + _dot(lo, rhs_bf16)


def _dot3_rhs(lhs_bf16, v):
    hi, mid, lo = _split3(v)
    return _dot(lhs_bf16, hi) + _dot(lhs_bf16, mid) + _dot(lhs_bf16, lo)


def _words(rows):
    return rows * jnp.dtype(BF16).itemsize // jnp.dtype(U32).itemsize


def _sigmoid(x):
    return 1.0 / (1.0 + jnp.exp(-x))


def _silu(x):
    return x * _sigmoid(x)


def _softplus(x):
    return jnp.maximum(x, 0.0) + jnp.log1p(jnp.exp(-jnp.abs(x)))


def _layer_norm(x, g, b):
    mu = jnp.mean(x, axis=-1, keepdims=True)
    xc = x - mu
    var = jnp.mean(xc * xc, axis=-1, keepdims=True)
    return xc * lax.rsqrt(var + LN_EPS) * g + b


def _tile(m, cap, mult):
    best = None
    for t in range(mult, min(m, cap) + 1, mult):
        if m % t == 0:
            best = t
    assert best is not None, (m, cap, mult)
    return best


def _in_proj_kernel(x_ref, wa_ref, wb_ref, o_ref, *, shift):
    tn = wa_ref.shape[0]
    if shift:
        w = jnp.concatenate([wa_ref[...], wb_ref[...]], axis=0)[shift:shift + tn]
    else:
        w = wa_ref[...]
    o_ref[...] = lax.dot_general(x_ref[...], w.astype(BF16), NT_DIMS, preferred_element_type=F32)


def _in_proj(x, wt, *, col0, n, tm):
    m, k = x.shape
    shift = col0 % LANES
    assert shift % SUBLANES == 0
    base = col0 - shift
    tn = _tile(math.gcd(n, base) if base else n, PROJ_COLS, LANES)
    last = pl.cdiv(wt.shape[0], LANES) - 1
    return pl.pallas_call(
        functools.partial(_in_proj_kernel, shift=shift),
        grid=(m // tm, n // tn),
        in_specs=[pl.BlockSpec((tm, k), lambda i, j: (i, 0)),
                  pl.BlockSpec((tn, k), lambda i, j: (base // tn + j, 0)),
                  pl.BlockSpec((LANES, k), lambda i, j: (jnp.minimum((base + (j + 1) * tn) // LANES, last), 0))],
        out_specs=pl.BlockSpec((tm, tn), lambda i, j: (i, j)),
        out_shape=jax.ShapeDtypeStruct((m, n), F32),
        compiler_params=_cparams("parallel", "parallel"),
        name="in_proj",
    )(x, wt, wt)


def _ssd_gate_norm(y, z, normw_ref, out_ref, n_groups):
    y = y * _silu(z)
    gw = y.shape[1] // n_groups
    for g in range(n_groups):
        sl = slice(g * gw, (g + 1) * gw)
        yg = y[:, sl]
        ms = jnp.mean(yg * yg, axis=-1, keepdims=True)
        out_ref[:, sl] = (yg * lax.rsqrt(ms + RMS_EPS) * normw_ref[:, sl]).astype(out_ref.dtype)


def _ssd_prompt_kernel(z_ref, xs_ref, b_ref, c_ref, dt_ref, convw_ref, convb_ref, dtb_ref,
                       alog_ref, de_ref, normw_ref, e1_ref, e2_ref,
                       ya_ref, hout_ref,
                       cbuf, act, h_scr, x_scr, xd_scr, eae_scr, acsb_scr, acst_scr, y_scr,
                       *, n_groups, n_state, head_dim):
    c = pl.program_id(1)
    L = SSD_CHUNK
    hw = xs_ref.shape[1]
    gn = b_ref.shape[1]
    w_all = hw + 2 * gn
    n_heads = hw // head_dim
    hpl = LANES // head_dim
    n_blk = n_heads // hpl
    blk_per_group = n_blk // n_groups
    taps = convw_ref.shape[0]

    @pl.when(c == 0)
    def _():
        h_scr[...] = jnp.zeros_like(h_scr)
        cbuf[0:SUBLANES, :] = jnp.zeros((SUBLANES, w_all), F32)

    cbuf[SUBLANES:SUBLANES + L, 0:hw] = xs_ref[...]
    cbuf[SUBLANES:SUBLANES + L, hw:hw + gn] = b_ref[...]
    cbuf[SUBLANES:SUBLANES + L, hw + gn:] = c_ref[...]

    cw = math.gcd(w_all, 512)
    for blk in range(w_all // cw):
        sl = slice(blk * cw, (blk + 1) * cw)
        acc = convb_ref[:, sl] + convw_ref[taps - 1:taps, sl] * cbuf[SUBLANES:SUBLANES + L, sl]
        for j in range(1, taps):
            acc = acc + convw_ref[taps - 1 - j:taps - j, sl] * cbuf[SUBLANES - j:SUBLANES - j + L, sl]
        act[:, sl] = _silu(acc)
    cbuf[0:SUBLANES, :] = cbuf[L:L + SUBLANES, :]

    dt = _softplus(dt_ref[...] + dtb_ref[...])
    a_neg = -jnp.exp(alog_ref[...])
    dta = dt * a_neg
    ri = lax.broadcasted_iota(jnp.int32, (L, L), 0)
    ci = lax.broadcasted_iota(jnp.int32, (L, L), 1)
    causal = ri >= ci
    tri = jnp.where(causal, 1.0, 0.0).astype(BF16)
    acs = _dot3_rhs(tri, dta)
    e1 = e1_ref[...]
    dte = _dot3_lhs(dt, e1)
    acs_p = _split3(acs)
    acse = _dot(acs_p[0], e1) + _dot(acs_p[1], e1) + _dot(acs_p[2], e1)
    e2 = e2_ref[...]
    acsb_scr[...] = _dot(acs_p[0], e2) + _dot(acs_p[1], e2) + _dot(acs_p[2], e2)
    acst_scr[...] = acs.T

    xdt = act[:, 0:hw] * dte
    x_scr[...] = xdt.astype(BF16)
    xd_scr[...] = (xdt * jnp.exp(acse[L - 1:L, :] - acse)).astype(BF16)
    eae_scr[...] = jnp.exp(acse)

    lane = lax.broadcasted_iota(jnp.int32, (L, LANES), 1)
    cb = None
    for j in range(n_blk):
        g = j // blk_per_group
        bsl = slice(hw + g * n_state, hw + (g + 1) * n_state)
        csl = slice(hw + gn + g * n_state, hw + gn + (g + 1) * n_state)
        bg = act[:, bsl].astype(BF16)
        cg = act[:, csl].astype(BF16)
        if j % blk_per_group == 0:
            cb = lax.dot_general(cg, bg, NT_DIMS, preferred_element_type=F32)
        psl = slice(j * LANES, (j + 1) * LANES)
        xp = x_scr[:, psl]
        ydiag = None
        cds = []
        for q in range(hpl):
            r = j * hpl + q
            ab = acsb_scr[:, r * LANES:(r + 1) * LANES]
            at = jnp.broadcast_to(acst_scr[r:r + 1, :], (L, L))
            lm = jnp.where(causal, jnp.exp(ab - at), 0.0)
            m = (cb * lm).astype(BF16)
            inhead = (lane >= q * head_dim) & (lane < (q + 1) * head_dim)
            xq = jnp.where(inhead, xp, jnp.zeros_like(xp))
            yq = _dot(m, xq)
            ydiag = yq if ydiag is None else ydiag + yq
            cds.append(jnp.broadcast_to(jnp.exp(acsb_scr[L - 1:L, r * LANES:(r + 1) * LANES]),
                                        (head_dim, LANES)))
        cd = jnp.concatenate(cds, axis=0)
        hp = h_scr[psl, :]
        yoff = lax.dot_general(cg, hp.astype(BF16), NT_DIMS, preferred_element_type=F32)
        yoff = yoff * eae_scr[:, psl]
        st = lax.dot_general(xd_scr[:, psl], bg, TN_DIMS, preferred_element_type=F32)
        h_scr[psl, :] = hp * cd + st
        y_scr[:, psl] = ydiag + yoff + act[:, psl] * de_ref[:, psl]

    _ssd_gate_norm(y_scr[...], z_ref[...], normw_ref, ya_ref, n_groups)

    @pl.when(c == pl.num_programs(1) - 1)
    def _():
        hout_ref[0] = h_scr[...]


def _ssd_prompt(proj, dtraw, convw, convb, dtb, alog, de, normw, e1, e2, *, batch, seq,
                hw, gn, n_groups, n_state, head_dim, col):
    L = SSD_CHUNK
    nc = seq // L
    n_heads = hw // head_dim
    w_all = hw + 2 * gn
    row = lambda b, c: b * nc + c
    const = lambda b, c: (0, 0)
    kern = functools.partial(_ssd_prompt_kernel, n_groups=n_groups, n_state=n_state, head_dim=head_dim)
    return pl.pallas_call(
        kern,
        grid=(batch, nc),
        in_specs=[
            pl.BlockSpec((L, hw), lambda b, c: (row(b, c), col["z"] // hw)),
            pl.BlockSpec((L, hw), lambda b, c: (row(b, c), col["xs"] // hw)),
            pl.BlockSpec((L, gn), lambda b, c: (row(b, c), col["B"] // gn)),
            pl.BlockSpec((L, gn), lambda b, c: (row(b, c), col["C"] // gn)),
            pl.BlockSpec((L, LANES), lambda b, c: (row(b, c), 0)),
            pl.BlockSpec(convw.shape, const),
            pl.BlockSpec(convb.shape, const),
            pl.BlockSpec(dtb.shape, const),
            pl.BlockSpec(alog.shape, const),
            pl.BlockSpec(de.shape, const),
            pl.BlockSpec(normw.shape, const),
            pl.BlockSpec(e1.shape, const),
            pl.BlockSpec(e2.shape, const),
        ],
        out_specs=[
            pl.BlockSpec((L, hw), lambda b, c: (row(b, c), 0)),
            pl.BlockSpec((1, hw, n_state), lambda b, c: (b, 0, 0)),
        ],
        out_shape=[
            jax.ShapeDtypeStruct((proj.shape[0], hw), BF16),
            jax.ShapeDtypeStruct((batch, hw, n_state), F32),
        ],
        scratch_shapes=[
            pltpu.VMEM((L + SUBLANES, w_all), F32),
            pltpu.VMEM((L, w_all), F32),
            pltpu.VMEM((hw, n_state), F32),
            pltpu.VMEM((L, hw), BF16),
            pltpu.VMEM((L, hw), BF16),
            pltpu.VMEM((L, hw), F32),
            pltpu.VMEM((L, n_heads * LANES), F32),
            pltpu.VMEM((L, L), F32),
            pltpu.VMEM((L, hw), F32),
        ],
        compiler_params=_cparams("parallel", "arbitrary"),
        name="ssd_prompt",
    )(proj, proj, proj, proj, dtraw, convw, convb, dtb, alog, de, normw, e1, e2)


def _ssd_sample_kernel(z_ref, xs_ref, b_ref, c_ref, dt_ref, cst_ref, h_ref, convw_ref, convb_ref,
                       dtb_ref, alog_ref, de_ref, normw_ref, e1_ref, ya_all_ref,
                       ya_ref, hout_ref, xbc, y_scr,
                       *, n_groups, n_state):
    bb = xs_ref.shape[0]
    hw = xs_ref.shape[1]
    gn = b_ref.shape[1]
    taps = convw_ref.shape[0]
    gw = hw // n_groups

    xbc[:, 0:hw] = xs_ref[...]
    xbc[:, hw:hw + gn] = b_ref[...]
    xbc[:, hw + gn:] = c_ref[...]
    acc = convb_ref[...] + convw_ref[taps - 1:taps, :] * xbc[...]
    for j in range(taps - 1):
        acc = acc + convw_ref[j:j + 1, :] * cst_ref[j]
    act = _silu(acc)
    xs = act[:, 0:hw]

    dt = _softplus(dt_ref[...] + dtb_ref[...])
    dec = jnp.exp(dt * (-jnp.exp(alog_ref[...])))
    e1 = e1_ref[...]
    dte = _dot3_lhs(dt, e1)
    dece = _dot3_lhs(dec, e1)
    xdt = xs * dte

    pieces = [p.astype(F32) for p in _split3(dece)] + [p.astype(F32) for p in _split3(xdt)]
    npc = len(pieces)
    stack = jnp.concatenate(pieces + [jnp.zeros((LANES - npc * bb, hw), F32)], axis=0)
    lt = stack.T.astype(BF16)

    krow = lax.broadcasted_iota(jnp.int32, (LANES, LANES), 0)
    rowid = lax.broadcasted_iota(jnp.int32, (bb, gw), 0)
    half = npc // 2
    y_scr[...] = jnp.zeros_like(y_scr)
    for s in range(bb):
        is_s = (krow % bb) == s
        sel_dec = jnp.where(is_s & (krow < half * bb), 1.0, 0.0).astype(BF16)
        sel_x = jnp.where(is_s & (krow >= half * bb) & (krow < npc * bb), 1.0, 0.0).astype(BF16)
        dec_b = _dot(lt, sel_dec)
        x_b = _dot(lt, sel_x)
        for g in range(n_groups):
            rows = slice(g * gw, (g + 1) * gw)
            brow = act[s:s + 1, hw + g * n_state:hw + (g + 1) * n_state]
            hn = h_ref[s, rows, :] * dec_b[rows, :] + x_b[rows, :] * brow
            hout_ref[s, rows, :] = hn
            cg = act[:, hw + gn + g * n_state:hw + gn + (g + 1) * n_state].astype(BF16)
            yg = lax.dot_general(cg, hn.astype(BF16), NT_DIMS, preferred_element_type=F32)
            y_scr[:, rows] = y_scr[:, rows] + jnp.where(rowid == s, yg, 0.0)

    y = y_scr[...] + xs * de_ref[...]
    _ssd_gate_norm(y, z_ref[...], normw_ref, ya_ref, n_groups)


def _ssd_sample(proj, dtraw, cst, h0, convw, convb, dtb, alog, de, normw, e1, ya_all, *, row0, nb,
                hw, gn, n_groups, n_state, col):
    bb = SUBLANES
    w_all = hw + 2 * gn
    r0 = row0 // bb
    const = lambda i: (0, 0)
    kern = functools.partial(_ssd_sample_kernel, n_groups=n_groups, n_state=n_state)
    return pl.pallas_call(
        kern,
        grid=(nb // bb,),
        in_specs=[
            pl.BlockSpec((bb, hw), lambda i: (r0 + i, col["z"] // hw)),
            pl.BlockSpec((bb, hw), lambda i: (r0 + i, col["xs"] // hw)),
            pl.BlockSpec((bb, gn), lambda i: (r0 + i, col["B"] // gn)),
            pl.BlockSpec((bb, gn), lambda i: (r0 + i, col["C"] // gn)),
            pl.BlockSpec((bb, LANES), lambda i: (r0 + i, 0)),
            pl.BlockSpec((cst.shape[0], bb, w_all), lambda i: (0, i, 0)),
            pl.BlockSpec((bb, hw, n_state), lambda i: (i, 0, 0)),
            pl.BlockSpec(convw.shape, const),
            pl.BlockSpec(convb.shape, const),
            pl.BlockSpec(dtb.shape, const),
            pl.BlockSpec(alog.shape, const),
            pl.BlockSpec(de.shape, const),
            pl.BlockSpec(normw.shape, const),
            pl.BlockSpec(e1.shape, const),
            pl.BlockSpec(memory_space=pl.ANY),
        ],
        out_specs=[
            pl.BlockSpec((bb, hw), lambda i: (r0 + i, 0)),
            pl.BlockSpec((bb, hw, n_state), lambda i: (i, 0, 0)),
        ],
        out_shape=[
            jax.ShapeDtypeStruct(ya_all.shape, BF16),
            jax.ShapeDtypeStruct((nb, hw, n_state), F32),
        ],
        scratch_shapes=[pltpu.VMEM((bb, w_all), F32), pltpu.VMEM((bb, hw), F32)],
        compiler_params=_cparams("parallel"),
        input_output_aliases={14: 0},
        name="ssd_sample",
    )(proj, proj, proj, proj, dtraw, cst, h0, convw, convb, dtb, alog, de, normw, e1, ya_all)


def _sc_prompt_kernel(b_ref, c_ref, h_ref, w_ref, yb_ref, tail_ref, cbuf):
    j = pl.program_id(1)
    ts = b_ref.shape[0]
    taps = w_ref.shape[0]

    @pl.when(j == 0)
    def _():
        cbuf[0:SUBLANES, :] = jnp.zeros((SUBLANES, cbuf.shape[1]), F32)

    cbuf[SUBLANES:SUBLANES + ts, :] = c_ref[...] * h_ref[...]
    u = w_ref[taps - 1:taps, :] * cbuf[SUBLANES:SUBLANES + ts, :]
    for k in range(1, taps):
        u = u + w_ref[taps - 1 - k:taps - k, :] * cbuf[SUBLANES - k:SUBLANES - k + ts, :]
    yb_ref[...] = (b_ref[...] * u).astype(yb_ref.dtype)
    cbuf[0:SUBLANES, :] = cbuf[ts:ts + SUBLANES, :]

    @pl.when(j == pl.num_programs(1) - 1)
    def _():
        tail_ref[0] = cbuf[0:SUBLANES, :]


def _sc_prompt(proj, w, *, batch, seq, width, col, ts):
    nt = seq // ts
    row = lambda b, j: b * nt + j
    return pl.pallas_call(
        _sc_prompt_kernel,
        grid=(batch, nt),
        in_specs=[
            pl.BlockSpec((ts, width), lambda b, j: (row(b, j), col["sc_b"] // width)),
            pl.BlockSpec((ts, width), lambda b, j: (row(b, j), col["sc_c"] // width)),
            pl.BlockSpec((ts, width), lambda b, j: (row(b, j), col["sc_h"] // width)),
            pl.BlockSpec(w.shape, lambda b, j: (0, 0)),
        ],
        out_specs=[
            pl.BlockSpec((ts, width), lambda b, j: (row(b, j), 0)),
            pl.BlockSpec((1, SUBLANES, width), lambda b, j: (b, 0, 0)),
        ],
        out_shape=[
            jax.ShapeDtypeStruct((proj.shape[0], width), BF16),
            jax.ShapeDtypeStruct((batch, SUBLANES, width), F32),
        ],
        scratch_shapes=[pltpu.VMEM((ts + SUBLANES, width), F32)],
        compiler_params=_cparams("parallel", "arbitrary"),
        name="shortconv_prompt",
    )(proj, proj, proj, w)


def _sc_sample_kernel(b_ref, c_ref, h_ref, st_ref, w_ref, yb_all_ref, yb_ref, ch_ref):
    taps = w_ref.shape[0]
    ch = c_ref[...] * h_ref[...]
    u = w_ref[taps - 1:taps, :] * ch
    for k in range(taps - 1):
        u = u + w_ref[k:k + 1, :] * st_ref[k]
    yb_ref[...] = (b_ref[...] * u).astype(yb_ref.dtype)
    ch_ref[...] = ch


def _sc_sample(proj, st, w, yb_all, *, row0, nb, width, col):
    r0 = row0 // nb
    return pl.pallas_call(
        _sc_sample_kernel,
        grid=(1,),
        in_specs=[
            pl.BlockSpec((nb, width), lambda i: (r0, col["sc_b"] // width)),
            pl.BlockSpec((nb, width), lambda i: (r0, col["sc_c"] // width)),
            pl.BlockSpec((nb, width), lambda i: (r0, col["sc_h"] // width)),
            pl.BlockSpec(st.shape, lambda i: (0, 0, 0)),
            pl.BlockSpec(w.shape, lambda i: (0, 0)),
            pl.BlockSpec(memory_space=pl.ANY),
        ],
        out_specs=[pl.BlockSpec((nb, width), lambda i: (r0, 0)),
                   pl.BlockSpec((nb, width), lambda i: (0, 0))],
        out_shape=[jax.ShapeDtypeStruct(yb_all.shape, BF16),
                   jax.ShapeDtypeStruct((nb, width), F32)],
        input_output_aliases={5: 0},
        compiler_params=_cparams("arbitrary"),
        name="shortconv_sample",
    )(proj, proj, proj, st, w, yb_all)


def _branch_kernel(ya_ref, yb_ref, wa_ref, wb_ref, ga_ref, gb_ref, o_ref):
    ta = _dot(ya_ref[...], wa_ref[...])
    tb = _dot(yb_ref[...], wb_ref[...])
    o_ref[...] = (_sigmoid(ga_ref[...]) * ta + _sigmoid(gb_ref[...]) * tb).astype(o_ref.dtype)


def _branch(ya, yb, wa, wb, proj, *, col, tm, tn):
    m, k = ya.shape
    n = wa.shape[1]
    return pl.pallas_call(
        _branch_kernel,
        grid=(m // tm, n // tn),
        in_specs=[
            pl.BlockSpec((tm, k), lambda i, j: (i, 0)),
            pl.BlockSpec((tm, yb.shape[1]), lambda i, j: (i, 0)),
            pl.BlockSpec((k, tn), lambda i, j: (0, j)),
            pl.BlockSpec((yb.shape[1], tn), lambda i, j: (0, j)),
            pl.BlockSpec((tm, tn), lambda i, j: (i, col["g_a"] // tn + j)),
            pl.BlockSpec((tm, tn), lambda i, j: (i, col["g_b"] // tn + j)),
        ],
        out_specs=pl.BlockSpec((tm, tn), lambda i, j: (i, j)),
        out_shape=jax.ShapeDtypeStruct((m, n), BF16),
        compiler_params=_cparams("parallel", "parallel"),
        name="branch_mix",
    )(ya, yb, wa, wb, proj, proj)


def _x1_kernel(x_ref, mix_ref, w_ref, g_ref, b_ref, *rest, alpha):
    o_ref, ob_ref = rest[-2:]
    t = alpha * x_ref[...] + _dot(mix_ref[...], w_ref[...])
    x1 = _layer_norm(t, g_ref[...], b_ref[...])
    o_ref[...] = x1
    ob_ref[...] = pltpu.bitcast(x1.astype(BF16), U32)


def _x1(x, mix, w, g, b, prev, *, alpha, row0, tm):
    n, d = x.shape
    m = mix.shape[0]
    r0 = row0 // tm
    const = lambda i: (0, 0)
    alias = [pl.BlockSpec(memory_space=pl.ANY)] * len(prev)
    return pl.pallas_call(
        functools.partial(_x1_kernel, alpha=alpha),
        grid=(n // tm,),
        in_specs=[
            pl.BlockSpec((tm, d), lambda i: (i, 0)),
            pl.BlockSpec((tm, d), lambda i: (r0 + i, 0)),
            pl.BlockSpec(w.shape, const),
            pl.BlockSpec(g.shape, const),
            pl.BlockSpec(b.shape, const),
        ] + alias,
        out_specs=[pl.BlockSpec((tm, d), lambda i: (r0 + i, 0)),
                   pl.BlockSpec((_words(tm), d), lambda i: (r0 + i, 0))],
        out_shape=[jax.ShapeDtypeStruct((m, d), F32), jax.ShapeDtypeStruct((_words(m), d), U32)],
        input_output_aliases={5 + k: k for k in range(len(prev))},
        compiler_params=_cparams("parallel"),
        name="x1_out_ln",
    )(x, mix, w, g, b, *prev)


def _sort_pairs(lo, hi):
    def merge(lo, hi, r):
        step = r * 2
        if step < hi - lo:
            yield from merge(lo, hi, step)
            yield from merge(lo + r, hi, step)
            yield from [(i, i + r) for i in range(lo + r, hi - r, step)]
        else:
            yield (lo, lo + r)
    if hi - lo >= 1:
        mid = lo + (hi - lo) // 2
        yield from _sort_pairs(lo, mid)
        yield from _sort_pairs(mid + 1, hi)
        yield from merge(lo, hi, 1)


def _exchange(v, i, j):
    v[i], v[j] = jnp.maximum(v[i], v[j]), jnp.minimum(v[i], v[j])


def _sublane_all(op, x):
    dist = SUBLANES // 2
    while dist:
        x = op(x, pltpu.roll(x, dist, 0))
        dist //= 2
    return x


def _top_sorted(s, k):
    assert s.shape[0] == k * SUBLANES and k & (k - 1) == 0
    v = [s[j * SUBLANES:(j + 1) * SUBLANES, :] for j in range(k)]
    for i, j in _sort_pairs(0, k - 1):
        _exchange(v, i, j)
    dist = SUBLANES // 2
    while dist:
        w = [pltpu.roll(x, dist, 0) for x in v]
        v = [jnp.maximum(v[j], w[k - 1 - j]) for j in range(k)]
        stride = k // 2
        while stride:
            for i in range(k):
                if not i & stride:
                    _exchange(v, i, i + stride)
            stride //= 2
        dist //= 2
    return v


def _prefix_count(pred, t):
    w = jnp.where
    m1 = pred(t[7])
    m2 = pred(w(m1, t[11], t[3]))
    m3 = pred(w(m1, w(m2, t[13], t[9]), w(m2, t[5], t[1])))
    m4 = pred(w(m1, w(m2, w(m3, t[14], t[12]), w(m3, t[10], t[8])),
                w(m2, w(m3, t[6], t[4]), w(m3, t[2], t[0]))))
    return (w(m1, 8.0, 0.0) + w(m2, 4.0, 0.0) + w(m3, 2.0, 0.0) + w(m4, 1.0, 0.0)
            + w(pred(t[15]), 1.0, 0.0))


def _route_kernel(x_ref, wq_ref, k1_ref, k2_ref, c1_ref, w1_ref, r2_ref, e2_ref, q_scr, s1_scr, s2_scr):
    n_heads, n_keys, dk = k1_ref.shape
    tm = q_scr.shape[0]
    K = PEER_TOPK
    S = SUBLANES
    assert K == 16 and n_keys == K * S
    sub = lax.broadcasted_iota(jnp.int32, (S, LANES), 0)
    q_scr[...] = _dot(pltpu.bitcast(x_ref[...], BF16), wq_ref[...]).astype(BF16)
    for h in range(n_heads):
        q1 = q_scr[:, (2 * h) * dk:(2 * h + 1) * dk]
        q2 = q_scr[:, (2 * h + 1) * dk:(2 * h + 2) * dk]
        s1_scr[...] = lax.dot_general(k1_ref[h], q1, NT_DIMS, preferred_element_type=F32)
        s2_scr[...] = lax.dot_general(k2_ref[h], q2, NT_DIMS, preferred_element_type=F32)
        for c in range(tm // LANES):
            cs = slice(c * LANES, (c + 1) * LANES)
            s1 = [s1_scr[j * S:(j + 1) * S, cs] for j in range(K)]
            s2 = [s2_scr[j * S:(j + 1) * S, cs] for j in range(K)]
            t1 = _top_sorted(s1_scr[:, cs], K)
            t2 = _top_sorted(s2_scr[:, cs], K)
            rows = lambda t: functools.reduce(lambda acc, b: jnp.where(sub == b, t[b], acc), range(1, S), t[0])
            t2lo, t2hi, t1hi = rows(t2[:S]), rows(t2[S:]), rows(t1[S:])
            cand = [t1[0] + t2lo, t1[0] + t2hi] + [t1[a] + t2lo for a in range(1, S)] + [t1hi + t2[0]]
            cur = cand
            for r in range(K):
                thr = _sublane_all(jnp.maximum, functools.reduce(jnp.maximum, cur))
                if r + 1 < K:
                    cur = [jnp.where(x == thr, -jnp.inf, x) for x in cur]
            top = t1[0] + t2[0]
            zsum = _sublane_all(jnp.add, sum(jnp.where(x >= thr, jnp.exp(x - top), 0.0) for x in cand))
            inv_z = 1.0 / zsum
            rank2 = []
            for j in range(K):
                rows8 = slice(j * S, (j + 1) * S)
                c1_ref[h, rows8, cs] = _prefix_count(lambda tv, x=s1[j]: x + tv >= thr, t2)
                w1_ref[h, rows8, cs] = jnp.exp(s1[j] - t1[0]) * inv_z
                rank2.append(_prefix_count(lambda tv, x=s2[j]: tv > x, t2))
            r2_ref[h, :, cs] = pltpu.bitcast(jnp.concatenate(rank2, axis=0).astype(BF16), U32)
            e2 = jnp.concatenate([jnp.exp(x - t2[0]) for x in s2], axis=0)
            e2_ref[h, :, cs] = pltpu.bitcast(e2.astype(BF16), U32)


def _route(x1p, wq, k1, k2, *, m, tm):
    d = x1p.shape[1]
    n_heads, n_keys, _ = k1.shape
    tok = lambda rows: pl.BlockSpec((n_heads, rows, tm), lambda i: (0, 0, i))
    shp = lambda rows, dt: jax.ShapeDtypeStruct((n_heads, rows, m), dt)
    return pl.pallas_call(
        _route_kernel,
        grid=(m // tm,),
        in_specs=[
            pl.BlockSpec((_words(tm), d), lambda i: (i, 0)),
            pl.BlockSpec(wq.shape, lambda i: (0, 0)),
            pl.BlockSpec(k1.shape, lambda i: (0, 0, 0)),
            pl.BlockSpec(k2.shape, lambda i: (0, 0, 0)),
        ],
        out_specs=[tok(n_keys), tok(n_keys), tok(_words(n_keys)), tok(_words(n_keys))],
        out_shape=[shp(n_keys, F32), shp(n_keys, F32), shp(_words(n_keys), U32), shp(_words(n_keys), U32)],
        scratch_shapes=[pltpu.VMEM((tm, wq.shape[1]), BF16),
                        pltpu.VMEM((n_keys, tm), F32), pltpu.VMEM((n_keys, tm), F32)],
        compiler_params=_cparams("parallel"),
        name="peer_route",
    )(x1p, wq, k1, k2)


def _expert_kernel(x_ref, u_ref, vt_ref, c1_ref, w1_ref, r2_ref, e2_ref, o_ref, acc, coef, hid):
    e = pl.program_id(1)
    n_heads, per, tm = c1_ref.shape
    n_keys = hid.shape[0] // per
    half = _words(n_keys)

    @pl.when(e == 0)
    def _():
        acc[...] = jnp.zeros_like(acc)

    hid[...] = lax.dot_general(pltpu.bitcast(u_ref[...], BF16), pltpu.bitcast(x_ref[...], BF16), NT_DIMS,
                               preferred_element_type=F32)
    for k in range(per):
        for c in range(tm // LANES):
            cs = slice(c * LANES, (c + 1) * LANES)
            gate = jnp.zeros((n_keys, LANES), BF16)
            for h in range(n_heads):
                cnt = c1_ref[h, k:k + 1, cs].astype(BF16)
                w1 = w1_ref[h, k:k + 1, cs].astype(BF16)
                r2 = pltpu.bitcast(r2_ref[h, :, cs], BF16)
                e2 = pltpu.bitcast(e2_ref[h, :, cs], BF16)
                gate = gate + jnp.where(r2 < cnt, e2 * w1, jnp.zeros_like(gate))
            hk = hid[k * n_keys:(k + 1) * n_keys, cs]
            gelu = 0.5 * hk * (1.0 + lax.erf(hk * (1.0 / math.sqrt(2.0))))
            coef[k * half:(k + 1) * half, cs] = pltpu.bitcast(gate * gelu.astype(BF16), U32)

    acc[...] += _dot(pltpu.bitcast(vt_ref[...], BF16), pltpu.bitcast(coef[...], BF16))

    @pl.when(e == pl.num_programs(1) - 1)
    def _():
        o_ref[...] = acc[...].T.astype(o_ref.dtype)


def _experts(x1p, u, vt, c1, w1, r2, e2, *, m, tm, te):
    d = x1p.shape[1]
    n_tiles = u.shape[0] // _words(te)
    n_heads, n_keys, _ = c1.shape
    tok = lambda rows: pl.BlockSpec((n_heads, rows, tm), lambda i, e: (0, 0, i))
    per = te // n_keys
    assert per % SUBLANES == 0
    row = pl.BlockSpec((n_heads, per, tm), lambda i, e: (0, e, i))
    return pl.pallas_call(
        _expert_kernel,
        grid=(m // tm, n_tiles),
        in_specs=[
            pl.BlockSpec((_words(tm), d), lambda i, e: (i, 0)),
            pl.BlockSpec((_words(te), d), lambda i, e: (e, 0)),
            pl.BlockSpec((_words(d), te), lambda i, e: (0, e)),
            row, row, tok(_words(n_keys)), tok(_words(n_keys)),
        ],
        out_specs=pl.BlockSpec((tm, d), lambda i, e: (i, 0)),
        out_shape=jax.ShapeDtypeStruct((m, d), BF16),
        scratch_shapes=[pltpu.VMEM((d, tm), F32),
                        pltpu.VMEM((_words(te), tm), U32),
                        pltpu.VMEM((te, tm), F32)],
        compiler_params=_cparams("parallel", "arbitrary"),
        name="peer_experts",
    )(x1p, u, vt, c1, w1, r2, e2)


def _final_kernel(x1_ref, peer_ref, p_ref, wg_ref, wp_ref, g_ref, b_ref, o_ref, *, alpha):
    x2 = _layer_norm(alpha * x1_ref[...] + peer_ref[...], g_ref[...], b_ref[...])
    gate = _sigmoid(_dot(x2.astype(BF16), wg_ref[...]))
    y = x2 + gate * _dot(p_ref[...].astype(BF16), wp_ref[...])
    if len(o_ref.shape) == 3:
        o_ref[:, 0, :] = y
    else:
        o_ref[...] = y


def _final(x1, peer, p, wg, wp, g, b, *, alpha, row0, tm, per_step):
    d = x1.shape[1]
    n = p.shape[0]
    r0 = row0 // tm
    rows = lambda i: (r0 + i, 0)
    const = lambda i: (0, 0)
    if per_step:
        out_shape, out_spec = (n, 1, d), pl.BlockSpec((tm, 1, d), lambda i: (i, 0, 0))
    else:
        out_shape, out_spec = (n, d), pl.BlockSpec((tm, d), lambda i: (i, 0))
    return pl.pallas_call(
        functools.partial(_final_kernel, alpha=alpha),
        grid=(n // tm,),
        in_specs=[
            pl.BlockSpec((tm, d), rows),
            pl.BlockSpec((tm, d), rows),
            pl.BlockSpec((tm, p.shape[1]), lambda i: (i, 0)),
            pl.BlockSpec(wg.shape, const),
            pl.BlockSpec(wp.shape, const),
            pl.BlockSpec(g.shape, const),
            pl.BlockSpec(b.shape, const),
        ],
        out_specs=out_spec,
        out_shape=jax.ShapeDtypeStruct(out_shape, F32),
        compiler_params=_cparams("parallel"),
        name="ln2_ple",
    )(x1, peer, p, wg, wp, g, b)


TOKEN_TILE = 640
PROJ_TILE = 1664
PROJ_COLS = 512
ROW_TILE = 320
FINAL_TILE = 512
EXPERT_TILE = 1024
PEER_TOKEN_TILE = 768


def _pack_kernel(w_ref, o_ref, *, transpose):
    w = w_ref[...]
    o_ref[...] = pltpu.bitcast((w.T if transpose else w).astype(BF16), U32)


def _pack_rows(w, transpose=False, tr=512):
    r, c = w.shape
    if transpose:
        out_shape, out_spec = (_words(c), r), pl.BlockSpec((_words(c), tr), lambda i: (0, i))
    else:
        out_shape, out_spec = (_words(r), c), pl.BlockSpec((_words(tr), c), lambda i: (i, 0))
    return pl.pallas_call(
        functools.partial(_pack_kernel, transpose=transpose),
        grid=(r // tr,),
        in_specs=[pl.BlockSpec((tr, c), lambda i: (i, 0))],
        out_specs=out_spec,
        out_shape=jax.ShapeDtypeStruct(out_shape, U32),
        compiler_params=_cparams("parallel"),
        name="pack_weight",
    )(w)


def _pad_lanes(v):
    return jnp.pad(v.astype(F32), (0, LANES - v.shape[0])).reshape(1, LANES)


def _layer(x_p, x_s, p_p, p_s, ssm_h, conv_buf, sc_buf, batch, seq, depth,
           w_in, ssd_conv_w, ssd_conv_b, ssd_dt_bias, ssd_a_log, ssd_d, ssd_norm_w,
           sc_conv_w, w_branch_ssd, w_branch_sc, w_out, ln1_g, ln1_b,
           peer_wq, peer_keys1, peer_keys2, peer_u, peer_v, ln2_g, ln2_b,
           ple_gate_w, ple_proj_w):
    n_prompt, d = x_p.shape
    nb = x_s.shape[0]
    m = n_prompt + nb
    x = jnp.concatenate([x_p, x_s], axis=0).astype(BF16)
    n_heads = ssd_dt_bias.shape[0]
    hw, n_state = ssm_h.shape[1] * ssm_h.shape[2], ssm_h.shape[3]
    head_dim = ssm_h.shape[2]
    conv_dim = ssd_conv_w.shape[1]
    gn = (conv_dim - hw) // 2
    n_groups = gn // n_state
    scw = sc_conv_w.shape[1]
    alpha = (2.0 * depth) ** 0.25
    assert hw == scw == d and n_heads <= LANES and LANES % head_dim == 0 and n_state == LANES

    o_dt = hw + conv_dim
    o_scb = o_dt + n_heads
    col = {"z": 0, "xs": hw, "B": 2 * hw, "C": 2 * hw + gn}
    col2 = {"sc_b": 0, "sc_c": scw, "sc_h": 2 * scw, "g_a": 3 * scw, "g_b": 3 * scw + d}
    tm = _tile(m, TOKEN_TILE, LANES)
    tmi = _tile(m, PROJ_TILE, LANES)
    w_in_t = w_in.T
    proj = _in_proj(x, w_in_t, col0=0, n=o_dt, tm=tmi)
    proj2 = _in_proj(x, w_in_t, col0=o_scb, n=3 * scw + 2 * d, tm=tmi)
    dtraw = _in_proj(x, w_in_t, col0=o_dt, n=LANES, tm=tmi)

    ch_head = jnp.arange(hw) // head_dim
    e1 = (jnp.arange(LANES)[:, None] == ch_head[None, :]).astype(BF16)
    e2 = (jnp.arange(LANES)[:, None] == (jnp.arange(n_heads * LANES) // LANES)[None, :]).astype(BF16)
    convb = ssd_conv_b.reshape(1, conv_dim)
    dtb, alog = _pad_lanes(ssd_dt_bias), _pad_lanes(ssd_a_log)
    de = jnp.repeat(ssd_d.astype(F32), head_dim).reshape(1, hw)
    normw = ssd_norm_w.reshape(1, hw)
    shp = dict(hw=hw, gn=gn, n_groups=n_groups, n_state=n_state, col=col)
    ya_p, h_p = _ssd_prompt(proj, dtraw, ssd_conv_w, convb, dtb, alog, de, normw, e1, e2,
                            batch=batch, seq=seq, head_dim=head_dim, **shp)
    cst = jnp.transpose(conv_buf, (1, 0, 2))
    ya, h_s = _ssd_sample(proj, dtraw, cst, ssm_h.reshape(nb, hw, n_state), ssd_conv_w, convb,
                          dtb, alog, de, normw, e1, ya_p, row0=n_prompt, nb=nb, **shp)

    yb_p, sc_tail = _sc_prompt(proj2, sc_conv_w, batch=batch, seq=seq, width=scw, col=col2,
                               ts=min(seq, 256))
    yb, ch_s = _sc_sample(proj2, jnp.transpose(sc_buf, (1, 0, 2)), sc_conv_w, yb_p,
                          row0=n_prompt, nb=nb, width=scw, col=col2)

    mix = _branch(ya, yb, w_branch_ssd.astype(BF16), w_branch_sc.astype(BF16), proj2,
                  col=col2, tm=tm, tn=_tile(d, 1024, LANES))
    assert n_prompt % nb == 0
    ln1 = functools.partial(_x1, mix=mix, w=w_out.astype(BF16), g=ln1_g.reshape(1, d), b=ln1_b.reshape(1, d),
                            alpha=alpha)
    x1_parts = ln1(x_p, prev=(), row0=0, tm=_tile(n_prompt, FINAL_TILE, BF16_ROWS))
    x1, x1b = ln1(x_s, prev=tuple(x1_parts), row0=n_prompt, tm=nb)

    tmp = PEER_TOKEN_TILE
    mp = m + (-m % tmp)
    x1p = jnp.pad(x1b, ((0, _words(mp - m)), (0, 0)))
    route = _route(x1p, peer_wq.astype(BF16), peer_keys1.astype(BF16), peer_keys2.astype(BF16), m=mp, tm=tmp)
    peer = _experts(x1p, _pack_rows(peer_u), _pack_rows(peer_v, transpose=True), *route,
                    m=mp, tm=tmp, te=EXPERT_TILE)

    fin = functools.partial(_final, x1, peer, wg=ple_gate_w.astype(BF16), wp=ple_proj_w.astype(BF16),
                            g=ln2_g.reshape(1, d), b=ln2_b.reshape(1, d), alpha=alpha)
    y_p = fin(p=p_p, row0=0, tm=_tile(n_prompt, FINAL_TILE, SUBLANES), per_step=False)
    y_s = fin(p=p_s, row0=n_prompt, tm=nb, per_step=True)

    k_ssd = ssd_conv_w.shape[0] - 1
    k_sc = sc_conv_w.shape[0] - 1
    xbc_cols = lambda rows: rows[..., hw:hw + conv_dim]
    conv_p = xbc_cols(jnp.stack([proj[(b + 1) * seq - k_ssd:(b + 1) * seq] for b in range(batch)]))
    conv_s = jnp.concatenate([conv_buf[:, 1:, :], xbc_cols(proj[n_prompt:])[:, None, :]], axis=1)
    sc_p = sc_tail[:, SUBLANES - k_sc:, :]
    sc_s = jnp.concatenate([sc_buf[:, 1:, :], ch_s[:, None, :]], axis=1)
    hshape = (-1, n_heads, head_dim, n_state)
    return y_p, y_s, conv_p, h_p.reshape(hshape), sc_p, conv_s, h_s.reshape(hshape), sc_s


def kernel(x_prompt, x_sample, p_prompt, p_sample, state_ssm, state_ssd_conv, state_shortconv, w_in, ssd_conv_w, ssd_conv_b, ssd_dt_bias, ssd_a_log, ssd_d, ssd_norm_w, sc_conv_w, w_branch_ssd, w_branch_sc, w_out, ln1_g, ln1_b, peer_wq, peer_keys1, peer_keys2, peer_u, peer_v, ln2_g, ln2_b, ple_gate_w, ple_proj_w):
    batch, seq, d = x_prompt.shape
    nb, dec_seq, _ = x_sample.shape
    assert dec_seq == 1 and seq % SSD_CHUNK == 0
    depth = w_in.shape[0]
    n_prompt = batch * seq
    x_p, x_s = x_prompt.reshape(n_prompt, d), x_sample.reshape(nb, d)
    weights = (w_in, ssd_conv_w, ssd_conv_b, ssd_dt_bias, ssd_a_log, ssd_d, ssd_norm_w,
               sc_conv_w, w_branch_ssd, w_branch_sc, w_out, ln1_g, ln1_b,
               peer_wq, peer_keys1, peer_keys2, peer_u, peer_v, ln2_g, ln2_b,
               ple_gate_w, ple_proj_w)
    outs = [[] for _ in range(6)]
    for i in range(depth):
        y_p, y_s, conv_p, h_p, sc_p, conv_s, h_s, sc_s = _layer(
            x_p, x_s, p_prompt[i].reshape(n_prompt, -1), p_sample[i].reshape(nb, -1),
            state_ssm[i], state_ssd_conv[i], state_shortconv[i], batch, seq, depth,
            *[w[i] for w in weights])
        for lst, val in zip(outs, (h_p, conv_p, sc_p, h_s, conv_s, sc_s)):
            lst.append(val)
        x_p, x_s = y_p, y_s.reshape(nb, d)
    y_prompt = y_p.reshape(batch, seq, d)
    y_sample = y_s
    return (y_prompt, y_sample) + tuple(jnp.stack(lst) for lst in outs)
```

```python
import functools
import math

import jax
import jax.numpy as jnp
from jax import lax
from jax.experimental import pallas as pl
from jax.experimental.pallas import tpu as pltpu

F32 = jnp.float32
BF16 = jnp.bfloat16
U32 = jnp.uint32

LANES = 128
SUBLANES = 8
BF16_ROWS = 16
PEER_TOPK = 16
SSD_CHUNK = 128
LN_EPS = 1e-5
RMS_EPS = 1e-5
VMEM_LIMIT = 56 * 1024 * 1024

NT_DIMS = (((1,), (1,)), ((), ()))
TN_DIMS = (((0,), (0,)), ((), ()))


def _cparams(*sem, flags=None):
    return pltpu.CompilerParams(dimension_semantics=sem, vmem_limit_bytes=VMEM_LIMIT, flags=flags)


def _dot(a, b):
    return jnp.dot(a, b, preferred_element_type=F32)


def _split3(v):
    hi = v.astype(BF16)
    r = v - hi.astype(F32)
    mid = r.astype(BF16)
    lo = (r - mid.astype(F32)).astype(BF16)
    return hi, mid, lo


def _dot3_lhs(v, rhs_bf16):
    hi, mid, lo = _split3(v)
    return _dot(hi, rhs_bf16) + _dot(mid, rhs_bf16) + _dot(lo, rhs_bf16)


def _dot3_rhs(lhs_bf16, v):
    hi, mid, lo = _split3(v)
    return _dot(lhs_bf16, hi) + _dot(lhs_bf16, mid) + _dot(lhs_bf16, lo)


def _words(rows):
    return rows * jnp.dtype(BF16).itemsize // jnp.dtype(U32).itemsize


def _bf16_rows(words):
    return words * jnp.dtype(U32).itemsize // jnp.dtype(BF16).itemsize


def _sigmoid(x):
    return 1.0 / (1.0 + jnp.exp(-x))


def _silu(x):
    return x * _sigmoid(x)


def _softplus(x):
    return jnp.maximum(x, 0.0) + jnp.log1p(jnp.exp(-jnp.abs(x)))


def _layer_norm(x, g, b):
    mu = jnp.mean(x, axis=-1, keepdims=True)
    xc = x - mu
    var = jnp.mean(xc * xc, axis=-1, keepdims=True)
    return xc * lax.rsqrt(var + LN_EPS) * g + b


def _tile(m, cap, mult):
    best = None
    for t in range(mult, min(m, cap) + 1, mult):
        if m % t == 0:
            best = t
    assert best is not None, (m, cap, mult)
    return best


def _in_proj_kernel(x_ref, wa_ref, wb_ref, o_ref, *, shift):
    tn = wa_ref.shape[0]
    if shift:
        w = jnp.concatenate([wa_ref[...], wb_ref[...]], axis=0)[shift:shift + tn]
    else:
        w = wa_ref[...]
    o_ref[...] = lax.dot_general(x_ref[...], w.astype(BF16), NT_DIMS, preferred_element_type=F32)


def _in_proj(x, wt, *, col0, n, tm):
    m, k = x.shape
    shift = col0 % LANES
    assert shift % SUBLANES == 0
    base = col0 - shift
    tn = _tile(math.gcd(n, base) if base else n, PROJ_COLS, LANES)
    last = pl.cdiv(wt.shape[0], LANES) - 1
    return pl.pallas_call(
        functools.partial(_in_proj_kernel, shift=shift),
        grid=(m // tm, n // tn),
        in_specs=[pl.BlockSpec((tm, k), lambda i, j: (i, 0)),
                  pl.BlockSpec((tn, k), lambda i, j: (base // tn + j, 0)),
                  pl.BlockSpec((LANES, k), lambda i, j: (jnp.minimum((base + (j + 1) * tn) // LANES, last), 0))],
        out_specs=pl.BlockSpec((tm, tn), lambda i, j: (i, j)),
        out_shape=jax.ShapeDtypeStruct((m, n), F32),
        compiler_params=_cparams("parallel", "parallel"),
        name="in_proj",
    )(x, wt, wt)


def _ssd_gate_norm(y, z, normw_ref, out_ref, n_groups):
    y = y * _silu(z)
    gw = y.shape[1] // n_groups
    for g in range(n_groups):
        sl = slice(g * gw, (g + 1) * gw)
        yg = y[:, sl]
        ms = jnp.mean(yg * yg, axis=-1, keepdims=True)
        out_ref[:, sl] = (yg * lax.rsqrt(ms + RMS_EPS) * normw_ref[:, sl]).astype(out_ref.dtype)


def _ssd_prompt_kernel(z_ref, xs_ref, b_ref, c_ref, dt_ref, convw_ref, convb_ref, dtb_ref,
                       alog_ref, de_ref, normw_ref, e1_ref, e2_ref,
                       ya_ref, hout_ref,
                       cbuf, act, h_scr, x_scr, xd_scr, eae_scr, acsb_scr, acst_scr, y_scr,
                       *, n_groups, n_state, head_dim):
    c = pl.program_id(1)
    L = SSD_CHUNK
    hw = xs_ref.shape[1]
    gn = b_ref.shape[1]
    w_all = hw + 2 * gn
    n_heads = hw // head_dim
    hpl = LANES // head_dim
    n_blk = n_heads // hpl
    blk_per_group = n_blk // n_groups
    taps = convw_ref.shape[0]

    @pl.when(c == 0)
    def _():
        h_scr[...] = jnp.zeros_like(h_scr)
        cbuf[0:SUBLANES, :] = jnp.zeros((SUBLANES, w_all), F32)

    cbuf[SUBLANES:SUBLANES + L, 0:hw] = xs_ref[...]
    cbuf[SUBLANES:SUBLANES + L, hw:hw + gn] = b_ref[...]
    cbuf[SUBLANES:SUBLANES + L, hw + gn:] = c_ref[...]

    cw = math.gcd(w_all, 512)
    for blk in range(w_all // cw):
        sl = slice(blk * cw, (blk + 1) * cw)
        acc = convb_ref[:, sl] + convw_ref[taps - 1:taps, sl] * cbuf[SUBLANES:SUBLANES + L, sl]
        for j in range(1, taps):
            acc = acc + convw_ref[taps - 1 - j:taps - j, sl] * cbuf[SUBLANES - j:SUBLANES - j + L, sl]
        act[:, sl] = _silu(acc)
    cbuf[0:SUBLANES, :] = cbuf[L:L + SUBLANES, :]

    dt = _softplus(dt_ref[...] + dtb_ref[...])
    a_neg = -jnp.exp(alog_ref[...])
    dta = dt * a_neg
    ri = lax.broadcasted_iota(jnp.int32, (L, L), 0)
    ci = lax.broadcasted_iota(jnp.int32, (L, L), 1)
    causal = ri >= ci
    tri = jnp.where(causal, 1.0, 0.0).astype(BF16)
    acs = _dot3_rhs(tri, dta)
    e1 = e1_ref[...]
    dte = _dot3_lhs(dt, e1)
    acs_p = _split3(acs)
    acse = _dot(acs_p[0], e1) + _dot(acs_p[1], e1) + _dot(acs_p[2], e1)
    e2 = e2_ref[...]
    acsb_scr[...] = _dot(acs_p[0], e2) + _dot(acs_p[1], e2) + _dot(acs_p[2], e2)
    acst_scr[...] = acs.T

    xdt = act[:, 0:hw] * dte
    x_scr[...] = xdt.astype(BF16)
    xd_scr[...] = (xdt * jnp.exp(acse[L - 1:L, :] - acse)).astype(BF16)
    eae_scr[...] = jnp.exp(acse)

    lane = lax.broadcasted_iota(jnp.int32, (L, LANES), 1)
    cb = None
    for j in range(n_blk):
        g = j // blk_per_group
        bsl = slice(hw + g * n_state, hw + (g + 1) * n_state)
        csl = slice(hw + gn + g * n_state, hw + gn + (g + 1) * n_state)
        bg = act[:, bsl].astype(BF16)
        cg = act[:, csl].astype(BF16)
        if j % blk_per_group == 0:
            cb = lax.dot_general(cg, bg, NT_DIMS, preferred_element_type=F32)
        psl = slice(j * LANES, (j + 1) * LANES)
        xp = x_scr[:, psl]
        ydiag = None
        cds = []
        for q in range(hpl):
            r = j * hpl + q
            ab = acsb_scr[:, r * LANES:(r + 1) * LANES]
            at = jnp.broadcast_to(acst_scr[r:r + 1, :], (L, L))
            lm = jnp.where(causal, jnp.exp(ab - at), 0.0)
            m = (cb * lm).astype(BF16)
            inhead = (lane >= q * head_dim) & (lane < (q + 1) * head_dim)
            xq = jnp.where(inhead, xp, jnp.zeros_like(xp))
            yq = _dot(m, xq)
            ydiag = yq if ydiag is None else ydiag + yq
            cds.append(jnp.broadcast_to(jnp.exp(acsb_scr[L - 1:L, r * LANES:(r + 1) * LANES]),
                                        (head_dim, LANES)))
        cd = jnp.concatenate(cds, axis=0)
        hp = h_scr[psl, :]
        yoff = lax.dot_general(cg, hp.astype(BF16), NT_DIMS, preferred_element_type=F32)
        yoff = yoff * eae_scr[:, psl]
        st = lax.dot_general(xd_scr[:, psl], bg, TN_DIMS, preferred_element_type=F32)
        h_scr[psl, :] = hp * cd + st
        y_scr[:, psl] = ydiag + yoff + act[:, psl] * de_ref[:, psl]

    _ssd_gate_norm(y_scr[...], z_ref[...], normw_ref, ya_ref, n_groups)

    @pl.when(c == pl.num_programs(1) - 1)
    def _():
        hout_ref[0] = h_scr[...]


def _ssd_prompt(proj, dtraw, convw, convb, dtb, alog, de, normw, e1, e2, *, batch, seq,
                hw, gn, n_groups, n_state, head_dim, col):
    L = SSD_CHUNK
    nc = seq // L
    n_heads = hw // head_dim
    w_all = hw + 2 * gn
    row = lambda b, c: b * nc + c
    const = lambda b, c: (0, 0)
    kern = functools.partial(_ssd_prompt_kernel, n_groups=n_groups, n_state=n_state, head_dim=head_dim)
    return pl.pallas_call(
        kern,
        grid=(batch, nc),
        in_specs=[
            pl.BlockSpec((L, hw), lambda b, c: (row(b, c), col["z"] // hw)),
            pl.BlockSpec((L, hw), lambda b, c: (row(b, c), col["xs"] // hw)),
            pl.BlockSpec((L, gn), lambda b, c: (row(b, c), col["B"] // gn)),
            pl.BlockSpec((L, gn), lambda b, c: (row(b, c), col["C"] // gn)),
            pl.BlockSpec((L, LANES), lambda b, c: (row(b, c), 0)),
            pl.BlockSpec(convw.shape, const),
            pl.BlockSpec(convb.shape, const),
            pl.BlockSpec(dtb.shape, const),
            pl.BlockSpec(alog.shape, const),
            pl.BlockSpec(de.shape, const),
            pl.BlockSpec(normw.shape, const),
            pl.BlockSpec(e1.shape, const),
            pl.BlockSpec(e2.shape, const),
        ],
        out_specs=[
            pl.BlockSpec((L, hw), lambda b, c: (row(b, c), 0)),
            pl.BlockSpec((1, hw, n_state), lambda b, c: (b, 0, 0)),
        ],
        out_shape=[
            jax.ShapeDtypeStruct((proj.shape[0], hw), BF16),
            jax.ShapeDtypeStruct((batch, hw, n_state), F32),
        ],
        scratch_shapes=[
            pltpu.VMEM((L + SUBLANES, w_all), F32),
            pltpu.VMEM((L, w_all), F32),
            pltpu.VMEM((hw, n_state), F32),
            pltpu.VMEM((L, hw), BF16),
            pltpu.VMEM((L, hw), BF16),
            pltpu.VMEM((L, hw), F32),
            pltpu.VMEM((L, n_heads * LANES), F32),
            pltpu.VMEM((L, L), F32),
            pltpu.VMEM((L, hw), F32),
        ],
        compiler_params=_cparams("parallel", "arbitrary"),
        name="ssd_prompt",
    )(proj, proj, proj, proj, dtraw, convw, convb, dtb, alog, de, normw, e1, e2)


def _ssd_sample_kernel(z_ref, xs_ref, b_ref, c_ref, dt_ref, cst_ref, h_ref, convw_ref, convb_ref,
                       dtb_ref, alog_ref, de_ref, normw_ref, e1_ref, ya_all_ref,
                       ya_ref, hout_ref, xbc, y_scr,
                       *, n_groups, n_state):
    bb = xs_ref.shape[0]
    hw = xs_ref.shape[1]
    gn = b_ref.shape[1]
    taps = convw_ref.shape[0]
    gw = hw // n_groups

    xbc[:, 0:hw] = xs_ref[...]
    xbc[:, hw:hw + gn] = b_ref[...]
    xbc[:, hw + gn:] = c_ref[...]
    acc = convb_ref[...] + convw_ref[taps - 1:taps, :] * xbc[...]
    for j in range(taps - 1):
        acc = acc + convw_ref[j:j + 1, :] * cst_ref[j]
    act = _silu(acc)
    xs = act[:, 0:hw]

    dt = _softplus(dt_ref[...] + dtb_ref[...])
    dec = jnp.exp(dt * (-jnp.exp(alog_ref[...])))
    e1 = e1_ref[...]
    dte = _dot3_lhs(dt, e1)
    dece = _dot3_lhs(dec, e1)
    xdt = xs * dte

    pieces = [p.astype(F32) for p in _split3(dece)] + [p.astype(F32) for p in _split3(xdt)]
    npc = len(pieces)
    stack = jnp.concatenate(pieces + [jnp.zeros((LANES - npc * bb, hw), F32)], axis=0)
    lt = stack.T.astype(BF16)

    krow = lax.broadcasted_iota(jnp.int32, (LANES, LANES), 0)
    rowid = lax.broadcasted_iota(jnp.int32, (bb, gw), 0)
    half = npc // 2
    y_scr[...] = jnp.zeros_like(y_scr)
    for s in range(bb):
        is_s = (krow % bb) == s
        sel_dec = jnp.where(is_s & (krow < half * bb), 1.0, 0.0).astype(BF16)
        sel_x = jnp.where(is_s & (krow >= half * bb) & (krow < npc * bb), 1.0, 0.0).astype(BF16)
        dec_b = _dot(lt, sel_dec)
        x_b = _dot(lt, sel_x)
        for g in range(n_groups):
            rows = slice(g * gw, (g + 1) * gw)
            brow = act[s:s + 1, hw + g * n_state:hw + (g + 1) * n_state]
            hn = h_ref[s, rows, :] * dec_b[rows, :] + x_b[rows, :] * brow
            hout_ref[s, rows, :] = hn
            cg = act[:, hw + gn + g * n_state:hw + gn + (g + 1) * n_state].astype(BF16)
            yg = lax.dot_general(cg, hn.astype(BF16), NT_DIMS, preferred_element_type=F32)
            y_scr[:, rows] = y_scr[:, rows] + jnp.where(rowid == s, yg, 0.0)

    y = y_scr[...] + xs * de_ref[...]
    _ssd_gate_norm(y, z_ref[...], normw_ref, ya_ref, n_groups)


def _ssd_sample(proj, dtraw, cst, h0, convw, convb, dtb, alog, de, normw, e1, ya_all, *, row0, nb,
                hw, gn, n_groups, n_state, col):
    bb = SUBLANES
    w_all = hw + 2 * gn
    r0 = row0 // bb
    const = lambda i: (0, 0)
    kern = functools.partial(_ssd_sample_kernel, n_groups=n_groups, n_state=n_state)
    return pl.pallas_call(
        kern,
        grid=(nb // bb,),
        in_specs=[
            pl.BlockSpec((bb, hw), lambda i: (r0 + i, col["z"] // hw)),
            pl.BlockSpec((bb, hw), lambda i: (r0 + i, col["xs"] // hw)),
            pl.BlockSpec((bb, gn), lambda i: (r0 + i, col["B"] // gn)),
            pl.BlockSpec((bb, gn), lambda i: (r0 + i, col["C"] // gn)),
            pl.BlockSpec((bb, LANES), lambda i: (r0 + i, 0)),
            pl.BlockSpec((cst.shape[0], bb, w_all), lambda i: (0, i, 0)),
            pl.BlockSpec((bb, hw, n_state), lambda i: (i, 0, 0)),
            pl.BlockSpec(convw.shape, const),
            pl.BlockSpec(convb.shape, const),
            pl.BlockSpec(dtb.shape, const),
            pl.BlockSpec(alog.shape, const),
            pl.BlockSpec(de.shape, const),
            pl.BlockSpec(normw.shape, const),
            pl.BlockSpec(e1.shape, const),
            pl.BlockSpec(memory_space=pl.ANY),
        ],
        out_specs=[
            pl.BlockSpec((bb, hw), lambda i: (r0 + i, 0)),
            pl.BlockSpec((bb, hw, n_state), lambda i: (i, 0, 0)),
        ],
        out_shape=[
            jax.ShapeDtypeStruct(ya_all.shape, BF16),
            jax.ShapeDtypeStruct((nb, hw, n_state), F32),
        ],
        scratch_shapes=[pltpu.VMEM((bb, w_all), F32), pltpu.VMEM((bb, hw), F32)],
        compiler_params=_cparams("parallel"),
        input_output_aliases={14: 0},
        name="ssd_sample",
    )(proj, proj, proj, proj, dtraw, cst, h0, convw, convb, dtb, alog, de, normw, e1, ya_all)


def _sc_prompt_kernel(b_ref, c_ref, h_ref, w_ref, yb_ref, tail_ref, cbuf):
    j = pl.program_id(1)
    ts = b_ref.shape[0]
    taps = w_ref.shape[0]

    @pl.when(j == 0)
    def _():
        cbuf[0:SUBLANES, :] = jnp.zeros((SUBLANES, cbuf.shape[1]), F32)

    cbuf[SUBLANES:SUBLANES + ts, :] = c_ref[...] * h_ref[...]
    u = w_ref[taps - 1:taps, :] * cbuf[SUBLANES:SUBLANES + ts, :]
    for k in range(1, taps):
        u = u + w_ref[taps - 1 - k:taps - k, :] * cbuf[SUBLANES - k:SUBLANES - k + ts, :]
    yb_ref[...] = (b_ref[...] * u).astype(yb_ref.dtype)
    cbuf[0:SUBLANES, :] = cbuf[ts:ts + SUBLANES, :]

    @pl.when(j == pl.num_programs(1) - 1)
    def _():
        tail_ref[0] = cbuf[0:SUBLANES, :]


def _sc_prompt(proj, w, *, batch, seq, width, col, ts):
    nt = seq // ts
    row = lambda b, j: b * nt + j
    return pl.pallas_call(
        _sc_prompt_kernel,
        grid=(batch, nt),
        in_specs=[
            pl.BlockSpec((ts, width), lambda b, j: (row(b, j), col["sc_b"] // width)),
            pl.BlockSpec((ts, width), lambda b, j: (row(b, j), col["sc_c"] // width)),
            pl.BlockSpec((ts, width), lambda b, j: (row(b, j), col["sc_h"] // width)),
            pl.BlockSpec(w.shape, lambda b, j: (0, 0)),
        ],
        out_specs=[
            pl.BlockSpec((ts, width), lambda b, j: (row(b, j), 0)),
            pl.BlockSpec((1, SUBLANES, width), lambda b, j: (b, 0, 0)),
        ],
        out_shape=[
            jax.ShapeDtypeStruct((proj.shape[0], width), BF16),
            jax.ShapeDtypeStruct((batch, SUBLANES, width), F32),
        ],
        scratch_shapes=[pltpu.VMEM((ts + SUBLANES, width), F32)],
        compiler_params=_cparams("parallel", "arbitrary"),
        name="shortconv_prompt",
    )(proj, proj, proj, w)


def _sc_sample_kernel(b_ref, c_ref, h_ref, st_ref, w_ref, yb_all_ref, yb_ref, ch_ref):
    taps = w_ref.shape[0]
    ch = c_ref[...] * h_ref[...]
    u = w_ref[taps - 1:taps, :] * ch
    for k in range(taps - 1):
        u = u + w_ref[k:k + 1, :] * st_ref[k]
    yb_ref[...] = (b_ref[...] * u).astype(yb_ref.dtype)
    ch_ref[...] = ch


def _sc_sample(proj, st, w, yb_all, *, row0, nb, width, col):
    r0 = row0 // nb
    return pl.pallas_call(
        _sc_sample_kernel,
        grid=(1,),
        in_specs=[
            pl.BlockSpec((nb, width), lambda i: (r0, col["sc_b"] // width)),
            pl.BlockSpec((nb, width), lambda i: (r0, col["sc_c"] // width)),
            pl.BlockSpec((nb, width), lambda i: (r0, col["sc_h"] // width)),
            pl.BlockSpec(st.shape, lambda i: (0, 0, 0)),
            pl.BlockSpec(w.shape, lambda i: (0, 0)),
            pl.BlockSpec(memory_space=pl.ANY),
        ],
        out_specs=[pl.BlockSpec((nb, width), lambda i: (r0, 0)),
                   pl.BlockSpec((nb, width), lambda i: (0, 0))],
        out_shape=[jax.ShapeDtypeStruct(yb_all.shape, BF16),
                   jax.ShapeDtypeStruct((nb, width), F32)],
        input_output_aliases={5: 0},
        compiler_params=_cparams("arbitrary"),
        name="shortconv_sample",
    )(proj, proj, proj, st, w, yb_all)


def _branch_kernel(ya_ref, yb_ref, wa_ref, wb_ref, ga_ref, gb_ref, o_ref):
    ta = _dot(ya_ref[...], wa_ref[...])
    tb = _dot(yb_ref[...], wb_ref[...])
    o_ref[...] = (_sigmoid(ga_ref[...]) * ta + _sigmoid(gb_ref[...]) * tb).astype(o_ref.dtype)


def _branch(ya, yb, wa, wb, proj, *, col, tm, tn):
    m, k = ya.shape
    n = wa.shape[1]
    return pl.pallas_call(
        _branch_kernel,
        grid=(m // tm, n // tn),
        in_specs=[
            pl.BlockSpec((tm, k), lambda i, j: (i, 0)),
            pl.BlockSpec((tm, yb.shape[1]), lambda i, j: (i, 0)),
            pl.BlockSpec((k, tn), lambda i, j: (0, j)),
            pl.BlockSpec((yb.shape[1], tn), lambda i, j: (0, j)),
            pl.BlockSpec((tm, tn), lambda i, j: (i, col["g_a"] // tn + j)),
            pl.BlockSpec((tm, tn), lambda i, j: (i, col["g_b"] // tn + j)),
        ],
        out_specs=pl.BlockSpec((tm, tn), lambda i, j: (i, j)),
        out_shape=jax.ShapeDtypeStruct((m, n), BF16),
        compiler_params=_cparams("parallel", "parallel"),
        name="branch_mix",
    )(ya, yb, wa, wb, proj, proj)


def _x1_kernel(x_ref, mix_ref, w_ref, g_ref, b_ref, *rest, alpha):
    o_ref, ob_ref = rest[-2:]
    t = alpha * x_ref[...] + _dot(mix_ref[...], w_ref[...])
    x1 = _layer_norm(t, g_ref[...], b_ref[...])
    o_ref[...] = x1
    nw = _words(x1.shape[0])
    ob_ref[0:nw, :] = pltpu.bitcast(x1.astype(BF16), U32)
    if ob_ref.shape[0] > nw:
        pad = jnp.zeros((_bf16_rows(ob_ref.shape[0] - nw), ob_ref.shape[1]), BF16)
        ob_ref[nw:, :] = pltpu.bitcast(pad, U32)


def _x1(x, mix, w, g, b, prev, *, alpha, row0, tm, mp, pad_tail=False):
    n, d = x.shape
    m = mix.shape[0]
    r0 = row0 // tm
    tb = tm + (mp - m if pad_tail else 0)
    assert row0 % tb == 0 and (not pad_tail or n == tm)
    const = lambda i: (0, 0)
    alias = [pl.BlockSpec(memory_space=pl.ANY)] * len(prev)
    return pl.pallas_call(
        functools.partial(_x1_kernel, alpha=alpha),
        grid=(n // tm,),
        in_specs=[
            pl.BlockSpec((tm, d), lambda i: (i, 0)),
            pl.BlockSpec((tm, d), lambda i: (r0 + i, 0)),
            pl.BlockSpec(w.shape, const),
            pl.BlockSpec(g.shape, const),
            pl.BlockSpec(b.shape, const),
        ] + alias,
        out_specs=[pl.BlockSpec((tm, d), lambda i: (r0 + i, 0)),
                   pl.BlockSpec((_words(tb), d), lambda i: (row0 // tb + i, 0))],
        out_shape=[jax.ShapeDtypeStruct((m, d), F32), jax.ShapeDtypeStruct((_words(mp), d), U32)],
        input_output_aliases={5 + k: k for k in range(len(prev))},
        compiler_params=_cparams("parallel"),
        name="x1_out_ln",
    )(x, mix, w, g, b, *prev)


def _sort_pairs(lo, hi):
    def merge(lo, hi, r):
        step = r * 2
        if step < hi - lo:
            yield from merge(lo, hi, step)
            yield from merge(lo + r, hi, step)
            yield from [(i, i + r) for i in range(lo + r, hi - r, step)]
        else:
            yield (lo, lo + r)
    if hi - lo >= 1:
        mid = lo + (hi - lo) // 2
        yield from _sort_pairs(lo, mid)
        yield from _sort_pairs(mid + 1, hi)
        yield from merge(lo, hi, 1)


def _exchange(v, i, j):
    v[i], v[j] = jnp.maximum(v[i], v[j]), jnp.minimum(v[i], v[j])


def _sublane_all(op, x):
    dist = SUBLANES // 2
    while dist:
        x = op(x, pltpu.roll(x, dist, 0))
        dist //= 2
    return x


def _top_sorted(s, k):
    assert s.shape[0] == k * SUBLANES and k & (k - 1) == 0
    v = [s[j * SUBLANES:(j + 1) * SUBLANES, :] for j in range(k)]
    for i, j in _sort_pairs(0, k - 1):
        _exchange(v, i, j)
    dist = SUBLANES // 2
    while dist:
        w = [pltpu.roll(x, dist, 0) for x in v]
        v = [jnp.maximum(v[j], w[k - 1 - j]) for j in range(k)]
        stride = k // 2
        while stride:
            for i in range(k):
                if not i & stride:
                    _exchange(v, i, i + stride)
            stride //= 2
        dist //= 2
    return v


def _prefix_count(pred, t):
    w = jnp.where
    m1 = pred(t[7])
    m2 = pred(w(m1, t[11], t[3]))
    m3 = pred(w(m1, w(m2, t[13], t[9]), w(m2, t[5], t[1])))
    m4 = pred(w(m1, w(m2, w(m3, t[14], t[12]), w(m3, t[10], t[8])),
                w(m2, w(m3, t[6], t[4]), w(m3, t[2], t[0]))))
    return (w(m1, 8.0, 0.0) + w(m2, 4.0, 0.0) + w(m3, 2.0, 0.0) + w(m4, 1.0, 0.0)
            + w(pred(t[15]), 1.0, 0.0))


def _route_kernel(x_ref, wq_ref, k1_ref, k2_ref, c1_ref, w1_ref, r2_ref, e2_ref, q_scr, s1_scr, s2_scr):
    n_heads, n_keys, dk = k1_ref.shape
    tm = q_scr.shape[0]
    K = PEER_TOPK
    S = SUBLANES
    assert K == 16 and n_keys == K * S
    sub = lax.broadcasted_iota(jnp.int32, (S, LANES), 0)
    q_scr[...] = _dot(pltpu.bitcast(x_ref[...], BF16), wq_ref[...]).astype(BF16)
    for h in range(n_heads):
        q1 = q_scr[:, (2 * h) * dk:(2 * h + 1) * dk]
        q2 = q_scr[:, (2 * h + 1) * dk:(2 * h + 2) * dk]
        s1_scr[...] = lax.dot_general(k1_ref[h], q1, NT_DIMS, preferred_element_type=F32)
        s2_scr[...] = lax.dot_general(k2_ref[h], q2, NT_DIMS, preferred_element_type=F32)
        for c in range(tm // LANES):
            cs = slice(c * LANES, (c + 1) * LANES)
            s1 = [s1_scr[j * S:(j + 1) * S, cs] for j in range(K)]
            s2 = [s2_scr[j * S:(j + 1) * S, cs] for j in range(K)]
            t1 = _top_sorted(s1_scr[:, cs], K)
            t2 = _top_sorted(s2_scr[:, cs], K)
            rows = lambda t: functools.reduce(lambda acc, b: jnp.where(sub == b, t[b], acc), range(1, S), t[0])
            t2lo, t2hi, t1hi = rows(t2[:S]), rows(t2[S:]), rows(t1[S:])
            cand = [t1[0] + t2lo, t1[0] + t2hi] + [t1[a] + t2lo for a in range(1, S)] + [t1hi + t2[0]]
            cur = cand
            for r in range(K):
                thr = _sublane_all(jnp.maximum, functools.reduce(jnp.maximum, cur))
                if r + 1 < K:
                    cur = [jnp.where(x == thr, -jnp.inf, x) for x in cur]
            top = t1[0] + t2[0]
            zsum = _sublane_all(jnp.add, sum(jnp.where(x >= thr, jnp.exp(x - top), 0.0) for x in cand))
            inv_z = 1.0 / zsum
            rank2 = []
            for j in range(K):
                rows8 = slice(j * S, (j + 1) * S)
                c1_ref[h, rows8, cs] = _prefix_count(lambda tv, x=s1[j]: x + tv >= thr, t2)
                w1_ref[h, rows8, cs] = jnp.exp(s1[j] - t1[0]) * inv_z
                rank2.append(_prefix_count(lambda tv, x=s2[j]: tv > x, t2))
            r2_ref[h, :, cs] = pltpu.bitcast(jnp.concatenate(rank2, axis=0).astype(BF16), U32)
            e2 = jnp.concatenate([jnp.exp(x - t2[0]) for x in s2], axis=0)
            e2_ref[h, :, cs] = pltpu.bitcast(e2.astype(BF16), U32)


def _route(x1p, wq, k1, k2, *, m, tm):
    d = x1p.shape[1]
    n_heads, n_keys, _ = k1.shape
    tok = lambda rows: pl.BlockSpec((n_heads, rows, tm), lambda i: (0, 0, i))
    shp = lambda rows, dt: jax.ShapeDtypeStruct((n_heads, rows, m), dt)
    return pl.pallas_call(
        _route_kernel,
        grid=(m // tm,),
        in_specs=[
            pl.BlockSpec((_words(tm), d), lambda i: (i, 0)),
            pl.BlockSpec(wq.shape, lambda i: (0, 0)),
            pl.BlockSpec(k1.shape, lambda i: (0, 0, 0)),
            pl.BlockSpec(k2.shape, lambda i: (0, 0, 0)),
        ],
        out_specs=[tok(n_keys), tok(n_keys), tok(_words(n_keys)), tok(_words(n_keys))],
        out_shape=[shp(n_keys, F32), shp(n_keys, F32), shp(_words(n_keys), U32), shp(_words(n_keys), U32)],
        scratch_shapes=[pltpu.VMEM((tm, wq.shape[1]), BF16),
                        pltpu.VMEM((n_keys, tm), F32), pltpu.VMEM((n_keys, tm), F32)],
        compiler_params=_cparams("parallel"),
        name="peer_route",
    )(x1p, wq, k1, k2)


def _expert_kernel(x_ref, u_ref, vt_ref, c1_ref, w1_ref, r2_ref, e2_ref, o_ref, acc, coef, hid):
    e = pl.program_id(1)
    n_heads, per, tm = c1_ref.shape
    n_keys = hid.shape[0] // per
    half = _words(n_keys)

    @pl.when(e == 0)
    def _():
        acc[...] = jnp.zeros_like(acc)

    xb = pltpu.bitcast(x_ref[...], BF16)
    te = hid.shape[0]
    for part in range(HID_PARTS):
        rows = slice(part * te // HID_PARTS, (part + 1) * te // HID_PARTS)
        wrows = slice(_words(rows.start), _words(rows.stop))
        hid[rows, :] = lax.dot_general(pltpu.bitcast(u_ref[wrows, :], BF16), xb, NT_DIMS,
                                       preferred_element_type=F32)
    for k in range(per):
        for c in range(tm // LANES):
            cs = slice(c * LANES, (c + 1) * LANES)
            gate = jnp.zeros((n_keys, LANES), BF16)
            for h in range(n_heads):
                cnt = c1_ref[h, k:k + 1, cs].astype(BF16)
                w1 = w1_ref[h, k:k + 1, cs].astype(BF16)
                r2 = pltpu.bitcast(r2_ref[h, :, cs], BF16)
                e2 = pltpu.bitcast(e2_ref[h, :, cs], BF16)
                gate = gate + jnp.where(r2 < cnt, e2 * w1, jnp.zeros_like(gate))
            hk = hid[k * n_keys:(k + 1) * n_keys, cs]
            gelu = 0.5 * hk * (1.0 + lax.erf(hk * (1.0 / math.sqrt(2.0))))
            coef[k * half:(k + 1) * half, cs] = pltpu.bitcast(gate * gelu.astype(BF16), U32)

    acc[...] += _dot(pltpu.bitcast(vt_ref[...], BF16), pltpu.bitcast(coef[...], BF16))

    @pl.when(e == pl.num_programs(1) - 1)
    def _():
        o_ref[...] = acc[...].T.astype(o_ref.dtype)


def _experts(x1p, u, vt, c1, w1, r2, e2, *, m, tm, te):
    d = x1p.shape[1]
    n_tiles = u.shape[0] // _words(te)
    n_heads, n_keys, _ = c1.shape
    tok = lambda rows: pl.BlockSpec((n_heads, rows, tm), lambda i, e: (0, 0, i))
    per = te // n_keys
    assert per % SUBLANES == 0
    row = pl.BlockSpec((n_heads, per, tm), lambda i, e: (0, e, i))
    return pl.pallas_call(
        _expert_kernel,
        grid=(m // tm, n_tiles),
        in_specs=[
            pl.BlockSpec((_words(tm), d), lambda i, e: (i, 0)),
            pl.BlockSpec((_words(te), d), lambda i, e: (e, 0)),
            pl.BlockSpec((_words(d), te), lambda i, e: (0, e)),
            row, row, tok(_words(n_keys)), tok(_words(n_keys)),
        ],
        out_specs=pl.BlockSpec((tm, d), lambda i, e: (i, 0)),
        out_shape=jax.ShapeDtypeStruct((m, d), BF16),
        scratch_shapes=[pltpu.VMEM((d, tm), F32),
                        pltpu.VMEM((_words(te), tm), U32),
                        pltpu.VMEM((te, tm), F32)],
        compiler_params=_cparams("parallel", "arbitrary"),
        name="peer_experts",
    )(x1p, u, vt, c1, w1, r2, e2)


def _final_kernel(x1_ref, peer_ref, p_ref, wg_ref, wp_ref, g_ref, b_ref, o_ref, *, alpha):
    x2 = _layer_norm(alpha * x1_ref[...] + peer_ref[...], g_ref[...], b_ref[...])
    gate = _sigmoid(_dot(x2.astype(BF16), wg_ref[...]))
    y = x2 + gate * _dot(p_ref[...].astype(BF16), wp_ref[...])
    if len(o_ref.shape) == 3:
        o_ref[:, 0, :] = y
    else:
        o_ref[...] = y


def _final(x1, peer, p, wg, wp, g, b, *, alpha, row0, tm, per_step):
    d = x1.shape[1]
    n = p.shape[0]
    r0 = row0 // tm
    rows = lambda i: (r0 + i, 0)
    const = lambda i: (0, 0)
    if per_step:
        out_shape, out_spec = (n, 1, d), pl.BlockSpec((tm, 1, d), lambda i: (i, 0, 0))
    else:
        out_shape, out_spec = (n, d), pl.BlockSpec((tm, d), lambda i: (i, 0))
    return pl.pallas_call(
        functools.partial(_final_kernel, alpha=alpha),
        grid=(n // tm,),
        in_specs=[
            pl.BlockSpec((tm, d), rows),
            pl.BlockSpec((tm, d), rows),
            pl.BlockSpec((tm, p.shape[1]), lambda i: (i, 0)),
            pl.BlockSpec(wg.shape, const),
            pl.BlockSpec(wp.shape, const),
            pl.BlockSpec(g.shape, const),
            pl.BlockSpec(b.shape, const),
        ],
        out_specs=out_spec,
        out_shape=jax.ShapeDtypeStruct(out_shape, F32),
        compiler_params=_cparams("parallel"),
        name="ln2_ple",
    )(x1, peer, p, wg, wp, g, b)


TOKEN_TILE = 640
PROJ_TILE = 1664
PROJ_COLS = 512
ROW_TILE = 320
FINAL_TILE = 512
EXPERT_TILE = 1024
HID_PARTS = 2
PEER_TOKEN_TILE = 768


def _pack_kernel(w_ref, o_ref, *, transpose):
    w = w_ref[...]
    o_ref[...] = pltpu.bitcast((w.T if transpose else w).astype(BF16), U32)


def _pack_rows(w, transpose=False, tr=512):
    r, c = w.shape
    if transpose:
        out_shape, out_spec = (_words(c), r), pl.BlockSpec((_words(c), tr), lambda i: (0, i))
    else:
        out_shape, out_spec = (_words(r), c), pl.BlockSpec((_words(tr), c), lambda i: (i, 0))
    return pl.pallas_call(
        functools.partial(_pack_kernel, transpose=transpose),
        grid=(r // tr,),
        in_specs=[pl.BlockSpec((tr, c), lambda i: (i, 0))],
        out_specs=out_spec,
        out_shape=jax.ShapeDtypeStruct(out_shape, U32),
        compiler_params=_cparams("parallel"),
        name="pack_weight",
    )(w)


def _to_bf16_kernel(x_ref, *rest):
    rest[-1][...] = x_ref[...].astype(BF16)


def _to_bf16(x, prev, *, row0, m, tm):
    n, d = x.shape
    r0 = row0 // tm
    assert row0 % tm == 0
    return pl.pallas_call(
        _to_bf16_kernel,
        grid=(n // tm,),
        in_specs=[pl.BlockSpec((tm, d), lambda i: (i, 0))] + [pl.BlockSpec(memory_space=pl.ANY)] * len(prev),
        out_specs=pl.BlockSpec((tm, d), lambda i: (r0 + i, 0)),
        out_shape=jax.ShapeDtypeStruct((m, d), BF16),
        input_output_aliases={1 + k: k for k in range(len(prev))},
        compiler_params=_cparams("parallel"),
        name="x_bf16",
    )(x, *prev)


def _pad_lanes(v):
    return jnp.pad(v.astype(F32), (0, LANES - v.shape[0])).reshape(1, LANES)


def _layer(x_p, x_s, p_p, p_s, ssm_h, conv_buf, sc_buf, batch, seq, depth,
           w_in, ssd_conv_w, ssd_conv_b, ssd_dt_bias, ssd_a_log, ssd_d, ssd_norm_w,
           sc_conv_w, w_branch_ssd, w_branch_sc, w_out, ln1_g, ln1_b,
           peer_wq, peer_keys1, peer_keys2, peer_u, peer_v, ln2_g, ln2_b,
           ple_gate_w, ple_proj_w):
    n_prompt, d = x_p.shape
    nb = x_s.shape[0]
    m = n_prompt + nb
    x = _to_bf16(x_s, (_to_bf16(x_p, (), row0=0, m=m, tm=_tile(n_prompt, FINAL_TILE, BF16_ROWS)),),
                 row0=n_prompt, m=m, tm=nb)
    n_heads = ssd_dt_bias.shape[0]
    hw, n_state = ssm_h.shape[1] * ssm_h.shape[2], ssm_h.shape[3]
    head_dim = ssm_h.shape[2]
    conv_dim = ssd_conv_w.shape[1]
    gn = (conv_dim - hw) // 2
    n_groups = gn // n_state
    scw = sc_conv_w.shape[1]
    alpha = (2.0 * depth) ** 0.25
    assert hw == scw == d and n_heads <= LANES and LANES % head_dim == 0 and n_state == LANES

    o_dt = hw + conv_dim
    o_scb = o_dt + n_heads
    col = {"z": 0, "xs": hw, "B": 2 * hw, "C": 2 * hw + gn}
    col2 = {"sc_b": 0, "sc_c": scw, "sc_h": 2 * scw, "g_a": 3 * scw, "g_b": 3 * scw + d}
    tm = _tile(m, TOKEN_TILE, LANES)
    tmi = _tile(m, PROJ_TILE, LANES)
    w_in_t = w_in.T
    proj = _in_proj(x, w_in_t, col0=0, n=o_dt, tm=tmi)
    proj2 = _in_proj(x, w_in_t, col0=o_scb, n=3 * scw + 2 * d, tm=tmi)
    dtraw = _in_proj(x, w_in_t, col0=o_dt, n=LANES, tm=tmi)

    ch_head = jnp.arange(hw) // head_dim
    e1 = (jnp.arange(LANES)[:, None] == ch_head[None, :]).astype(BF16)
    e2 = (jnp.arange(LANES)[:, None] == (jnp.arange(n_heads * LANES) // LANES)[None, :]).astype(BF16)
    convb = ssd_conv_b.reshape(1, conv_dim)
    dtb, alog = _pad_lanes(ssd_dt_bias), _pad_lanes(ssd_a_log)
    de = jnp.repeat(ssd_d.astype(F32), head_dim).reshape(1, hw)
    normw = ssd_norm_w.reshape(1, hw)
    shp = dict(hw=hw, gn=gn, n_groups=n_groups, n_state=n_state, col=col)
    ya_p, h_p = _ssd_prompt(proj, dtraw, ssd_conv_w, convb, dtb, alog, de, normw, e1, e2,
                            batch=batch, seq=seq, head_dim=head_dim, **shp)
    cst = jnp.transpose(conv_buf, (1, 0, 2))
    ya, h_s = _ssd_sample(proj, dtraw, cst, ssm_h.reshape(nb, hw, n_state), ssd_conv_w, convb,
                          dtb, alog, de, normw, e1, ya_p, row0=n_prompt, nb=nb, **shp)

    yb_p, sc_tail = _sc_prompt(proj2, sc_conv_w, batch=batch, seq=seq, width=scw, col=col2,
                               ts=min(seq, 256))
    yb, ch_s = _sc_sample(proj2, jnp.transpose(sc_buf, (1, 0, 2)), sc_conv_w, yb_p,
                          row0=n_prompt, nb=nb, width=scw, col=col2)

    mix = _branch(ya, yb, w_branch_ssd.astype(BF16), w_branch_sc.astype(BF16), proj2,
                  col=col2, tm=tm, tn=_tile(d, 1024, LANES))
    assert n_prompt % nb == 0
    ln1 = functools.partial(_x1, mix=mix, w=w_out.astype(BF16), g=ln1_g.reshape(1, d), b=ln1_b.reshape(1, d),
                            alpha=alpha)
    tmp = PEER_TOKEN_TILE
    mp = m + (-m % tmp)
    fused_pad = n_prompt % (nb + mp - m) == 0
    x1_parts = ln1(x_p, prev=(), row0=0, tm=_tile(n_prompt, FINAL_TILE, BF16_ROWS), mp=mp if fused_pad else m)
    x1, x1p = ln1(x_s, prev=tuple(x1_parts), row0=n_prompt, tm=nb, mp=mp if fused_pad else m, pad_tail=fused_pad)
    if not fused_pad:
        x1p = jnp.pad(x1p, ((0, _words(mp - m)), (0, 0)))

    route = _route(x1p, peer_wq.astype(BF16), peer_keys1.astype(BF16), peer_keys2.astype(BF16), m=mp, tm=tmp)
    peer = _experts(x1p, _pack_rows(peer_u), _pack_rows(peer_v, transpose=True), *route,
                    m=mp, tm=tmp, te=EXPERT_TILE)

    fin = functools.partial(_final, x1, peer, wg=ple_gate_w.astype(BF16), wp=ple_proj_w.astype(BF16),
                            g=ln2_g.reshape(1, d), b=ln2_b.reshape(1, d), alpha=alpha)
    y_p = fin(p=p_p, row0=0, tm=_tile(n_prompt, FINAL_TILE, SUBLANES), per_step=False)
    y_s = fin(p=p_s, row0=n_prompt, tm=nb, per_step=True)

    k_ssd = ssd_conv_w.shape[0] - 1
    k_sc = sc_conv_w.shape[0] - 1
    xbc_cols = lambda rows: rows[..., hw:hw + conv_dim]
    conv_p = xbc_cols(jnp.stack([proj[(b + 1) * seq - k_ssd:(b + 1) * seq] for b in range(batch)]))
    conv_s = jnp.concatenate([conv_buf[:, 1:, :], xbc_cols(proj[n_prompt:])[:, None, :]], axis=1)
    sc_p = sc_tail[:, SUBLANES - k_sc:, :]
    sc_s = jnp.concatenate([sc_buf[:, 1:, :], ch_s[:, None, :]], axis=1)
    hshape = (-1, n_heads, head_dim, n_state)
    return y_p, y_s, conv_p, h_p.reshape(hshape), sc_p, conv_s, h_s.reshape(hshape), sc_s


def kernel(x_prompt, x_sample, p_prompt, p_sample, state_ssm, state_ssd_conv, state_shortconv, w_in, ssd_conv_w, ssd_conv_b, ssd_dt_bias, ssd_a_log, ssd_d, ssd_norm_w, sc_conv_w, w_branch_ssd, w_branch_sc, w_out, ln1_g, ln1_b, peer_wq, peer_keys1, peer_keys2, peer_u, peer_v, ln2_g, ln2_b, ple_gate_w, ple_proj_w):
    batch, seq, d = x_prompt.shape
    nb, dec_seq, _ = x_sample.shape
    assert dec_seq == 1 and seq % SSD_CHUNK == 0
    depth = w_in.shape[0]
    n_prompt = batch * seq
    x_p, x_s = x_prompt.reshape(n_prompt, d), x_sample.reshape(nb, d)
    weights = (w_in, ssd_conv_w, ssd_conv_b, ssd_dt_bias, ssd_a_log, ssd_d, ssd_norm_w,
               sc_conv_w, w_branch_ssd, w_branch_sc, w_out, ln1_g, ln1_b,
               peer_wq, peer_keys1, peer_keys2, peer_u, peer_v, ln2_g, ln2_b,
               ple_gate_w, ple_proj_w)
    outs = [[] for _ in range(6)]
    for i in range(depth):
        y_p, y_s, conv_p, h_p, sc_p, conv_s, h_s, sc_s = _layer(
            x_p, x_s, p_prompt[i].reshape(n_prompt, -1), p_sample[i].reshape(nb, -1),
            state_ssm[i], state_ssd_conv[i], state_shortconv[i], batch, seq, depth,
            *[w[i] for w in weights])
        for lst, val in zip(outs, (h_p, conv_p, sc_p, h_s, conv_s, sc_s)):
            lst.append(val)
        x_p, x_s = y_p, y_s.reshape(nb, d)
    y_prompt = y_p.reshape(batch, seq, d)
    y_sample = y_s
    return (y_prompt, y_sample) + tuple(jnp.stack(lst) for lst in outs)
```

```python
import functools
import math

import jax
import jax.numpy as jnp
from jax import lax
from jax.experimental import pallas as pl
from jax.experimental.pallas import tpu as pltpu

F32 = jnp.float32
BF16 = jnp.bfloat16
U32 = jnp.uint32

LANES = 128
SUBLANES = 8
BF16_ROWS = 16
PEER_TOPK = 16
SSD_CHUNK = 128
LN_EPS = 1e-5
RMS_EPS = 1e-5
VMEM_LIMIT = 56 * 1024 * 1024

NT_DIMS = (((1,), (1,)), ((), ()))
TN_DIMS = (((0,), (0,)), ((), ()))


def _cparams(*sem):
    return pltpu.CompilerParams(dimension_semantics=sem, vmem_limit_bytes=VMEM_LIMIT)


def _dot(a, b):
    return jnp.dot(a, b, preferred_element_type=F32)


def _split3(v):
    hi = v.astype(BF16)
    r = v - hi.astype(F32)
    mid = r.astype(BF16)
    lo = (r - mid.astype(F32)).astype(BF16)
    return hi, mid, lo


def _dot3_lhs(v, rhs_bf16):
    hi, mid, lo = _split3(v)
    return _dot(hi, rhs_bf16) + _dot(mid, rhs_bf16) + _dot(lo, rhs_bf16)


def _dot3_rhs(lhs_bf16, v):
    hi, mid, lo = _split3(v)
    return _dot(lhs_bf16, hi) + _dot(lhs_bf16, mid) + _dot(lhs_bf16, lo)


def _words(rows):
    return rows * jnp.dtype(BF16).itemsize // jnp.dtype(U32).itemsize


def _bf16_rows(words):
    return words * jnp.dtype(U32).itemsize // jnp.dtype(BF16).itemsize


def _sigmoid(x):
    return 1.0 / (1.0 + jnp.exp(-x))


def _silu(x):
    return x * _sigmoid(x)


def _softplus(x):
    return jnp.maximum(x, 0.0) + jnp.log1p(jnp.exp(-jnp.abs(x)))


def _layer_norm(x, g, b):
    mu = jnp.mean(x, axis=-1, keepdims=True)
    xc = x - mu
    var = jnp.mean(xc * xc, axis=-1, keepdims=True)
    return xc * lax.rsqrt(var + LN_EPS) * g + b


def _tile(m, cap, mult):
    best = None
    for t in range(mult, min(m, cap) + 1, mult):
        if m % t == 0:
            best = t
    assert best is not None, (m, cap, mult)
    return best


def _in_proj_kernel(x_ref, wa_ref, wb_ref, o_ref, *, shift):
    tn = wa_ref.shape[0]
    if shift:
        w = jnp.concatenate([wa_ref[...], wb_ref[...]], axis=0)[shift:shift + tn]
    else:
        w = wa_ref[...]
    o_ref[...] = lax.dot_general(x_ref[...], w.astype(BF16), NT_DIMS, preferred_element_type=F32)


def _in_proj(x, wt, *, col0, n, tm):
    m, k = x.shape
    shift = col0 % LANES
    assert shift % SUBLANES == 0
    base = col0 - shift
    tn = _tile(math.gcd(n, base) if base else n, PROJ_COLS, LANES)
    last = pl.cdiv(wt.shape[0], LANES) - 1
    return pl.pallas_call(
        functools.partial(_in_proj_kernel, shift=shift),
        grid=(m // tm, n // tn),
        in_specs=[pl.BlockSpec((tm, k), lambda i, j: (i, 0)),
                  pl.BlockSpec((tn, k), lambda i, j: (base // tn + j, 0)),
                  pl.BlockSpec((LANES, k), lambda i, j: (jnp.minimum((base + (j + 1) * tn) // LANES, last), 0))],
        out_specs=pl.BlockSpec((tm, tn), lambda i, j: (i, j)),
        out_shape=jax.ShapeDtypeStruct((m, n), F32),
        compiler_params=_cparams("parallel", "parallel"),
        name="in_proj",
    )(x, wt, wt)


def _ssd_gate_norm(y, z, normw_ref, out_ref, n_groups):
    y = y * _silu(z)
    gw = y.shape[1] // n_groups
    for g in range(n_groups):
        sl = slice(g * gw, (g + 1) * gw)
        yg = y[:, sl]
        ms = jnp.mean(yg * yg, axis=-1, keepdims=True)
        out_ref[:, sl] = (yg * lax.rsqrt(ms + RMS_EPS) * normw_ref[:, sl]).astype(out_ref.dtype)


def _ssd_prompt_kernel(z_ref, xs_ref, b_ref, c_ref, dt_ref, convw_ref, convb_ref, dtb_ref,
                       alog_ref, de_ref, normw_ref, e1_ref, e2_ref,
                       ya_ref, hout_ref,
                       cbuf, act, h_scr, x_scr, xd_scr, eae_scr, acsb_scr, acst_scr, y_scr,
                       *, n_groups, n_state, head_dim):
    c = pl.program_id(1)
    L = SSD_CHUNK
    hw = xs_ref.shape[1]
    gn = b_ref.shape[1]
    w_all = hw + 2 * gn
    n_heads = hw // head_dim
    hpl = LANES // head_dim
    n_blk = n_heads // hpl
    blk_per_group = n_blk // n_groups
    taps = convw_ref.shape[0]

    @pl.when(c == 0)
    def _():
        h_scr[...] = jnp.zeros_like(h_scr)
        cbuf[0:SUBLANES, :] = jnp.zeros((SUBLANES, w_all), F32)

    cbuf[SUBLANES:SUBLANES + L, 0:hw] = xs_ref[...]
    cbuf[SUBLANES:SUBLANES + L, hw:hw + gn] = b_ref[...]
    cbuf[SUBLANES:SUBLANES + L, hw + gn:] = c_ref[...]

    cw = math.gcd(w_all, 512)
    for blk in range(w_all // cw):
        sl = slice(blk * cw, (blk + 1) * cw)
        acc = convb_ref[:, sl] + convw_ref[taps - 1:taps, sl] * cbuf[SUBLANES:SUBLANES + L, sl]
        for j in range(1, taps):
            acc = acc + convw_ref[taps - 1 - j:taps - j, sl] * cbuf[SUBLANES - j:SUBLANES - j + L, sl]
        act[:, sl] = _silu(acc)
    cbuf[0:SUBLANES, :] = cbuf[L:L + SUBLANES, :]

    dt = _softplus(dt_ref[...] + dtb_ref[...])
    a_neg = -jnp.exp(alog_ref[...])
    dta = dt * a_neg
    ri = lax.broadcasted_iota(jnp.int32, (L, L), 0)
    ci = lax.broadcasted_iota(jnp.int32, (L, L), 1)
    causal = ri >= ci
    tri = jnp.where(causal, 1.0, 0.0).astype(BF16)
    acs = _dot3_rhs(tri, dta)
    e1 = e1_ref[...]
    dte = _dot3_lhs(dt, e1)
    acs_p = _split3(acs)
    acse = _dot(acs_p[0], e1) + _dot(acs_p[1], e1) + _dot(acs_p[2], e1)
    e2 = e2_ref[...]
    acsb_scr[...] = _dot(acs_p[0], e2) + _dot(acs_p[1], e2) + _dot(acs_p[2], e2)
    acst_scr[...] = acs.T

    xdt = act[:, 0:hw] * dte
    x_scr[...] = xdt.astype(BF16)
    xd_scr[...] = (xdt * jnp.exp(acse[L - 1:L, :] - acse)).astype(BF16)
    eae_scr[...] = jnp.exp(acse)

    lane = lax.broadcasted_iota(jnp.int32, (L, LANES), 1)
    cb = None
    for j in range(n_blk):
        g = j // blk_per_group
        bsl = slice(hw + g * n_state, hw + (g + 1) * n_state)
        csl = slice(hw + gn + g * n_state, hw + gn + (g + 1) * n_state)
        bg = act[:, bsl].astype(BF16)
        cg = act[:, csl].astype(BF16)
        if j % blk_per_group == 0:
            cb = lax.dot_general(cg, bg, NT_DIMS, preferred_element_type=F32)
        psl = slice(j * LANES, (j + 1) * LANES)
        xp = x_scr[:, psl]
        ydiag = None
        cds = []
        for q in range(hpl):
            r = j * hpl + q
            ab = acsb_scr[:, r * LANES:(r + 1) * LANES]
            at = jnp.broadcast_to(acst_scr[r:r + 1, :], (L, L))
            lm = jnp.where(causal, jnp.exp(ab - at), 0.0)
            m = (cb * lm).astype(BF16)
            inhead = (lane >= q * head_dim) & (lane < (q + 1) * head_dim)
            xq = jnp.where(inhead, xp, jnp.zeros_like(xp))
            yq = _dot(m, xq)
            ydiag = yq if ydiag is None else ydiag + yq
            cds.append(jnp.broadcast_to(jnp.exp(acsb_scr[L - 1:L, r * LANES:(r + 1) * LANES]),
                                        (head_dim, LANES)))
        cd = jnp.concatenate(cds, axis=0)
        hp = h_scr[psl, :]
        yoff = lax.dot_general(cg, hp.astype(BF16), NT_DIMS, preferred_element_type=F32)
        yoff = yoff * eae_scr[:, psl]
        st = lax.dot_general(xd_scr[:, psl], bg, TN_DIMS, preferred_element_type=F32)
        h_scr[psl, :] = hp * cd + st
        y_scr[:, psl] = ydiag + yoff + act[:, psl] * de_ref[:, psl]

    _ssd_gate_norm(y_scr[...], z_ref[...], normw_ref, ya_ref, n_groups)

    @pl.when(c == pl.num_programs(1) - 1)
    def _():
        hout_ref[0] = h_scr[...]


def _ssd_prompt(proj, dtraw, convw, convb, dtb, alog, de, normw, e1, e2, *, batch, seq,
                hw, gn, n_groups, n_state, head_dim, col):
    L = SSD_CHUNK
    nc = seq // L
    n_heads = hw // head_dim
    w_all = hw + 2 * gn
    row = lambda b, c: b * nc + c
    const = lambda b, c: (0, 0)
    kern = functools.partial(_ssd_prompt_kernel, n_groups=n_groups, n_state=n_state, head_dim=head_dim)
    return pl.pallas_call(
        kern,
        grid=(batch, nc),
        in_specs=[
            pl.BlockSpec((L, hw), lambda b, c: (row(b, c), col["z"] // hw)),
            pl.BlockSpec((L, hw), lambda b, c: (row(b, c), col["xs"] // hw)),
            pl.BlockSpec((L, gn), lambda b, c: (row(b, c), col["B"] // gn)),
            pl.BlockSpec((L, gn), lambda b, c: (row(b, c), col["C"] // gn)),
            pl.BlockSpec((L, LANES), lambda b, c: (row(b, c), 0)),
            pl.BlockSpec(convw.shape, const),
            pl.BlockSpec(convb.shape, const),
            pl.BlockSpec(dtb.shape, const),
            pl.BlockSpec(alog.shape, const),
            pl.BlockSpec(de.shape, const),
            pl.BlockSpec(normw.shape, const),
            pl.BlockSpec(e1.shape, const),
            pl.BlockSpec(e2.shape, const),
        ],
        out_specs=[
            pl.BlockSpec((L, hw), lambda b, c: (row(b, c), 0)),
            pl.BlockSpec((1, hw, n_state), lambda b, c: (b, 0, 0)),
        ],
        out_shape=[
            jax.ShapeDtypeStruct((proj.shape[0], hw), BF16),
            jax.ShapeDtypeStruct((batch, hw, n_state), F32),
        ],
        scratch_shapes=[
            pltpu.VMEM((L + SUBLANES, w_all), F32),
            pltpu.VMEM((L, w_all), F32),
            pltpu.VMEM((hw, n_state), F32),
            pltpu.VMEM((L, hw), BF16),
            pltpu.VMEM((L, hw), BF16),
            pltpu.VMEM((L, hw), F32),
            pltpu.VMEM((L, n_heads * LANES), F32),
            pltpu.VMEM((L, L), F32),
            pltpu.VMEM((L, hw), F32),
        ],
        compiler_params=_cparams("parallel", "arbitrary"),
        name="ssd_prompt",
    )(proj, proj, proj, proj, dtraw, convw, convb, dtb, alog, de, normw, e1, e2)


def _ssd_sample_kernel(z_ref, xs_ref, b_ref, c_ref, dt_ref, cst_ref, h_ref, convw_ref, convb_ref,
                       dtb_ref, alog_ref, de_ref, normw_ref, e1_ref, ya_all_ref,
                       ya_ref, hout_ref, xbc, y_scr,
                       *, n_groups, n_state):
    bb = xs_ref.shape[0]
    hw = xs_ref.shape[1]
    gn = b_ref.shape[1]
    taps = convw_ref.shape[0]
    gw = hw // n_groups

    xbc[:, 0:hw] = xs_ref[...]
    xbc[:, hw:hw + gn] = b_ref[...]
    xbc[:, hw + gn:] = c_ref[...]
    acc = convb_ref[...] + convw_ref[taps - 1:taps, :] * xbc[...]
    for j in range(taps - 1):
        acc = acc + convw_ref[j:j + 1, :] * cst_ref[j]
    act = _silu(acc)
    xs = act[:, 0:hw]

    dt = _softplus(dt_ref[...] + dtb_ref[...])
    dec = jnp.exp(dt * (-jnp.exp(alog_ref[...])))
    e1 = e1_ref[...]
    dte = _dot3_lhs(dt, e1)
    dece = _dot3_lhs(dec, e1)
    xdt = xs * dte

    pieces = [p.astype(F32) for p in _split3(dece)] + [p.astype(F32) for p in _split3(xdt)]
    npc = len(pieces)
    stack = jnp.concatenate(pieces + [jnp.zeros((LANES - npc * bb, hw), F32)], axis=0)
    lt = stack.T.astype(BF16)

    krow = lax.broadcasted_iota(jnp.int32, (LANES, LANES), 0)
    rowid = lax.broadcasted_iota(jnp.int32, (bb, gw), 0)
    half = npc // 2
    y_scr[...] = jnp.zeros_like(y_scr)
    for s in range(bb):
        is_s = (krow % bb) == s
        sel_dec = jnp.where(is_s & (krow < half * bb), 1.0, 0.0).astype(BF16)
        sel_x = jnp.where(is_s & (krow >= half * bb) & (krow < npc * bb), 1.0, 0.0).astype(BF16)
        dec_b = _dot(lt, sel_dec)
        x_b = _dot(lt, sel_x)
        for g in range(n_groups):
            rows = slice(g * gw, (g + 1) * gw)
            brow = act[s:s + 1, hw + g * n_state:hw + (g + 1) * n_state]
            hn = h_ref[s, rows, :] * dec_b[rows, :] + x_b[rows, :] * brow
            hout_ref[s, rows, :] = hn
            cg = act[:, hw + gn + g * n_state:hw + gn + (g + 1) * n_state].astype(BF16)
            yg = lax.dot_general(cg, hn.astype(BF16), NT_DIMS, preferred_element_type=F32)
            y_scr[:, rows] = y_scr[:, rows] + jnp.where(rowid == s, yg, 0.0)

    y = y_scr[...] + xs * de_ref[...]
    _ssd_gate_norm(y, z_ref[...], normw_ref, ya_ref, n_groups)


def _ssd_sample(proj, dtraw, cst, h0, convw, convb, dtb, alog, de, normw, e1, ya_all, *, row0, nb,
                hw, gn, n_groups, n_state, col):
    bb = SUBLANES
    w_all = hw + 2 * gn
    r0 = row0 // bb
    const = lambda i: (0, 0)
    kern = functools.partial(_ssd_sample_kernel, n_groups=n_groups, n_state=n_state)
    return pl.pallas_call(
        kern,
        grid=(nb // bb,),
        in_specs=[
            pl.BlockSpec((bb, hw), lambda i: (r0 + i, col["z"] // hw)),
            pl.BlockSpec((bb, hw), lambda i: (r0 + i, col["xs"] // hw)),
            pl.BlockSpec((bb, gn), lambda i: (r0 + i, col["B"] // gn)),
            pl.BlockSpec((bb, gn), lambda i: (r0 + i, col["C"] // gn)),
            pl.BlockSpec((bb, LANES), lambda i: (r0 + i, 0)),
            pl.BlockSpec((cst.shape[0], bb, w_all), lambda i: (0, i, 0)),
            pl.BlockSpec((bb, hw, n_state), lambda i: (i, 0, 0)),
            pl.BlockSpec(convw.shape, const),
            pl.BlockSpec(convb.shape, const),
            pl.BlockSpec(dtb.shape, const),
            pl.BlockSpec(alog.shape, const),
            pl.BlockSpec(de.shape, const),
            pl.BlockSpec(normw.shape, const),
            pl.BlockSpec(e1.shape, const),
            pl.BlockSpec(memory_space=pl.ANY),
        ],
        out_specs=[
            pl.BlockSpec((bb, hw), lambda i: (r0 + i, 0)),
            pl.BlockSpec((bb, hw, n_state), lambda i: (i, 0, 0)),
        ],
        out_shape=[
            jax.ShapeDtypeStruct(ya_all.shape, BF16),
            jax.ShapeDtypeStruct((nb, hw, n_state), F32),
        ],
        scratch_shapes=[pltpu.VMEM((bb, w_all), F32), pltpu.VMEM((bb, hw), F32)],
        compiler_params=_cparams("parallel"),
        input_output_aliases={14: 0},
        name="ssd_sample",
    )(proj, proj, proj, proj, dtraw, cst, h0, convw, convb, dtb, alog, de, normw, e1, ya_all)


def _sc_prompt_kernel(b_ref, c_ref, h_ref, w_ref, yb_ref, tail_ref, cbuf):
    j = pl.program_id(1)
    ts = b_ref.shape[0]
    taps = w_ref.shape[0]

    @pl.when(j == 0)
    def _():
        cbuf[0:SUBLANES, :] = jnp.zeros((SUBLANES, cbuf.shape[1]), F32)

    cbuf[SUBLANES:SUBLANES + ts, :] = c_ref[...] * h_ref[...]
    u = w_ref[taps - 1:taps, :] * cbuf[SUBLANES:SUBLANES + ts, :]
    for k in range(1, taps):
        u = u + w_ref[taps - 1 - k:taps - k, :] * cbuf[SUBLANES - k:SUBLANES - k + ts, :]
    yb_ref[...] = (b_ref[...] * u).astype(yb_ref.dtype)
    cbuf[0:SUBLANES, :] = cbuf[ts:ts + SUBLANES, :]

    @pl.when(j == pl.num_programs(1) - 1)
    def _():
        tail_ref[0] = cbuf[0:SUBLANES, :]


def _sc_prompt(proj, w, *, batch, seq, width, col, ts):
    nt = seq // ts
    row = lambda b, j: b * nt + j
    return pl.pallas_call(
        _sc_prompt_kernel,
        grid=(batch, nt),
        in_specs=[
            pl.BlockSpec((ts, width), lambda b, j: (row(b, j), col["sc_b"] // width)),
            pl.BlockSpec((ts, width), lambda b, j: (row(b, j), col["sc_c"] // width)),
            pl.BlockSpec((ts, width), lambda b, j: (row(b, j), col["sc_h"] // width)),
            pl.BlockSpec(w.shape, lambda b, j: (0, 0)),
        ],
        out_specs=[
            pl.BlockSpec((ts, width), lambda b, j: (row(b, j), 0)),
            pl.BlockSpec((1, SUBLANES, width), lambda b, j: (b, 0, 0)),
        ],
        out_shape=[
            jax.ShapeDtypeStruct((proj.shape[0], width), BF16),
            jax.ShapeDtypeStruct((batch, SUBLANES, width), F32),
        ],
        scratch_shapes=[pltpu.VMEM((ts + SUBLANES, width), F32)],
        compiler_params=_cparams("parallel", "arbitrary"),
        name="shortconv_prompt",
    )(proj, proj, proj, w)


def _sc_sample_kernel(b_ref, c_ref, h_ref, st_ref, w_ref, yb_all_ref, yb_ref, ch_ref):
    taps = w_ref.shape[0]
    ch = c_ref[...] * h_ref[...]
    u = w_ref[taps - 1:taps, :] * ch
    for k in range(taps - 1):
        u = u + w_ref[k:k + 1, :] * st_ref[k]
    yb_ref[...] = (b_ref[...] * u).astype(yb_ref.dtype)
    ch_ref[...] = ch


def _sc_sample(proj, st, w, yb_all, *, row0, nb, width, col):
    r0 = row0 // nb
    return pl.pallas_call(
        _sc_sample_kernel,
        grid=(1,),
        in_specs=[
            pl.BlockSpec((nb, width), lambda i: (r0, col["sc_b"] // width)),
            pl.BlockSpec((nb, width), lambda i: (r0, col["sc_c"] // width)),
            pl.BlockSpec((nb, width), lambda i: (r0, col["sc_h"] // width)),
            pl.BlockSpec(st.shape, lambda i: (0, 0, 0)),
            pl.BlockSpec(w.shape, lambda i: (0, 0)),
            pl.BlockSpec(memory_space=pl.ANY),
        ],
        out_specs=[pl.BlockSpec((nb, width), lambda i: (r0, 0)),
                   pl.BlockSpec((nb, width), lambda i: (0, 0))],
        out_shape=[jax.ShapeDtypeStruct(yb_all.shape, BF16),
                   jax.ShapeDtypeStruct((nb, width), F32)],
        input_output_aliases={5: 0},
        compiler_params=_cparams("arbitrary"),
        name="shortconv_sample",
    )(proj, proj, proj, st, w, yb_all)


def _branch_kernel(ya_ref, yb_ref, wa_ref, wb_ref, ga_ref, gb_ref, o_ref):
    ta = _dot(ya_ref[...], wa_ref[...])
    tb = _dot(yb_ref[...], wb_ref[...])
    o_ref[...] = (_sigmoid(ga_ref[...]) * ta + _sigmoid(gb_ref[...]) * tb).astype(o_ref.dtype)


def _branch(ya, yb, wa, wb, proj, *, col, tm, tn):
    m, k = ya.shape
    n = wa.shape[1]
    return pl.pallas_call(
        _branch_kernel,
        grid=(m // tm, n // tn),
        in_specs=[
            pl.BlockSpec((tm, k), lambda i, j: (i, 0)),
            pl.BlockSpec((tm, yb.shape[1]), lambda i, j: (i, 0)),
            pl.BlockSpec((k, tn), lambda i, j: (0, j)),
            pl.BlockSpec((yb.shape[1], tn), lambda i, j: (0, j)),
            pl.BlockSpec((tm, tn), lambda i, j: (i, col["g_a"] // tn + j)),
            pl.BlockSpec((tm, tn), lambda i, j: (i, col["g_b"] // tn + j)),
        ],
        out_specs=pl.BlockSpec((tm, tn), lambda i, j: (i, j)),
        out_shape=jax.ShapeDtypeStruct((m, n), BF16),
        compiler_params=_cparams("parallel", "parallel"),
        name="branch_mix",
    )(ya, yb, wa, wb, proj, proj)


def _x1_kernel(x_ref, mix_ref, w_ref, g_ref, b_ref, *rest, alpha):
    o_ref, ob_ref = rest[-2:]
    t = alpha * x_ref[...] + _dot(mix_ref[...], w_ref[...])
    x1 = _layer_norm(t, g_ref[...], b_ref[...])
    o_ref[...] = x1
    nw = _words(x1.shape[0])
    ob_ref[0:nw, :] = pltpu.bitcast(x1.astype(BF16), U32)
    if ob_ref.shape[0] > nw:
        pad = jnp.zeros((_bf16_rows(ob_ref.shape[0] - nw), ob_ref.shape[1]), BF16)
        ob_ref[nw:, :] = pltpu.bitcast(pad, U32)


def _x1(x, mix, w, g, b, prev, *, alpha, row0, tm, mp, pad_tail=False):
    n, d = x.shape
    m = mix.shape[0]
    r0 = row0 // tm
    tb = tm + (mp - m if pad_tail else 0)
    assert row0 % tb == 0 and (not pad_tail or n == tm)
    const = lambda i: (0, 0)
    alias = [pl.BlockSpec(memory_space=pl.ANY)] * len(prev)
    return pl.pallas_call(
        functools.partial(_x1_kernel, alpha=alpha),
        grid=(n // tm,),
        in_specs=[
            pl.BlockSpec((tm, d), lambda i: (i, 0)),
            pl.BlockSpec((tm, d), lambda i: (r0 + i, 0)),
            pl.BlockSpec(w.shape, const),
            pl.BlockSpec(g.shape, const),
            pl.BlockSpec(b.shape, const),
        ] + alias,
        out_specs=[pl.BlockSpec((tm, d), lambda i: (r0 + i, 0)),
                   pl.BlockSpec((_words(tb), d), lambda i: (row0 // tb + i, 0))],
        out_shape=[jax.ShapeDtypeStruct((m, d), F32), jax.ShapeDtypeStruct((_words(mp), d), U32)],
        input_output_aliases={5 + k: k for k in range(len(prev))},
        compiler_params=_cparams("parallel"),
        name="x1_out_ln",
    )(x, mix, w, g, b, *prev)


def _sort_pairs(lo, hi):
    def merge(lo, hi, r):
        step = r * 2
        if step < hi - lo:
            yield from merge(lo, hi, step)
            yield from merge(lo + r, hi, step)
            yield from [(i, i + r) for i in range(lo + r, hi - r, step)]
        else:
            yield (lo, lo + r)
    if hi - lo >= 1:
        mid = lo + (hi - lo) // 2
        yield from _sort_pairs(lo, mid)
        yield from _sort_pairs(mid + 1, hi)
        yield from merge(lo, hi, 1)


def _exchange(v, i, j):
    v[i], v[j] = jnp.maximum(v[i], v[j]), jnp.minimum(v[i], v[j])


def _sublane_all(op, x):
    dist = SUBLANES // 2
    while dist:
        x = op(x, pltpu.roll(x, dist, 0))
        dist //= 2
    return x


def _top_sorted(s, k):
    assert s.shape[0] == k * SUBLANES and k & (k - 1) == 0
    v = [s[j * SUBLANES:(j + 1) * SUBLANES, :] for j in range(k)]
    for i, j in _sort_pairs(0, k - 1):
        _exchange(v, i, j)
    dist = SUBLANES // 2
    while dist:
        w = [pltpu.roll(x, dist, 0) for x in v]
        v = [jnp.maximum(v[j], w[k - 1 - j]) for j in range(k)]
        stride = k // 2
        while stride:
            for i in range(k):
                if not i & stride:
                    _exchange(v, i, i + stride)
            stride //= 2
        dist //= 2
    return v


def _prefix_count(pred, t):
    w = jnp.where
    m1 = pred(t[7])
    m2 = pred(w(m1, t[11], t[3]))
    m3 = pred(w(m1, w(m2, t[13], t[9]), w(m2, t[5], t[1])))
    m4 = pred(w(m1, w(m2, w(m3, t[14], t[12]), w(m3, t[10], t[8])),
                w(m2, w(m3, t[6], t[4]), w(m3, t[2], t[0]))))
    return (w(m1, 8.0, 0.0) + w(m2, 4.0, 0.0) + w(m3, 2.0, 0.0) + w(m4, 1.0, 0.0)
            + w(pred(t[15]), 1.0, 0.0))


def _route_kernel(x_ref, wq_ref, k1_ref, k2_ref, c1_ref, w1_ref, r2_ref, e2_ref, q_scr, s1_scr, s2_scr):
    n_heads, n_keys, dk = k1_ref.shape
    tm = q_scr.shape[0]
    K = PEER_TOPK
    S = SUBLANES
    assert K == 16 and n_keys == K * S
    sub = lax.broadcasted_iota(jnp.int32, (S, LANES), 0)
    q_scr[...] = _dot(pltpu.bitcast(x_ref[...], BF16), wq_ref[...]).astype(BF16)
    for h in range(n_heads):
        q1 = q_scr[:, (2 * h) * dk:(2 * h + 1) * dk]
        q2 = q_scr[:, (2 * h + 1) * dk:(2 * h + 2) * dk]
        s1_scr[...] = lax.dot_general(k1_ref[h], q1, NT_DIMS, preferred_element_type=F32)
        s2_scr[...] = lax.dot_general(k2_ref[h], q2, NT_DIMS, preferred_element_type=F32)
        for c in range(tm // LANES):
            cs = slice(c * LANES, (c + 1) * LANES)
            s1 = [s1_scr[j * S:(j + 1) * S, cs] for j in range(K)]
            s2 = [s2_scr[j * S:(j + 1) * S, cs] for j in range(K)]
            t1 = _top_sorted(s1_scr[:, cs], K)
            t2 = _top_sorted(s2_scr[:, cs], K)
            rows = lambda t: functools.reduce(lambda acc, b: jnp.where(sub == b, t[b], acc), range(1, S), t[0])
            t2lo, t2hi, t1hi = rows(t2[:S]), rows(t2[S:]), rows(t1[S:])
            cand = [t1[0] + t2lo, t1[0] + t2hi] + [t1[a] + t2lo for a in range(1, S)] + [t1hi + t2[0]]
            cur = cand
            for r in range(K):
                thr = _sublane_all(jnp.maximum, functools.reduce(jnp.maximum, cur))
                if r + 1 < K:
                    cur = [jnp.where(x == thr, -jnp.inf, x) for x in cur]
            top = t1[0] + t2[0]
            zsum = _sublane_all(jnp.add, sum(jnp.where(x >= thr, jnp.exp(x - top), 0.0) for x in cand))
            inv_z = 1.0 / zsum
            rank2 = []
            for j in range(K):
                rows8 = slice(j * S, (j + 1) * S)
                c1_ref[h, rows8, cs] = _prefix_count(lambda tv, x=s1[j]: x + tv >= thr, t2)
                w1_ref[h, rows8, cs] = jnp.exp(s1[j] - t1[0]) * inv_z
                rank2.append(_prefix_count(lambda tv, x=s2[j]: tv > x, t2))
            r2_ref[h, :, cs] = pltpu.bitcast(jnp.concatenate(rank2, axis=0).astype(BF16), U32)
            e2 = jnp.concatenate([jnp.exp(x - t2[0]) for x in s2], axis=0)
            e2_ref[h, :, cs] = pltpu.bitcast(e2.astype(BF16), U32)


def _route(x1p, wq, k1, k2, *, m, tm):
    d = x1p.shape[1]
    n_heads, n_keys, _ = k1.shape
    tok = lambda rows: pl.BlockSpec((n_heads, rows, tm), lambda i: (0, 0, i))
    shp = lambda rows, dt: jax.ShapeDtypeStruct((n_heads, rows, m), dt)
    return pl.pallas_call(
        _route_kernel,
        grid=(m // tm,),
        in_specs=[
            pl.BlockSpec((_words(tm), d), lambda i: (i, 0)),
            pl.BlockSpec(wq.shape, lambda i: (0, 0)),
            pl.BlockSpec(k1.shape, lambda i: (0, 0, 0)),
            pl.BlockSpec(k2.shape, lambda i: (0, 0, 0)),
        ],
        out_specs=[tok(n_keys), tok(n_keys), tok(_words(n_keys)), tok(_words(n_keys))],
        out_shape=[shp(n_keys, F32), shp(n_keys, F32), shp(_words(n_keys), U32), shp(_words(n_keys), U32)],
        scratch_shapes=[pltpu.VMEM((tm, wq.shape[1]), BF16),
                        pltpu.VMEM((n_keys, tm), F32), pltpu.VMEM((n_keys, tm), F32)],
        compiler_params=_cparams("parallel"),
        name="peer_route",
    )(x1p, wq, k1, k2)


def _expert_kernel(x_ref, u_ref, vt_ref, c1_ref, w1_ref, r2_ref, e2_ref, o_ref, acc, coef, hid):
    e = pl.program_id(1)
    n_heads, per, tm = c1_ref.shape
    n_keys = hid.shape[0] // per
    half = _words(n_keys)

    @pl.when(e == 0)
    def _():
        acc[...] = jnp.zeros_like(acc)

    xb = pltpu.bitcast(x_ref[...], BF16)
    te = hid.shape[0]
    for part in range(HID_PARTS):
        rows = slice(part * te // HID_PARTS, (part + 1) * te // HID_PARTS)
        wrows = slice(_words(rows.start), _words(rows.stop))
        hid[rows, :] = lax.dot_general(pltpu.bitcast(u_ref[wrows, :], BF16), xb, NT_DIMS,
                                       preferred_element_type=F32)
    for k in range(per):
        for c in range(tm // LANES):
            cs = slice(c * LANES, (c + 1) * LANES)
            gate = jnp.zeros((n_keys, LANES), BF16)
            for h in range(n_heads):
                cnt = c1_ref[h, k:k + 1, cs].astype(BF16)
                w1 = w1_ref[h, k:k + 1, cs].astype(BF16)
                r2 = pltpu.bitcast(r2_ref[h, :, cs], BF16)
                e2 = pltpu.bitcast(e2_ref[h, :, cs], BF16)
                gate = gate + jnp.where(r2 < cnt, e2 * w1, jnp.zeros_like(gate))
            hk = hid[k * n_keys:(k + 1) * n_keys, cs]
            gelu = 0.5 * hk * (1.0 + lax.erf(hk * (1.0 / math.sqrt(2.0))))
            coef[k * half:(k + 1) * half, cs] = pltpu.bitcast(gate * gelu.astype(BF16), U32)

    acc[...] += _dot(pltpu.bitcast(vt_ref[...], BF16), pltpu.bitcast(coef[...], BF16))

    @pl.when(e == pl.num_programs(1) - 1)
    def _():
        o_ref[...] = acc[...].T.astype(o_ref.dtype)


def _experts(x1p, u, vt, c1, w1, r2, e2, *, m, tm, te):
    d = x1p.shape[1]
    n_tiles = u.shape[0] // _words(te)
    n_heads, n_keys, _ = c1.shape
    tok = lambda rows: pl.BlockSpec((n_heads, rows, tm), lambda i, e: (0, 0, i))
    per = te // n_keys
    assert per % SUBLANES == 0
    row = pl.BlockSpec((n_heads, per, tm), lambda i, e: (0, e, i))
    return pl.pallas_call(
        _expert_kernel,
        grid=(m // tm, n_tiles),
        in_specs=[
            pl.BlockSpec((_words(tm), d), lambda i, e: (i, 0)),
            pl.BlockSpec((_words(te), d), lambda i, e: (e, 0)),
            pl.BlockSpec((_words(d), te), lambda i, e: (0, e)),
            row, row, tok(_words(n_keys)), tok(_words(n_keys)),
        ],
        out_specs=pl.BlockSpec((tm, d), lambda i, e: (i, 0)),
        out_shape=jax.ShapeDtypeStruct((m, d), BF16),
        scratch_shapes=[pltpu.VMEM((d, tm), F32),
                        pltpu.VMEM((_words(te), tm), U32),
                        pltpu.VMEM((te, tm), F32)],
        compiler_params=_cparams("parallel", "arbitrary"),
        name="peer_experts",
    )(x1p, u, vt, c1, w1, r2, e2)


def _final_kernel(x1_ref, peer_ref, p_ref, wg_ref, wp_ref, g_ref, b_ref, o_ref, *, alpha):
    x2 = _layer_norm(alpha * x1_ref[...] + peer_ref[...], g_ref[...], b_ref[...])
    gate = _sigmoid(_dot(x2.astype(BF16), wg_ref[...]))
    y = x2 + gate * _dot(p_ref[...].astype(BF16), wp_ref[...])
    if len(o_ref.shape) == 3:
        o_ref[:, 0, :] = y
    else:
        o_ref[...] = y


def _final(x1, peer, p, wg, wp, g, b, *, alpha, row0, tm, per_step):
    d = x1.shape[1]
    n = p.shape[0]
    r0 = row0 // tm
    rows = lambda i: (r0 + i, 0)
    const = lambda i: (0, 0)
    if per_step:
        out_shape, out_spec = (n, 1, d), pl.BlockSpec((tm, 1, d), lambda i: (i, 0, 0))
    else:
        out_shape, out_spec = (n, d), pl.BlockSpec((tm, d), lambda i: (i, 0))
    return pl.pallas_call(
        functools.partial(_final_kernel, alpha=alpha),
        grid=(n // tm,),
        in_specs=[
            pl.BlockSpec((tm, d), rows),
            pl.BlockSpec((tm, d), rows),
            pl.BlockSpec((tm, p.shape[1]), lambda i: (i, 0)),
            pl.BlockSpec(wg.shape, const),
            pl.BlockSpec(wp.shape, const),
            pl.BlockSpec(g.shape, const),
            pl.BlockSpec(b.shape, const),
        ],
        out_specs=out_spec,
        out_shape=jax.ShapeDtypeStruct(out_shape, F32),
        compiler_params=_cparams("parallel"),
        name="ln2_ple",
    )(x1, peer, p, wg, wp, g, b)


TOKEN_TILE = 640
PROJ_TILE = 1664
PROJ_COLS = 512
FINAL_TILE = 512
EXPERT_TILE = 1024
HID_PARTS = 2
PEER_TOKEN_TILE = 768


def _pack_kernel(w_ref, o_ref, *, transpose):
    w = w_ref[...]
    o_ref[...] = pltpu.bitcast((w.T if transpose else w).astype(BF16), U32)


def _pack_rows(w, transpose=False, tr=512):
    r, c = w.shape
    if transpose:
        out_shape, out_spec = (_words(c), r), pl.BlockSpec((_words(c), tr), lambda i: (0, i))
    else:
        out_shape, out_spec = (_words(r), c), pl.BlockSpec((_words(tr), c), lambda i: (i, 0))
    return pl.pallas_call(
        functools.partial(_pack_kernel, transpose=transpose),
        grid=(r // tr,),
        in_specs=[pl.BlockSpec((tr, c), lambda i: (i, 0))],
        out_specs=out_spec,
        out_shape=jax.ShapeDtypeStruct(out_shape, U32),
        compiler_params=_cparams("parallel"),
        name="pack_weight",
    )(w)


def _to_bf16_kernel(x_ref, *rest):
    rest[-1][...] = x_ref[...].astype(BF16)


def _to_bf16(x, prev, *, row0, m, tm):
    n, d = x.shape
    r0 = row0 // tm
    assert row0 % tm == 0
    return pl.pallas_call(
        _to_bf16_kernel,
        grid=(n // tm,),
        in_specs=[pl.BlockSpec((tm, d), lambda i: (i, 0))] + [pl.BlockSpec(memory_space=pl.ANY)] * len(prev),
        out_specs=pl.BlockSpec((tm, d), lambda i: (r0 + i, 0)),
        out_shape=jax.ShapeDtypeStruct((m, d), BF16),
        input_output_aliases={1 + k: k for k in range(len(prev))},
        compiler_params=_cparams("parallel"),
        name="x_bf16",
    )(x, *prev)


def _pad_lanes(v):
    return jnp.pad(v.astype(F32), (0, LANES - v.shape[0])).reshape(1, LANES)


def _layer(x_p, x_s, p_p, p_s, ssm_h, conv_buf, sc_buf, batch, seq, depth,
           w_in, ssd_conv_w, ssd_conv_b, ssd_dt_bias, ssd_a_log, ssd_d, ssd_norm_w,
           sc_conv_w, w_branch_ssd, w_branch_sc, w_out, ln1_g, ln1_b,
           peer_wq, peer_keys1, peer_keys2, peer_u, peer_v, ln2_g, ln2_b,
           ple_gate_w, ple_proj_w):
    n_prompt, d = x_p.shape
    nb = x_s.shape[0]
    m = n_prompt + nb
    x = _to_bf16(x_s, (_to_bf16(x_p, (), row0=0, m=m, tm=_tile(n_prompt, FINAL_TILE, BF16_ROWS)),),
                 row0=n_prompt, m=m, tm=nb)
    n_heads = ssd_dt_bias.shape[0]
    hw, n_state = ssm_h.shape[1] * ssm_h.shape[2], ssm_h.shape[3]
    head_dim = ssm_h.shape[2]
    conv_dim = ssd_conv_w.shape[1]
    gn = (conv_dim - hw) // 2
    n_groups = gn // n_state
    scw = sc_conv_w.shape[1]
    alpha = (2.0 * depth) ** 0.25
    assert hw == scw == d and n_heads <= LANES and LANES % head_dim == 0 and n_state == LANES

    o_dt = hw + conv_dim
    o_scb = o_dt + n_heads
    col = {"z": 0, "xs": hw, "B": 2 * hw, "C": 2 * hw + gn}
    col2 = {"sc_b": 0, "sc_c": scw, "sc_h": 2 * scw, "g_a": 3 * scw, "g_b": 3 * scw + d}
    tm = _tile(m, TOKEN_TILE, LANES)
    tmi = _tile(m, PROJ_TILE, LANES)
    w_in_t = w_in.T
    proj = _in_proj(x, w_in_t, col0=0, n=o_dt, tm=tmi)
    proj2 = _in_proj(x, w_in_t, col0=o_scb, n=3 * scw + 2 * d, tm=tmi)
    dtraw = _in_proj(x, w_in_t, col0=o_dt, n=LANES, tm=tmi)

    ch_head = jnp.arange(hw) // head_dim
    e1 = (jnp.arange(LANES)[:, None] == ch_head[None, :]).astype(BF16)
    e2 = (jnp.arange(LANES)[:, None] == (jnp.arange(n_heads * LANES) // LANES)[None, :]).astype(BF16)
    convb = ssd_conv_b.reshape(1, conv_dim)
    dtb, alog = _pad_lanes(ssd_dt_bias), _pad_lanes(ssd_a_log)
    de = jnp.repeat(ssd_d.astype(F32), head_dim).reshape(1, hw)
    normw = ssd_norm_w.reshape(1, hw)
    shp = dict(hw=hw, gn=gn, n_groups=n_groups, n_state=n_state, col=col)
    ya_p, h_p = _ssd_prompt(proj, dtraw, ssd_conv_w, convb, dtb, alog, de, normw, e1, e2,
                            batch=batch, seq=seq, head_dim=head_dim, **shp)
    cst = jnp.transpose(conv_buf, (1, 0, 2))
    ya, h_s = _ssd_sample(proj, dtraw, cst, ssm_h.reshape(nb, hw, n_state), ssd_conv_w, convb,
                          dtb, alog, de, normw, e1, ya_p, row0=n_prompt, nb=nb, **shp)

    yb_p, sc_tail = _sc_prompt(proj2, sc_conv_w, batch=batch, seq=seq, width=scw, col=col2,
                               ts=min(seq, 256))
    yb, ch_s = _sc_sample(proj2, jnp.transpose(sc_buf, (1, 0, 2)), sc_conv_w, yb_p,
                          row0=n_prompt, nb=nb, width=scw, col=col2)

    mix = _branch(ya, yb, w_branch_ssd.astype(BF16), w_branch_sc.astype(BF16), proj2,
                  col=col2, tm=tm, tn=_tile(d, 1024, LANES))
    assert n_prompt % nb == 0
    ln1 = functools.partial(_x1, mix=mix, w=w_out.astype(BF16), g=ln1_g.reshape(1, d), b=ln1_b.reshape(1, d),
                            alpha=alpha)
    tmp = PEER_TOKEN_TILE
    mp = m + (-m % tmp)
    fused_pad = n_prompt % (nb + mp - m) == 0
    x1_parts = ln1(x_p, prev=(), row0=0, tm=_tile(n_prompt, FINAL_TILE, BF16_ROWS), mp=mp if fused_pad else m)
    x1, x1p = ln1(x_s, prev=tuple(x1_parts), row0=n_prompt, tm=nb, mp=mp if fused_pad else m, pad_tail=fused_pad)
    if not fused_pad:
        x1p = jnp.pad(x1p, ((0, _words(mp - m)), (0, 0)))

    route = _route(x1p, peer_wq.astype(BF16), peer_keys1.astype(BF16), peer_keys2.astype(BF16), m=mp, tm=tmp)
    peer = _experts(x1p, _pack_rows(peer_u), _pack_rows(peer_v, transpose=True), *route,
                    m=mp, tm=tmp, te=EXPERT_TILE)

    fin = functools.partial(_final, x1, peer, wg=ple_gate_w.astype(BF16), wp=ple_proj_w.astype(BF16),
                            g=ln2_g.reshape(1, d), b=ln2_b.reshape(1, d), alpha=alpha)
    y_p = fin(p=p_p, row0=0, tm=_tile(n_prompt, FINAL_TILE, SUBLANES), per_step=False)
    y_s = fin(p=p_s, row0=n_prompt, tm=nb, per_step=True)

    k_ssd = ssd_conv_w.shape[0] - 1
    k_sc = sc_conv_w.shape[0] - 1
    xbc_cols = lambda rows: rows[..., hw:hw + conv_dim]
    conv_p = xbc_cols(jnp.stack([proj[(b + 1) * seq - k_ssd:(b + 1) * seq] for b in range(batch)]))
    conv_s = jnp.concatenate([conv_buf[:, 1:, :], xbc_cols(proj[n_prompt:])[:, None, :]], axis=1)
    sc_p = sc_tail[:, SUBLANES - k_sc:, :]
    sc_s = jnp.concatenate([sc_buf[:, 1:, :], ch_s[:, None, :]], axis=1)
    hshape = (-1, n_heads, head_dim, n_state)
    return y_p, y_s, conv_p, h_p.reshape(hshape), sc_p, conv_s, h_s.reshape(hshape), sc_s


def kernel(x_prompt, x_sample, p_prompt, p_sample, state_ssm, state_ssd_conv, state_shortconv, w_in, ssd_conv_w, ssd_conv_b, ssd_dt_bias, ssd_a_log, ssd_d, ssd_norm_w, sc_conv_w, w_branch_ssd, w_branch_sc, w_out, ln1_g, ln1_b, peer_wq, peer_keys1, peer_keys2, peer_u, peer_v, ln2_g, ln2_b, ple_gate_w, ple_proj_w):
    batch, seq, d = x_prompt.shape
    nb, dec_seq, _ = x_sample.shape
    assert dec_seq == 1 and seq % SSD_CHUNK == 0
    depth = w_in.shape[0]
    n_prompt = batch * seq
    x_p, x_s = x_prompt.reshape(n_prompt, d), x_sample.reshape(nb, d)
    weights = (w_in, ssd_conv_w, ssd_conv_b, ssd_dt_bias, ssd_a_log, ssd_d, ssd_norm_w,
               sc_conv_w, w_branch_ssd, w_branch_sc, w_out, ln1_g, ln1_b,
               peer_wq, peer_keys1, peer_keys2, peer_u, peer_v, ln2_g, ln2_b,
               ple_gate_w, ple_proj_w)
    outs = [[] for _ in range(6)]
    for i in range(depth):
        y_p, y_s, conv_p, h_p, sc_p, conv_s, h_s, sc_s = _layer(
            x_p, x_s, p_prompt[i].reshape(n_prompt, -1), p_sample[i].reshape(nb, -1),
            state_ssm[i], state_ssd_conv[i], state_shortconv[i], batch, seq, depth,
            *[w[i] for w in weights])
        for lst, val in zip(outs, (h_p, conv_p, sc_p, h_s, conv_s, sc_s)):
            lst.append(val)
        x_p, x_s = y_p, y_s.reshape(nb, d)
    y_prompt = y_p.reshape(batch, seq, d)
    y_sample = y_s
    return (y_prompt, y_sample) + tuple(jnp.stack(lst) for lst in outs)
```
